```python
import math
import jax, jax.numpy as jnp
from jax import lax
import numpy as np

D_MODEL = 1024
BATCH = 4
SEQ = 8192
DEPTH = 2

N_A_LAYERS = DEPTH // 2
N_B_LAYERS = DEPTH - N_A_LAYERS

EXPAND = 2
A_WIDTH = EXPAND * D_MODEL
A_GROUPS = 8
A_GROUP_DIM = A_WIDTH // A_GROUPS
CHUNK = 128

HEAD_DIM = 64
N_Q_HEADS = D_MODEL // HEAD_DIM
N_KV_HEADS = max(1, N_Q_HEADS // 8)
Q_PER_KV = N_Q_HEADS // N_KV_HEADS
B_WIDTH = N_Q_HEADS * HEAD_DIM
KV_WIDTH = N_KV_HEADS * HEAD_DIM
WINDOW = 128

REL_BUCKETS = 32
REL_MAX_DIST = 128

ALPHA = (2.0 * DEPTH) ** 0.25
BETA = (8.0 * DEPTH) ** -0.25
LN_EPS = 1e-5
NEG_INF = -1e30

kernel_name = "yoco_gmlp_swa_sink_hybrid"


def layer_norm(x, g, b):
    xf = x.astype(jnp.float32)
    mu = jnp.mean(xf, axis=-1, keepdims=True)
    var = jnp.mean(jnp.square(xf - mu), axis=-1, keepdims=True)
    y = (xf - mu) * lax.rsqrt(var + LN_EPS)
    return (y * g.astype(jnp.float32) + b.astype(jnp.float32)).astype(x.dtype)


def rel_bucket(d):
    max_exact = REL_BUCKETS // 2
    df = jnp.maximum(d, 1).astype(jnp.float32)
    large = max_exact + (jnp.log(df / max_exact) / math.log(REL_MAX_DIST / max_exact)
                         * (REL_BUCKETS - max_exact)).astype(jnp.int32)
    large = jnp.minimum(large, REL_BUCKETS - 1)
    return jnp.where(d < max_exact, d, large)


def sgu_branch(h, w_in, ln_g, ln_b, w_spatial, b_spatial, w_out):
    bsz, seq, _ = h.shape
    nc = seq // CHUNK
    u, v, z = jnp.split(h @ w_in, 3, axis=-1)
    v = layer_norm(v, ln_g, ln_b).reshape(bsz, nc, CHUNK, A_GROUPS, A_GROUP_DIM)
    tri = jnp.tril(jnp.ones((CHUNK, CHUNK), dtype=bool))
    ws = jnp.where(tri, w_spatial, jnp.zeros((), w_spatial.dtype))
    s = jnp.einsum("gts,bcsgd->bctgd", ws, v) + b_spatial.T[:, :, None]
    y = u * s.reshape(bsz, seq, A_WIDTH) * jax.nn.silu(z)
    return y @ w_out


def shared_kv_bands(h, w_kv):
    bsz, seq, _ = h.shape
    nc = seq // CHUNK
    k, v = jnp.split(h @ w_kv, 2, axis=-1)

    def band(t):
        t = t.reshape(bsz, seq, N_KV_HEADS, HEAD_DIM)
        prev = jnp.pad(t, ((0, 0), (CHUNK, 0), (0, 0), (0, 0)))[:, :seq]
        prev = prev.reshape(bsz, nc, CHUNK, N_KV_HEADS, HEAD_DIM)
        cur = t.reshape(bsz, nc, CHUNK, N_KV_HEADS, HEAD_DIM)
        return jnp.concatenate([prev, cur], axis=2)

    return band(k), band(v)


def band_bias_and_mask(rel_bias, nc):
    t = jnp.arange(CHUNK, dtype=jnp.int32)[:, None]
    j = jnp.arange(2 * CHUNK, dtype=jnp.int32)[None, :]
    d = t + CHUNK - j
    in_window = (d >= 0) & (d < WINDOW)
    bias = rel_bias[rel_bucket(jnp.clip(d, 0, REL_MAX_DIST - 1))]
    bias = jnp.transpose(bias, (2, 0, 1)).astype(jnp.float32)
    bias = bias.reshape(N_KV_HEADS, Q_PER_KV, 1, CHUNK, 2 * CHUNK)
    has_prev = (jnp.arange(nc)[:, None, None] > 0) | (j[None] >= CHUNK)
    mask = in_window[None] & has_prev
    return bias, mask


def swa_branch(h, k_band, v_band, bias, mask, w_in, sinks, w_out):
    bsz, seq, _ = h.shape
    nc = seq // CHUNK
    q, z = jnp.split(h @ w_in, 2, axis=-1)
    q = q.reshape(bsz, nc, CHUNK, N_KV_HEADS, Q_PER_KV, HEAD_DIM)
    scores = jnp.einsum("bcqkgd,bcjkd->bkgcqj", q, k_band)
    logits = jnp.where(mask, scores.astype(jnp.float32) * (HEAD_DIM ** -0.5) + bias, NEG_INF)
    sink = sinks.astype(jnp.float32).reshape(N_KV_HEADS, Q_PER_KV, 1, 1, 1)
    m = jnp.maximum(jnp.max(logits, axis=-1, keepdims=True), sink)
    e = jnp.exp(logits - m)
    p = e / (jnp.sum(e, axis=-1, keepdims=True) + jnp.exp(sink - m))
    o = jnp.einsum("bkgcqj,bcjkd->bcqkgd", p.astype(v_band.dtype), v_band)
    y = o.reshape(bsz, seq, B_WIDTH) * jax.nn.silu(z)
    return y @ w_out


def setup_inputs(seed: int = 0) -> dict:
    key = jax.random.key(seed)
    ks = jax.random.split(key, 16)
    f32 = jnp.float32
    nrm = lambda k, shape, s: jax.random.normal(k, shape, f32) * s
    return {
        "x": nrm(ks[0], (BATCH, SEQ, D_MODEL), 1.0),
        "w_in_a": nrm(ks[1], (N_A_LAYERS, D_MODEL, 3 * A_WIDTH), D_MODEL ** -0.5),
        "sgu_ln_g": 1.0 + nrm(ks[2], (N_A_LAYERS, A_WIDTH), 0.1),
        "sgu_ln_b": nrm(ks[3], (N_A_LAYERS, A_WIDTH), 0.1),
        "w_spatial": nrm(ks[4], (N_A_LAYERS, A_GROUPS, CHUNK, CHUNK), 0.1),
        "b_spatial": 1.0 + nrm(ks[5], (N_A_LAYERS, A_GROUPS, CHUNK), 0.1),
        "w_out_a": nrm(ks[6], (N_A_LAYERS, A_WIDTH, D_MODEL), BETA * A_WIDTH ** -0.5),
        "w_kv": nrm(ks[7], (D_MODEL, 2 * KV_WIDTH), D_MODEL ** -0.5),
        "w_in_b": nrm(ks[8], (N_B_LAYERS, D_MODEL, 2 * B_WIDTH), D_MODEL ** -0.5),
        "attn_sinks": nrm(ks[9], (N_B_LAYERS, N_Q_HEADS), 0.5),
        "rel_bias": nrm(ks[10], (REL_BUCKETS, N_Q_HEADS), 0.5),
        "w_out_b": nrm(ks[11], (N_B_LAYERS, B_WIDTH, D_MODEL), BETA * B_WIDTH ** -0.5),
        "post_ln_g": 1.0 + nrm(ks[12], (DEPTH, D_MODEL), 0.1),
        "post_ln_b": nrm(ks[13], (DEPTH, D_MODEL), 0.1),
    }


def reference(x, w_in_a, sgu_ln_g, sgu_ln_b, w_spatial, b_spatial, w_out_a, w_kv,
              w_in_b, attn_sinks, rel_bias, w_out_b, post_ln_g, post_ln_b):
    nc = x.shape[1] // CHUNK
    bias, mask = band_bias_and_mask(rel_bias, nc)
    h = x
    k_band = None
    v_band = None
    for layer in range(DEPTH):
        if layer < N_A_LAYERS:
            i = layer
            sub = sgu_branch(h, w_in_a[i], sgu_ln_g[i], sgu_ln_b[i], w_spatial[i],
                             b_spatial[i], w_out_a[i])
        else:
            i = layer - N_A_LAYERS
            if i == 0:
                k_band, v_band = shared_kv_bands(h, w_kv)
            sub = swa_branch(h, k_band, v_band, bias, mask, w_in_b[i], attn_sinks[i], w_out_b[i])
        h = layer_norm(ALPHA * h + sub, post_ln_g[layer], post_ln_b[layer])
    return h
```

```python
import functools
import math

import numpy as np
import jax
import jax.numpy as jnp
from jax import lax
from jax.experimental import pallas as pl
from jax.experimental.pallas import tpu as pltpu

F32 = jnp.float32
BF16 = jnp.bfloat16

CHUNK = 128
A_GROUPS = 8
HEAD_DIM = 64
Q_PER_KV = 8
REL_BUCKETS = 32
REL_MAX_DIST = 128
LN_EPS = 1e-5
NEG_INF = -1e30

LANES = 128
V7X_VMEM_BYTES = 64 * 1024 * 1024


def _bucket_lower_bounds():
    max_exact = REL_BUCKETS // 2
    d = np.arange(REL_MAX_DIST, dtype=np.int32)
    df = np.maximum(d, 1).astype(np.float32)
    large = max_exact + (np.log(df / np.float32(max_exact)) / np.float32(math.log(REL_MAX_DIST / max_exact))
                         * np.float32(REL_BUCKETS - max_exact)).astype(np.int32)
    large = np.minimum(large, REL_BUCKETS - 1)
    bucket = np.where(d < max_exact, d, large)
    assert np.all(np.diff(bucket) >= 0) and bucket[0] == 0
    return [int(np.argmax(bucket >= b)) if np.any(bucket >= b) else REL_MAX_DIST for b in range(REL_BUCKETS)]


def _band_mask_table():
    t = np.arange(CHUNK)[:, None]
    j = np.arange(2 * CHUNK)[None, :]
    d = t + CHUNK - j
    in_window = (d >= 0) & (d < CHUNK)
    first = in_window & (j >= CHUNK)
    tab = np.stack([np.tile(in_window, (1, 2)), np.tile(first, (1, 2))]).astype(np.float32)
    return tab


def _layer_norm(x, g, b):
    mu = jnp.mean(x, axis=-1, keepdims=True)
    d = x - mu
    var = jnp.mean(d * d, axis=-1, keepdims=True)
    return d * lax.rsqrt(var + LN_EPS) * g + b


def _silu(z):
    return z / (1.0 + jnp.exp(-z))


def _dot(a, b):
    return jnp.dot(a, b, preferred_element_type=F32)


def _rel_bias_kernel(rb_ref, out_ref, *, lower_bounds):
    p = pl.program_id(0)
    t = lax.broadcasted_iota(jnp.int32, (CHUNK, 2 * CHUNK), 0)
    j = lax.broadcasted_iota(jnp.int32, (CHUNK, 2 * CHUNK), 1)
    d = jnp.clip(t + CHUNK - j, 0, REL_MAX_DIST - 1)
    for e in range(2):
        h = 2 * p + e
        acc = jnp.full((CHUNK, 2 * CHUNK), rb_ref[0, h], F32)
        for b in range(1, REL_BUCKETS):
            if lower_bounds[b] < REL_MAX_DIST:
                acc = jnp.where(d >= lower_bounds[b], rb_ref[b, h], acc)
        out_ref[0, :, e * 2 * CHUNK:(e + 1) * 2 * CHUNK] = acc


def _rel_bias_table(rel_bias):
    n_heads = rel_bias.shape[1]
    n_pairs = n_heads // 2
    return pl.pallas_call(
        functools.partial(_rel_bias_kernel, lower_bounds=_bucket_lower_bounds()),
        grid=(n_pairs,),
        in_specs=[pl.BlockSpec(memory_space=pltpu.SMEM)],
        out_specs=pl.BlockSpec((1, CHUNK, 4 * CHUNK), lambda p: (p, 0, 0)),
        out_shape=jax.ShapeDtypeStruct((n_pairs, CHUNK, 4 * CHUNK), F32),
        name="rel_bias_table",
    )(rel_bias)


def _sgu_kernel(x_ref, w_in_ref, lng_ref, lnb_ref, ws_ref, bsp_ref, w_out_ref, w_kv_ref,
                pg_ref, pb_ref, h_ref, kv_ref, vn_s, y_s, *, alpha):
    tm, _ = x_ref.shape
    aw = vn_s.shape[1]
    gd = aw // A_GROUPS
    x = x_ref[...]
    xb = x.astype(BF16)

    v = _dot(xb, w_in_ref[:, aw:2 * aw])
    vn_s[...] = _layer_norm(v, lng_ref[...], lnb_ref[...]).astype(BF16)

    row = lax.broadcasted_iota(jnp.int32, (CHUNK, CHUNK), 0)
    col = lax.broadcasted_iota(jnp.int32, (CHUNK, CHUNK), 1)
    tril = col <= row
    for g in range(A_GROUPS):
        cols = slice(g * gd, (g + 1) * gd)
        u = _dot(xb, w_in_ref[:, g * gd:(g + 1) * gd])
        z = _dot(xb, w_in_ref[:, 2 * aw + g * gd:2 * aw + (g + 1) * gd])
        ws = jnp.where(tril, ws_ref[g], 0.0).astype(BF16)
        gate = u * _silu(z)
        for c in range(tm // CHUNK):
            rows = slice(c * CHUNK, (c + 1) * CHUNK)
            s = _dot(ws, vn_s[rows, cols]) + bsp_ref[g]
            y_s[rows, cols] = (gate[rows] * s).astype(BF16)

    sub = _dot(y_s[...], w_out_ref[...])
    h = _layer_norm(alpha * x + sub, pg_ref[...], pb_ref[...])
    h_ref[...] = h
    kv_ref[...] = _dot(h.astype(BF16), w_kv_ref[...]).astype(BF16)


def _sgu_layer(x2, w_in, ln_g, ln_b, w_spatial, bsp_b, w_out, w_kv, pg, pb, *, alpha, tm):
    t_tokens, d_model = x2.shape
    aw = w_out.shape[0]
    kvw = w_kv.shape[1]
    whole = pl.BlockSpec(memory_space=pltpu.VMEM)
    return pl.pallas_call(
        functools.partial(_sgu_kernel, alpha=alpha),
        grid=(t_tokens // tm,),
        in_specs=[pl.BlockSpec((tm, d_model), lambda i: (i, 0)),
                  whole, whole, whole, whole, whole, whole, whole, whole, whole],
        out_specs=[pl.BlockSpec((tm, d_model), lambda i: (i, 0)),
                   pl.BlockSpec((tm, kvw), lambda i: (i, 0))],
        out_shape=[jax.ShapeDtypeStruct((t_tokens, d_model), F32),
                   jax.ShapeDtypeStruct((t_tokens, kvw), BF16)],
        scratch_shapes=[pltpu.VMEM((tm, aw), BF16), pltpu.VMEM((tm, aw), BF16)],
        compiler_params=pltpu.CompilerParams(
            dimension_semantics=("arbitrary",),
            vmem_limit_bytes=V7X_VMEM_BYTES * 7 // 8),
        name="sgu_layer",
    )(x2, w_in, ln_g, ln_b, w_spatial, bsp_b, w_out, w_kv, pg, pb)


def _swa_kernel(h_ref, kv_ref, kvp_ref, w_in_ref, bias_ref, mask_ref, sink_ref, w_out_ref,
                pg_ref, pb_ref, o_ref, y_s, *, alpha, tiles_per_seq):
    tm, _ = h_ref.shape
    bw = y_s.shape[1]
    n_kv = bw // (HEAD_DIM * Q_PER_KV)
    pairs_per_kv = Q_PER_KV // 2
    i = pl.program_id(0)
    h = h_ref[...]
    hb = h.astype(BF16)
    q = (_dot(hb, w_in_ref[:, :bw]) * (HEAD_DIM ** -0.5)).astype(BF16)
    z = _dot(hb, w_in_ref[:, bw:])

    lane = lax.broadcasted_iota(jnp.int32, (2 * CHUNK, LANES), 1).astype(F32).astype(BF16)
    lo = lane < HEAD_DIM
    lo_q = lax.broadcasted_iota(jnp.int32, (CHUNK, LANES), 1) < HEAD_DIM
    zeros = jnp.zeros((2 * CHUNK, LANES), BF16)
    ones = jnp.ones((2 * CHUNK, LANES), BF16)
    ones_lo = jnp.where(lo, ones, zeros)
    ones_hi = jnp.where(lo, zeros, ones)
    first = (i % tiles_per_seq == 0).astype(jnp.int32)

    for c in range(tm // CHUNK):
        rows = slice(c * CHUNK, (c + 1) * CHUNK)
        kv_prev = kvp_ref[...] if c == 0 else kv_ref[(c - 1) * CHUNK:c * CHUNK, :]
        band = jnp.concatenate([kv_prev, kv_ref[rows, :]], axis=0)
        kb, vb = band[:, :LANES], band[:, LANES:]
        kr = pltpu.roll(kb, HEAD_DIM, axis=1)
        vr = pltpu.roll(vb, HEAD_DIM, axis=1)
        mask = (mask_ref[first] if c == 0 else mask_ref[0]) > 0.5
        for kvh in range(n_kv):
            k_src, k_alt = (kb, kr) if kvh == 0 else (kr, kb)
            v_src, v_alt = (vb, vr) if kvh == 0 else (vr, vb)
            k_cat = jnp.concatenate([jnp.where(lo, k_src, zeros), jnp.where(lo, zeros, k_alt)], axis=0)
            v_rhs = jnp.concatenate(
                [jnp.concatenate([jnp.where(lo, v_src, zeros), ones_lo], axis=1),
                 jnp.concatenate([jnp.where(lo, zeros, v_alt), ones_hi], axis=1)], axis=0)
            pair0 = kvh * pairs_per_kv
            q_stack = jnp.concatenate(
                [q[rows, (pair0 + p) * LANES:(pair0 + p + 1) * LANES] for p in range(pairs_per_kv)], axis=0)
            s = lax.dot_general(q_stack, k_cat, (((1,), (1,)), ((), ())),
                                preferred_element_type=F32)
            e_blocks, sink_terms = [], []
            for p in range(pairs_per_kv):
                pr = slice(p * CHUNK, (p + 1) * CHUNK)
                logit = jnp.where(mask, s[pr] + bias_ref[pair0 + p], NEG_INF)
                halves, sink_half = [], []
                for e in range(2):
                    sink = sink_ref[2 * (pair0 + p) + e]
                    le = logit[:, e * 2 * CHUNK:(e + 1) * 2 * CHUNK]
                    m = jnp.maximum(jnp.max(le, axis=-1, keepdims=True), sink)
                    halves.append(jnp.exp(le - m))
                    sink_half.append(jnp.broadcast_to(jnp.exp(sink - m), (CHUNK, LANES)))
                e_blocks.append(jnp.concatenate(halves, axis=1).astype(BF16))
                sink_terms.append(jnp.where(lo_q, sink_half[0], sink_half[1]))
            pv = _dot(jnp.concatenate(e_blocks, axis=0), v_rhs)
            for p in range(pairs_per_kv):
                pr = slice(p * CHUNK, (p + 1) * CHUNK)
                pc = slice((pair0 + p) * LANES, (pair0 + p + 1) * LANES)
                o = pv[pr, :LANES] / (pv[pr, LANES:] + sink_terms[p])
                y_s[rows, pc] = (o * _silu(z[rows, pc])).astype(BF16)

    sub = _dot(y_s[...], w_out_ref[...])
    o_ref[...] = _layer_norm(alpha * h + sub, pg_ref[...], pb_ref[...])


def _swa_layer(h1, kv, w_in, bias_tab, mask_tab, sinks, w_out, pg, pb, *, alpha, tm, seq):
    t_tokens, d_model = h1.shape
    bw = w_out.shape[0]
    kvw2 = kv.shape[1]
    cpt = tm // CHUNK
    whole = pl.BlockSpec(memory_space=pltpu.VMEM)
    return pl.pallas_call(
        functools.partial(_swa_kernel, alpha=alpha, tiles_per_seq=seq // tm),
        grid=(t_tokens // tm,),
        in_specs=[pl.BlockSpec((tm, d_model), lambda i: (i, 0)),
                  pl.BlockSpec((tm, kvw2), lambda i: (i, 0)),
                  pl.BlockSpec((CHUNK, kvw2), lambda i: (jnp.maximum(i * cpt - 1, 0), 0)),
                  whole, whole, whole,
                  pl.BlockSpec(memory_space=pltpu.SMEM),
                  whole, whole, whole],
        out_specs=pl.BlockSpec((tm, d_model), lambda i: (i, 0)),
        out_shape=jax.ShapeDtypeStruct((t_tokens, d_model), F32),
        scratch_shapes=[pltpu.VMEM((tm, bw), BF16)],
        compiler_params=pltpu.CompilerParams(
            dimension_semantics=("arbitrary",),
            vmem_limit_bytes=V7X_VMEM_BYTES * 7 // 8),
        name="swa_layer",
    )(h1, kv, kv, w_in, bias_tab, mask_tab, sinks, w_out, pg, pb)


def kernel(x, w_in_a, sgu_ln_g, sgu_ln_b, w_spatial, b_spatial, w_out_a, w_kv, w_in_b, attn_sinks,
           rel_bias, w_out_b, post_ln_g, post_ln_b):
    bsz, seq, d_model = x.shape
    depth = post_ln_g.shape[0]
    assert w_in_a.shape[0] == 1 and w_in_b.shape[0] == 1 and depth == 2
    assert seq % CHUNK == 0 and w_spatial.shape[1:] == (A_GROUPS, CHUNK, CHUNK)
    assert w_kv.shape[1] == 2 * LANES and rel_bias.shape == (REL_BUCKETS, w_out_b.shape[1] // HEAD_DIM)
    alpha = (2.0 * depth) ** 0.25
    aw = w_out_a.shape[1]
    gd = aw // A_GROUPS
    tm = 2 * CHUNK

    x2 = x.reshape(bsz * seq, d_model)
    row = lambda a: a.reshape(1, -1)
    bsp_b = jnp.broadcast_to(b_spatial[0][:, :, None], (A_GROUPS, CHUNK, gd))
    h1, kv = _sgu_layer(x2, w_in_a[0].astype(BF16), row(sgu_ln_g[0]), row(sgu_ln_b[0]), w_spatial[0], bsp_b,
                        w_out_a[0].astype(BF16), w_kv.astype(BF16), row(post_ln_g[0]), row(post_ln_b[0]),
                        alpha=alpha, tm=tm)
    bias_tab = _rel_bias_table(rel_bias)
    out = _swa_layer(h1, kv, w_in_b[0].astype(BF16), bias_tab, jnp.asarray(_band_mask_table()), attn_sinks[0],
                     w_out_b[0].astype(BF16), row(post_ln_g[1]), row(post_ln_b[1]),
                     alpha=alpha, tm=tm, seq=seq)
    return out.reshape(bsz, seq, d_model)
```

```python
import functools
import math

import numpy as np
import jax
import jax.numpy as jnp
from jax import lax
from jax.experimental import pallas as pl
from jax.experimental.pallas import tpu as pltpu

F32 = jnp.float32
BF16 = jnp.bfloat16

CHUNK = 128
A_GROUPS = 8
HEAD_DIM = 64
Q_PER_KV = 8
REL_BUCKETS = 32
REL_MAX_DIST = 128
LN_EPS = 1e-5
NEG_INF = -1e30
LOG2E = math.log2(math.e)

LANES = 128
V7X_VMEM_BYTES = 64 * 1024 * 1024


def _bucket_lower_bounds():
    max_exact = REL_BUCKETS // 2
    d = np.arange(REL_MAX_DIST, dtype=np.int32)
    df = np.maximum(d, 1).astype(np.float32)
    large = max_exact + (np.log(df / np.float32(max_exact)) / np.float32(math.log(REL_MAX_DIST / max_exact))
                         * np.float32(REL_BUCKETS - max_exact)).astype(np.int32)
    large = np.minimum(large, REL_BUCKETS - 1)
    bucket = np.where(d < max_exact, d, large)
    assert np.all(np.diff(bucket) >= 0) and bucket[0] == 0
    return [int(np.argmax(bucket >= b)) if np.any(bucket >= b) else REL_MAX_DIST for b in range(REL_BUCKETS)]


def _band_scale_table():
    t = np.arange(CHUNK)[:, None]
    j = np.arange(2 * CHUNK)[None, :]
    d = t + CHUNK - j
    in_window = (d >= 0) & (d < CHUNK)
    first = in_window & (j >= CHUNK)
    tab = np.stack([np.tile(in_window, (1, 2)), np.tile(first, (1, 2))]).astype(np.float32)
    return tab * np.float32(LOG2E)


def _layer_norm(x, g, b):
    mu = jnp.mean(x, axis=-1, keepdims=True)
    d = x - mu
    var = jnp.mean(d * d, axis=-1, keepdims=True)
    return d * lax.rsqrt(var + LN_EPS) * g + b


def _silu(z):
    return z / (1.0 + jnp.exp(-z))


def _rel_bias_kernel(rb_ref, sink_ref, out_ref, *, lower_bounds):
    variant = pl.program_id(0)
    p = pl.program_id(1)
    t = lax.broadcasted_iota(jnp.int32, (CHUNK, 2 * CHUNK), 0)
    j = lax.broadcasted_iota(jnp.int32, (CHUNK, 2 * CHUNK), 1)
    dist = t + CHUNK - j
    visible = (dist >= 0) & (dist < CHUNK) & (j >= variant * CHUNK)
    d = jnp.clip(dist, 0, REL_MAX_DIST - 1)
    for e in range(2):
        h = 2 * p + e
        acc = jnp.full((CHUNK, 2 * CHUNK), rb_ref[0, h], F32)
        for b in range(1, REL_BUCKETS):
            if lower_bounds[b] < REL_MAX_DIST:
                acc = jnp.where(d >= lower_bounds[b], rb_ref[b, h], acc)
        hidden = jnp.where(j == 0, sink_ref[h], NEG_INF)
        out_ref[0, 0, :, e * 2 * CHUNK:(e + 1) * 2 * CHUNK] = jnp.where(visible, acc, hidden) * LOG2E


def _rel_bias_table(rel_bias, sinks):
    n_pairs = rel_bias.shape[1] // 2
    smem = pl.BlockSpec(memory_space=pltpu.SMEM)
    return pl.pallas_call(
        functools.partial(_rel_bias_kernel, lower_bounds=_bucket_lower_bounds()),
        grid=(2, n_pairs),
        in_specs=[smem, smem],
        out_specs=pl.BlockSpec((1, 1, CHUNK, 4 * CHUNK), lambda v, p: (v, p, 0, 0)),
        out_shape=jax.ShapeDtypeStruct((2, n_pairs, CHUNK, 4 * CHUNK), F32),
        name="rel_bias_table",
    )(rel_bias, sinks)


def _sgu_kernel(x_ref, xn_ref, w_in_ref, lng_ref, lnb_ref, ws_ref, bsp_ref, w_out_ref, w_kv_ref,
                pg_ref, pb_ref, h_ref, kv_ref, v_s, vn_s, y_s, *, alpha):
    tm, _ = x_ref.shape
    aw = vn_s.shape[1]
    gd = aw // A_GROUPS
    n_chunks = tm // CHUNK

    def project_v(src_ref, c):
        rows = slice(c * CHUNK, (c + 1) * CHUNK)
        v_s[rows, :] = jnp.dot(src_ref[rows, :].astype(BF16), w_in_ref[:, aw:2 * aw],
                               preferred_element_type=F32)

    @pl.when(pl.program_id(0) == 0)
    def _():
        for c in range(n_chunks):
            project_v(x_ref, c)

    x = x_ref[...]
    xb = x.astype(BF16)
    vn_s[...] = _layer_norm(v_s[...], lng_ref[...], lnb_ref[...]).astype(BF16)

    row = lax.broadcasted_iota(jnp.int32, (CHUNK, CHUNK), 0)
    col = lax.broadcasted_iota(jnp.int32, (CHUNK, CHUNK), 1)
    tril = col <= row

    def gate_of(g):
        u = jnp.dot(xb, w_in_ref[:, g * gd:(g + 1) * gd], preferred_element_type=F32)
        z = jnp.dot(xb, w_in_ref[:, 2 * aw + g * gd:2 * aw + (g + 1) * gd], preferred_element_type=F32)
        return u * _silu(z)

    def spatial(g, gate):
        cols = slice(g * gd, (g + 1) * gd)
        ws = jnp.where(tril, ws_ref[g], 0.0).astype(BF16)
        for c in range(n_chunks):
            rows = slice(c * CHUNK, (c + 1) * CHUNK)
            s = jnp.dot(ws, vn_s[rows, cols], preferred_element_type=F32) + bsp_ref[g]
            y_s[rows, cols] = (gate[rows] * s).astype(BF16)

    gates = {0: gate_of(0)}
    for g in range(A_GROUPS):
        if g + 1 < A_GROUPS:
            gates[g + 1] = gate_of(g + 1)
        spatial(g, gates.pop(g))

    sub = jnp.dot(y_s[...], w_out_ref[...], preferred_element_type=F32)
    project_v(xn_ref, 0)
    h = _layer_norm(alpha * x + sub, pg_ref[...], pb_ref[...])
    h_ref[...] = h
    kv_ref[...] = jnp.dot(h.astype(BF16), w_kv_ref[...], preferred_element_type=F32).astype(BF16)
    for c in range(1, n_chunks):
        project_v(xn_ref, c)


def _sgu_layer(x2, w_in, ln_g, ln_b, w_spatial, bsp_b, w_out, w_kv, pg, pb, *, alpha, tm):
    t_tokens, d_model = x2.shape
    aw = w_out.shape[0]
    kvw = w_kv.shape[1]
    n_tiles = t_tokens // tm
    whole = pl.BlockSpec(memory_space=pltpu.VMEM)
    return pl.pallas_call(
        functools.partial(_sgu_kernel, alpha=alpha),
        grid=(n_tiles,),
        in_specs=[pl.BlockSpec((tm, d_model), lambda i: (i, 0)),
                  pl.BlockSpec((tm, d_model), lambda i: (jnp.minimum(i + 1, n_tiles - 1), 0)),
                  whole, whole, whole, whole, whole, whole, whole, whole, whole],
        out_specs=[pl.BlockSpec((tm, d_model), lambda i: (i, 0)),
                   pl.BlockSpec((tm, kvw), lambda i: (i, 0))],
        out_shape=[jax.ShapeDtypeStruct((t_tokens, d_model), F32),
                   jax.ShapeDtypeStruct((t_tokens, kvw), BF16)],
        scratch_shapes=[pltpu.VMEM((tm, aw), F32), pltpu.VMEM((tm, aw), BF16), pltpu.VMEM((tm, aw), BF16)],
        compiler_params=pltpu.CompilerParams(
            dimension_semantics=("arbitrary",),
            vmem_limit_bytes=V7X_VMEM_BYTES * 7 // 8),
        name="sgu_layer",
    )(x2, x2, w_in, ln_g, ln_b, w_spatial, bsp_b, w_out, w_kv, pg, pb)


def _swa_kernel(h_ref, hn_ref, kv_ref, kvp_ref, w_in_ref, bias_ref, scale_ref, w_out_ref,
                pg_ref, pb_ref, o_ref, q_s, g_s, y_s, *, alpha, tiles_per_seq):
    tm, _ = h_ref.shape
    bw = y_s.shape[1]
    n_kv = bw // (HEAD_DIM * Q_PER_KV)
    pairs_per_kv = Q_PER_KV // 2
    i = pl.program_id(0)

    def project_qz(src_ref):
        hb = src_ref[...].astype(BF16)
        q_s[...] = (jnp.dot(hb, w_in_ref[:, :bw], preferred_element_type=F32) * (HEAD_DIM ** -0.5)).astype(BF16)
        g_s[...] = _silu(jnp.dot(hb, w_in_ref[:, bw:], preferred_element_type=F32))

    @pl.when(i == 0)
    def _():
        project_qz(h_ref)

    as_bf16 = lambda a: a.astype(F32).astype(BF16)
    lo = as_bf16(lax.broadcasted_iota(jnp.int32, (2 * CHUNK, LANES), 1)) < HEAD_DIM
    key0 = as_bf16(lax.broadcasted_iota(jnp.int32, (2 * CHUNK, LANES), 0)) < 1
    zeros = jnp.zeros((2 * CHUNK, LANES), BF16)
    ones = jnp.ones((2 * CHUNK, LANES), BF16)
    ones_lo = jnp.where(lo, ones, zeros)
    ones_hi = jnp.where(lo, zeros, ones)
    first = (i % tiles_per_seq == 0).astype(jnp.int32)

    units = []
    for c in range(tm // CHUNK):
        rows = slice(c * CHUNK, (c + 1) * CHUNK)
        kv_prev = kvp_ref[...] if c == 0 else kv_ref[(c - 1) * CHUNK:c * CHUNK, :]
        band = jnp.concatenate([kv_prev, kv_ref[rows, :]], axis=0)
        kb, vb = band[:, :LANES], band[:, LANES:]
        vb = jnp.where(key0, zeros, vb)
        kr = pltpu.roll(kb, HEAD_DIM, axis=1)
        vr = pltpu.roll(vb, HEAD_DIM, axis=1)
        variant = first if c == 0 else 0
        for kvh in range(n_kv):
            k_src, k_alt = (kb, kr) if kvh == 0 else (kr, kb)
            v_src, v_alt = (vb, vr) if kvh == 0 else (vr, vb)
            k_cat = jnp.concatenate([jnp.where(lo, k_src, zeros), jnp.where(lo, zeros, k_alt)], axis=0)
            v_rhs = jnp.concatenate(
                [jnp.concatenate([jnp.where(lo, v_src, zeros), ones_lo], axis=1),
                 jnp.concatenate([jnp.where(lo, zeros, v_alt), ones_hi], axis=1)], axis=0)
            pair0 = kvh * pairs_per_kv
            q_stack = jnp.concatenate(
                [q_s[rows, (pair0 + p) * LANES:(pair0 + p + 1) * LANES] for p in range(pairs_per_kv)], axis=0)
            s = lax.dot_general(q_stack, k_cat, (((1,), (1,)), ((), ())),
                                preferred_element_type=F32)
            units.append((rows, variant, pair0, s, v_rhs))

    for rows, variant, pair0, s, v_rhs in units:
        e_blocks = []
        for p in range(pairs_per_kv):
            pr = slice(p * CHUNK, (p + 1) * CHUNK)
            logit = s[pr] * scale_ref[variant] + bias_ref[variant, pair0 + p]
            halves = []
            for e in range(2):
                le = logit[:, e * 2 * CHUNK:(e + 1) * 2 * CHUNK]
                halves.append(jnp.exp2(le - jnp.max(le, axis=-1, keepdims=True)))
            e_blocks.append(jnp.concatenate(halves, axis=1).astype(BF16))
        pv = jnp.dot(jnp.concatenate(e_blocks, axis=0), v_rhs,
                     preferred_element_type=F32)
        for p in range(pairs_per_kv):
            pr = slice(p * CHUNK, (p + 1) * CHUNK)
            pc = slice((pair0 + p) * LANES, (pair0 + p + 1) * LANES)
            y_s[rows, pc] = (pv[pr, :LANES] / pv[pr, LANES:] * g_s[rows, pc]).astype(BF16)

    sub = jnp.dot(y_s[...], w_out_ref[...], preferred_element_type=F32)
    project_qz(hn_ref)
    o_ref[...] = _layer_norm(alpha * h_ref[...] + sub, pg_ref[...], pb_ref[...])


def _swa_layer(h1, kv, w_in, bias_tab, scale_tab, w_out, pg, pb, *, alpha, tm, seq):
    t_tokens, d_model = h1.shape
    bw = w_out.shape[0]
    kvw2 = kv.shape[1]
    cpt = tm // CHUNK
    n_tiles = t_tokens // tm
    whole = pl.BlockSpec(memory_space=pltpu.VMEM)
    return pl.pallas_call(
        functools.partial(_swa_kernel, alpha=alpha, tiles_per_seq=seq // tm),
        grid=(n_tiles,),
        in_specs=[pl.BlockSpec((tm, d_model), lambda i: (i, 0)),
                  pl.BlockSpec((tm, d_model), lambda i: (jnp.minimum(i + 1, n_tiles - 1), 0)),
                  pl.BlockSpec((tm, kvw2), lambda i: (i, 0)),
                  pl.BlockSpec((CHUNK, kvw2), lambda i: (jnp.maximum(i * cpt - 1, 0), 0)),
                  whole, whole, whole, whole, whole, whole],
        out_specs=pl.BlockSpec((tm, d_model), lambda i: (i, 0)),
        out_shape=jax.ShapeDtypeStruct((t_tokens, d_model), F32),
        scratch_shapes=[pltpu.VMEM((tm, bw), BF16), pltpu.VMEM((tm, bw), F32), pltpu.VMEM((tm, bw), BF16)],
        compiler_params=pltpu.CompilerParams(
            dimension_semantics=("arbitrary",),
            vmem_limit_bytes=V7X_VMEM_BYTES * 7 // 8),
        name="swa_layer",
    )(h1, h1, kv, kv, w_in, bias_tab, scale_tab, w_out, pg, pb)


def kernel(x, w_in_a, sgu_ln_g, sgu_ln_b, w_spatial, b_spatial, w_out_a, w_kv, w_in_b, attn_sinks,
           rel_bias, w_out_b, post_ln_g, post_ln_b):
    bsz, seq, d_model = x.shape
    depth = post_ln_g.shape[0]
    assert w_in_a.shape[0] == 1 and w_in_b.shape[0] == 1 and depth == 2
    assert seq % CHUNK == 0 and w_spatial.shape[1:] == (A_GROUPS, CHUNK, CHUNK)
    assert w_kv.shape[1] == 2 * LANES and rel_bias.shape == (REL_BUCKETS, w_out_b.shape[1] // HEAD_DIM)
    alpha = (2.0 * depth) ** 0.25
    aw = w_out_a.shape[1]
    gd = aw // A_GROUPS
    tm = 2 * CHUNK

    x2 = x.reshape(bsz * seq, d_model)
    row = lambda a: a.reshape(1, -1)
    bsp_b = jnp.broadcast_to(b_spatial[0][:, :, None], (A_GROUPS, CHUNK, gd))
    h1, kv = _sgu_layer(x2, w_in_a[0].astype(BF16), row(sgu_ln_g[0]), row(sgu_ln_b[0]), w_spatial[0], bsp_b,
                        w_out_a[0].astype(BF16), w_kv.astype(BF16), row(post_ln_g[0]), row(post_ln_b[0]),
                        alpha=alpha, tm=tm)
    bias_tab = _rel_bias_table(rel_bias, attn_sinks[0])
    out = _swa_layer(h1, kv, w_in_b[0].astype(BF16), bias_tab, jnp.asarray(_band_scale_table()),
                     w_out_b[0].astype(BF16), row(post_ln_g[1]), row(post_ln_b[1]),
                     alpha=alpha, tm=tm, seq=seq)
    return out.reshape(bsz, seq, d_model)
```

```python
import functools
import math

import numpy as np
import jax
import jax.numpy as jnp
from jax import lax
from jax.experimental import pallas as pl
from jax.experimental.pallas import tpu as pltpu

F32 = jnp.float32
BF16 = jnp.bfloat16

CHUNK = 128
A_GROUPS = 8
HEAD_DIM = 64
Q_PER_KV = 8
REL_BUCKETS = 32
REL_MAX_DIST = 128
LN_EPS = 1e-5
NEG_INF = -1e30
LOG2E = math.log2(math.e)

LANES = 128
V7X_VMEM_BYTES = 64 * 1024 * 1024


def _bucket_lower_bounds():
    max_exact = REL_BUCKETS // 2
    d = np.arange(REL_MAX_DIST, dtype=np.int32)
    df = np.maximum(d, 1).astype(np.float32)
    large = max_exact + (np.log(df / np.float32(max_exact)) / np.float32(math.log(REL_MAX_DIST / max_exact))
                         * np.float32(REL_BUCKETS - max_exact)).astype(np.int32)
    large = np.minimum(large, REL_BUCKETS - 1)
    bucket = np.where(d < max_exact, d, large)
    assert np.all(np.diff(bucket) >= 0) and bucket[0] == 0
    return [int(np.argmax(bucket >= b)) if np.any(bucket >= b) else REL_MAX_DIST for b in range(REL_BUCKETS)]


def _band_scale_table():
    t = np.arange(CHUNK)[:, None]
    j = np.arange(2 * CHUNK)[None, :]
    d = t + CHUNK - j
    in_window = (d >= 0) & (d < CHUNK)
    first = in_window & (j >= CHUNK)
    tab = np.stack([np.tile(in_window, (1, 2)), np.tile(first, (1, 2))]).astype(np.float32)
    return tab * np.float32(LOG2E)


def _layer_norm(x, g, b):
    mu = jnp.mean(x, axis=-1, keepdims=True)
    d = x - mu
    var = jnp.mean(d * d, axis=-1, keepdims=True)
    return d * lax.rsqrt(var + LN_EPS) * g + b


def _silu(z):
    return z / (1.0 + jnp.exp(-z))


def _rel_bias_kernel(rb_ref, sink_ref, out_ref, *, lower_bounds):
    variant = pl.program_id(0)
    p = pl.program_id(1)
    t = lax.broadcasted_iota(jnp.int32, (CHUNK, 2 * CHUNK), 0)
    j = lax.broadcasted_iota(jnp.int32, (CHUNK, 2 * CHUNK), 1)
    dist = t + CHUNK - j
    visible = (dist >= 0) & (dist < CHUNK) & (j >= variant * CHUNK)
    d = jnp.clip(dist, 0, REL_MAX_DIST - 1)
    for e in range(2):
        h = 2 * p + e
        acc = jnp.full((CHUNK, 2 * CHUNK), rb_ref[0, h], F32)
        for b in range(1, REL_BUCKETS):
            if lower_bounds[b] < REL_MAX_DIST:
                acc = jnp.where(d >= lower_bounds[b], rb_ref[b, h], acc)
        hidden = jnp.where(j == 0, sink_ref[h], NEG_INF)
        out_ref[0, 0, :, e * 2 * CHUNK:(e + 1) * 2 * CHUNK] = jnp.where(visible, acc, hidden) * LOG2E


def _rel_bias_table(rel_bias, sinks):
    n_pairs = rel_bias.shape[1] // 2
    smem = pl.BlockSpec(memory_space=pltpu.SMEM)
    return pl.pallas_call(
        functools.partial(_rel_bias_kernel, lower_bounds=_bucket_lower_bounds()),
        grid=(2, n_pairs),
        in_specs=[smem, smem],
        out_specs=pl.BlockSpec((1, 1, CHUNK, 4 * CHUNK), lambda v, p: (v, p, 0, 0)),
        out_shape=jax.ShapeDtypeStruct((2, n_pairs, CHUNK, 4 * CHUNK), F32),
        name="rel_bias_table",
    )(rel_bias, sinks)


def _sgu_kernel(x_ref, xn_ref, w_in_ref, lng_ref, lnb_ref, ws_ref, bsp_ref, w_out_ref, w_kv_ref,
                pg_ref, pb_ref, h_ref, kv_ref, v_s, vn_s, y_s, *, alpha):
    tm, _ = x_ref.shape
    aw = vn_s.shape[1]
    gd = aw // A_GROUPS
    n_chunks = tm // CHUNK

    def project_v(src_ref, c):
        rows = slice(c * CHUNK, (c + 1) * CHUNK)
        v_s[rows, :] = jnp.dot(src_ref[rows, :].astype(BF16), w_in_ref[:, aw:2 * aw],
                               preferred_element_type=F32)

    @pl.when(pl.program_id(0) == 0)
    def _():
        for c in range(n_chunks):
            project_v(x_ref, c)

    x = x_ref[...]
    xb = x.astype(BF16)
    vn_s[...] = _layer_norm(v_s[...], lng_ref[...], lnb_ref[...]).astype(BF16)

    row = lax.broadcasted_iota(jnp.int32, (CHUNK, CHUNK), 0)
    col = lax.broadcasted_iota(jnp.int32, (CHUNK, CHUNK), 1)
    tril = col <= row

    def gate_of(g):
        u = jnp.dot(xb, w_in_ref[:, g * gd:(g + 1) * gd], preferred_element_type=F32)
        z = jnp.dot(xb, w_in_ref[:, 2 * aw + g * gd:2 * aw + (g + 1) * gd], preferred_element_type=F32)
        return u * _silu(z)

    def spatial(g, gate):
        cols = slice(g * gd, (g + 1) * gd)
        ws = jnp.where(tril, ws_ref[g], 0.0).astype(BF16)
        for c in range(n_chunks):
            rows = slice(c * CHUNK, (c + 1) * CHUNK)
            s = jnp.dot(ws, vn_s[rows, cols], preferred_element_type=F32) + bsp_ref[g]
            y_s[rows, cols] = (gate[rows] * s).astype(BF16)

    gates = {0: gate_of(0)}
    for g in range(A_GROUPS):
        if g + 1 < A_GROUPS:
            gates[g + 1] = gate_of(g + 1)
        spatial(g, gates.pop(g))

    sub = jnp.dot(y_s[...], w_out_ref[...], preferred_element_type=F32)
    project_v(xn_ref, 0)
    h = _layer_norm(alpha * x + sub, pg_ref[...], pb_ref[...])
    h_ref[...] = h
    kv_ref[...] = jnp.dot(h.astype(BF16), w_kv_ref[...], preferred_element_type=F32).astype(BF16)
    for c in range(1, n_chunks):
        project_v(xn_ref, c)


def _sgu_layer(x2, w_in, ln_g, ln_b, w_spatial, bsp_b, w_out, w_kv, pg, pb, *, alpha, tm):
    t_tokens, d_model = x2.shape
    aw = w_out.shape[0]
    kvw = w_kv.shape[1]
    n_tiles = t_tokens // tm
    whole = pl.BlockSpec(memory_space=pltpu.VMEM)
    return pl.pallas_call(
        functools.partial(_sgu_kernel, alpha=alpha),
        grid=(n_tiles,),
        in_specs=[pl.BlockSpec((tm, d_model), lambda i: (i, 0)),
                  pl.BlockSpec((tm, d_model), lambda i: (jnp.minimum(i + 1, n_tiles - 1), 0)),
                  whole, whole, whole, whole, whole, whole, whole, whole, whole],
        out_specs=[pl.BlockSpec((tm, d_model), lambda i: (i, 0)),
                   pl.BlockSpec((tm, kvw), lambda i: (i, 0))],
        out_shape=[jax.ShapeDtypeStruct((t_tokens, d_model), F32),
                   jax.ShapeDtypeStruct((t_tokens, kvw), BF16)],
        scratch_shapes=[pltpu.VMEM((tm, aw), F32), pltpu.VMEM((tm, aw), BF16), pltpu.VMEM((tm, aw), BF16)],
        compiler_params=pltpu.CompilerParams(
            dimension_semantics=("arbitrary",),
            vmem_limit_bytes=V7X_VMEM_BYTES * 7 // 8),
        name="sgu_layer",
    )(x2, x2, w_in, ln_g, ln_b, w_spatial, bsp_b, w_out, w_kv, pg, pb)


def _swa_kernel(h_ref, hn_ref, kv_ref, kvp_ref, w_in_ref, bias_ref, scale_ref, w_out_ref,
                pg_ref, pb_ref, o_ref, q_s, g_s, y_s, *, alpha, tiles_per_seq):
    tm, _ = h_ref.shape
    bw = y_s.shape[1]
    n_kv = bw // (HEAD_DIM * Q_PER_KV)
    pairs_per_kv = Q_PER_KV // 2
    i = pl.program_id(0)

    def project_qz(src_ref):
        hb = src_ref[...].astype(BF16)
        q_s[...] = (jnp.dot(hb, w_in_ref[:, :bw], preferred_element_type=F32) * (HEAD_DIM ** -0.5)).astype(BF16)
        g_s[...] = _silu(jnp.dot(hb, w_in_ref[:, bw:], preferred_element_type=F32))

    @pl.when(i == 0)
    def _():
        project_qz(h_ref)

    as_bf16 = lambda a: a.astype(F32).astype(BF16)
    lo = as_bf16(lax.broadcasted_iota(jnp.int32, (2 * CHUNK, LANES), 1)) < HEAD_DIM
    key0 = as_bf16(lax.broadcasted_iota(jnp.int32, (2 * CHUNK, LANES), 0)) < 1
    zeros = jnp.zeros((2 * CHUNK, LANES), BF16)
    ones = jnp.ones((2 * CHUNK, LANES), BF16)
    ones_lo = jnp.where(lo, ones, zeros)
    ones_hi = jnp.where(lo, zeros, ones)
    first = (i % tiles_per_seq == 0).astype(jnp.int32)

    units = []
    for c in range(tm // CHUNK):
        rows = slice(c * CHUNK, (c + 1) * CHUNK)
        kv_prev = kvp_ref[...] if c == 0 else kv_ref[(c - 1) * CHUNK:c * CHUNK, :]
        band = jnp.concatenate([kv_prev, kv_ref[rows, :]], axis=0)
        kb, vb = band[:, :LANES], band[:, LANES:]
        vb = jnp.where(key0, zeros, vb)
        kr = pltpu.roll(kb, HEAD_DIM, axis=1)
        vr = pltpu.roll(vb, HEAD_DIM, axis=1)
        variant = first if c == 0 else 0
        for kvh in range(n_kv):
            k_src, k_alt = (kb, kr) if kvh == 0 else (kr, kb)
            v_src, v_alt = (vb, vr) if kvh == 0 else (vr, vb)
            k_cat = jnp.concatenate([jnp.where(lo, k_src, zeros), jnp.where(lo, zeros, k_alt)], axis=0)
            v_rhs = jnp.concatenate(
                [jnp.concatenate([jnp.where(lo, v_src, zeros), ones_lo], axis=1),
                 jnp.concatenate([jnp.where(lo, zeros, v_alt), ones_hi], axis=1)], axis=0)
            pair0 = kvh * pairs_per_kv
            q_stack = jnp.concatenate(
                [q_s[rows, (pair0 + p) * LANES:(pair0 + p + 1) * LANES] for p in range(pairs_per_kv)], axis=0)
            s = lax.dot_general(q_stack, k_cat, (((1,), (1,)), ((), ())),
                                preferred_element_type=F32)
            units.append((rows, variant, pair0, s, v_rhs))

    for rows, variant, pair0, s, v_rhs in units:
        e_blocks = []
        for p in range(pairs_per_kv):
            pr = slice(p * CHUNK, (p + 1) * CHUNK)
            logit = s[pr] * scale_ref[variant] + bias_ref[variant, pair0 + p]
            halves = []
            for e in range(2):
                le = logit[:, e * 2 * CHUNK:(e + 1) * 2 * CHUNK]
                halves.append(jnp.exp2(le - jnp.max(le, axis=-1, keepdims=True)))
            e_blocks.append(jnp.concatenate(halves, axis=1).astype(BF16))
        pv = jnp.dot(jnp.concatenate(e_blocks, axis=0), v_rhs,
                     preferred_element_type=F32)
        for p in range(pairs_per_kv):
            pr = slice(p * CHUNK, (p + 1) * CHUNK)
            pc = slice((pair0 + p) * LANES, (pair0 + p + 1) * LANES)
            y_s[rows, pc] = (pv[pr, :LANES] / pv[pr, LANES:] * g_s[rows, pc]).astype(BF16)

    sub = jnp.dot(y_s[...], w_out_ref[...], preferred_element_type=F32)
    project_qz(hn_ref)
    o_ref[...] = _layer_norm(alpha * h_ref[...] + sub, pg_ref[...], pb_ref[...])


def _swa_layer(h1, kv, w_in, bias_tab, scale_tab, w_out, pg, pb, *, alpha, tm, seq):
    t_tokens, d_model = h1.shape
    bw = w_out.shape[0]
    kvw2 = kv.shape[1]
    cpt = tm // CHUNK
    n_tiles = t_tokens // tm
    whole = pl.BlockSpec(memory_space=pltpu.VMEM)
    return pl.pallas_call(
        functools.partial(_swa_kernel, alpha=alpha, tiles_per_seq=seq // tm),
        grid=(n_tiles,),
        in_specs=[pl.BlockSpec((tm, d_model), lambda i: (i, 0)),
                  pl.BlockSpec((tm, d_model), lambda i: (jnp.minimum(i + 1, n_tiles - 1), 0)),
                  pl.BlockSpec((tm, kvw2), lambda i: (i, 0)),
                  pl.BlockSpec((CHUNK, kvw2), lambda i: (jnp.maximum(i * cpt - 1, 0), 0)),
                  whole, whole, whole, whole, whole, whole],
        out_specs=pl.BlockSpec((tm, d_model), lambda i: (i, 0)),
        out_shape=jax.ShapeDtypeStruct((t_tokens, d_model), F32),
        scratch_shapes=[pltpu.VMEM((tm, bw), BF16), pltpu.VMEM((tm, bw), F32), pltpu.VMEM((tm, bw), BF16)],
        compiler_params=pltpu.CompilerParams(
            dimension_semantics=("arbitrary",),
            vmem_limit_bytes=V7X_VMEM_BYTES * 7 // 8),
        name="swa_layer",
    )(h1, h1, kv, kv, w_in, bias_tab, scale_tab, w_out, pg, pb)


def kernel(x, w_in_a, sgu_ln_g, sgu_ln_b, w_spatial, b_spatial, w_out_a, w_kv, w_in_b, attn_sinks,
           rel_bias, w_out_b, post_ln_g, post_ln_b):
    bsz, seq, d_model = x.shape
    depth = post_ln_g.shape[0]
    assert w_in_a.shape[0] == 1 and w_in_b.shape[0] == 1 and depth == 2
    assert seq % CHUNK == 0 and w_spatial.shape[1:] == (A_GROUPS, CHUNK, CHUNK)
    assert w_kv.shape[1] == 2 * LANES and rel_bias.shape == (REL_BUCKETS, w_out_b.shape[1] // HEAD_DIM)
    alpha = (2.0 * depth) ** 0.25
    aw = w_out_a.shape[1]
    gd = aw // A_GROUPS
    tm = 4 * CHUNK

    x2 = x.reshape(bsz * seq, d_model)
    row = lambda a: a.reshape(1, -1)
    bsp_b = jnp.broadcast_to(b_spatial[0][:, :, None], (A_GROUPS, CHUNK, gd))
    h1, kv = _sgu_layer(x2, w_in_a[0].astype(BF16), row(sgu_ln_g[0]), row(sgu_ln_b[0]), w_spatial[0], bsp_b,
                        w_out_a[0].astype(BF16), w_kv.astype(BF16), row(post_ln_g[0]), row(post_ln_b[0]),
                        alpha=alpha, tm=tm)
    bias_tab = _rel_bias_table(rel_bias, attn_sinks[0])
    out = _swa_layer(h1, kv, w_in_b[0].astype(BF16), bias_tab, jnp.asarray(_band_scale_table()),
                     w_out_b[0].astype(BF16), row(post_ln_g[1]), row(post_ln_b[1]),
                     alpha=alpha, tm=tm, seq=seq)
    return out.reshape(bsz, seq, d_model)
```

```python
import functools
import math

import numpy as np
import jax
import jax.numpy as jnp
from jax import lax
from jax.experimental import pallas as pl
from jax.experimental.pallas import tpu as pltpu

F32 = jnp.float32
BF16 = jnp.bfloat16

CHUNK = 128
A_GROUPS = 8
HEAD_DIM = 64
Q_PER_KV = 8
REL_BUCKETS = 32
REL_MAX_DIST = 128
LN_EPS = 1e-5
NEG_INF = -1e30
LOG2E = math.log2(math.e)

LANES = 128
PROJ_ROWS = 2 * CHUNK
DEN_ROWS = 16
V7X_VMEM_BYTES = 64 * 1024 * 1024


def _bucket_lower_bounds():
    max_exact = REL_BUCKETS // 2
    d = np.arange(REL_MAX_DIST, dtype=np.int32)
    df = np.maximum(d, 1).astype(np.float32)
    large = max_exact + (np.log(df / np.float32(max_exact)) / np.float32(math.log(REL_MAX_DIST / max_exact))
                         * np.float32(REL_BUCKETS - max_exact)).astype(np.int32)
    large = np.minimum(large, REL_BUCKETS - 1)
    bucket = np.where(d < max_exact, d, large)
    assert np.all(np.diff(bucket) >= 0) and bucket[0] == 0
    return [int(np.argmax(bucket >= b)) if np.any(bucket >= b) else REL_MAX_DIST for b in range(REL_BUCKETS)]


def _band_scale_table():
    j = np.arange(2 * CHUNK)[:, None]
    t = np.arange(CHUNK)[None, :]
    d = t + CHUNK - j
    in_window = (d >= 0) & (d < CHUNK)
    first = in_window & (j >= CHUNK)
    tab = np.stack([np.tile(in_window, (2, 1)), np.tile(first, (2, 1))]).astype(np.float32)
    return tab * np.float32(LOG2E)


def _layer_norm(x, g, b):
    mu = jnp.mean(x, axis=-1, keepdims=True)
    d = x - mu
    var = jnp.mean(d * d, axis=-1, keepdims=True)
    return d * lax.rsqrt(var + LN_EPS) * g + b


def _silu(z):
    return z / (1.0 + jnp.exp(-z))


def _rel_bias_kernel(rb_ref, sink_ref, out_ref, *, lower_bounds, pairs_per_kv):
    pair = pl.program_id(0) * pairs_per_kv + pl.program_id(1)
    j = lax.broadcasted_iota(jnp.int32, (2 * CHUNK, CHUNK), 0)
    t = lax.broadcasted_iota(jnp.int32, (2 * CHUNK, CHUNK), 1)
    dist = t + CHUNK - j
    in_window = (dist >= 0) & (dist < CHUNK)
    d = jnp.clip(dist, 0, REL_MAX_DIST - 1)
    for e in range(2):
        h = 2 * pair + e
        acc = jnp.full((2 * CHUNK, CHUNK), rb_ref[0, h], F32)
        for b in range(1, REL_BUCKETS):
            if lower_bounds[b] < REL_MAX_DIST:
                acc = jnp.where(d >= lower_bounds[b], rb_ref[b, h], acc)
        hidden = jnp.where(j == 0, sink_ref[h], NEG_INF)
        for variant in range(2):
            visible = in_window & (j >= variant * CHUNK)
            out_ref[variant, 0, e * 2 * CHUNK:(e + 1) * 2 * CHUNK, :] = jnp.where(visible, acc, hidden) * LOG2E


def _rel_bias_table(rel_bias, sinks):
    n_heads = rel_bias.shape[1]
    n_kv = n_heads // Q_PER_KV
    pairs_per_kv = Q_PER_KV // 2
    smem = pl.BlockSpec(memory_space=pltpu.SMEM)
    return pl.pallas_call(
        functools.partial(_rel_bias_kernel, lower_bounds=_bucket_lower_bounds(), pairs_per_kv=pairs_per_kv),
        grid=(n_kv, pairs_per_kv),
        in_specs=[smem, smem],
        out_specs=pl.BlockSpec((2, 1, 4 * CHUNK, CHUNK), lambda k, p: (0, k, 0, p)),
        out_shape=jax.ShapeDtypeStruct((2, n_kv, 4 * CHUNK, pairs_per_kv * CHUNK), F32),
        name="rel_bias_table",
    )(rel_bias, sinks)


def _sgu_kernel(x_ref, xn_ref, w_in_ref, lng_ref, lnb_ref, ws_ref, bsp_ref, w_out_ref, w_kv_ref,
                pg_ref, pb_ref, h_ref, kv_ref, v_s, mu_s, rstd_s, y_s, *, alpha):
    tm, _ = x_ref.shape
    aw = v_s.shape[1]
    gd = aw // A_GROUPS
    n_chunks = tm // CHUNK
    n_blocks = tm // PROJ_ROWS

    def project_v(src_ref, b):
        rows = slice(b * PROJ_ROWS, (b + 1) * PROJ_ROWS)
        v = jnp.dot(src_ref[rows, :].astype(BF16), w_in_ref[:, aw:2 * aw], preferred_element_type=F32)
        v_s[rows, :] = v
        mu = jnp.mean(v, axis=-1, keepdims=True)
        var = jnp.mean(jnp.square(v - mu), axis=-1, keepdims=True)
        mu_s[rows, :] = jnp.broadcast_to(mu, (PROJ_ROWS, LANES))
        rstd_s[rows, :] = jnp.broadcast_to(lax.rsqrt(var + LN_EPS), (PROJ_ROWS, LANES))

    @pl.when(pl.program_id(0) == 0)
    def _():
        for b in range(n_blocks):
            project_v(x_ref, b)

    xb = x_ref[...].astype(BF16)
    lane_tiles = gd // LANES
    mu = jnp.concatenate([mu_s[...]] * lane_tiles, axis=1)
    rstd = jnp.concatenate([rstd_s[...]] * lane_tiles, axis=1)

    row = lax.broadcasted_iota(jnp.int32, (CHUNK, CHUNK), 0)
    col = lax.broadcasted_iota(jnp.int32, (CHUNK, CHUNK), 1)
    tril = col <= row

    def gate_of(g):
        u = jnp.dot(xb, w_in_ref[:, g * gd:(g + 1) * gd], preferred_element_type=F32)
        z = jnp.dot(xb, w_in_ref[:, 2 * aw + g * gd:2 * aw + (g + 1) * gd], preferred_element_type=F32)
        return u * _silu(z)

    def spatial(g, gate):
        cols = slice(g * gd, (g + 1) * gd)
        ws = jnp.where(tril, ws_ref[g], 0.0).astype(BF16)
        vn = ((v_s[:, cols] - mu) * rstd * lng_ref[:, cols] + lnb_ref[:, cols]).astype(BF16)
        for c in range(n_chunks):
            rows = slice(c * CHUNK, (c + 1) * CHUNK)
            s = jnp.dot(ws, vn[rows], preferred_element_type=F32) + bsp_ref[g]
            y_s[rows, cols] = (gate[rows] * s).astype(BF16)

    gates = {0: gate_of(0)}
    for g in range(A_GROUPS):
        if g + 1 < A_GROUPS:
            gates[g + 1] = gate_of(g + 1)
        spatial(g, gates.pop(g))

    for b in range(n_blocks):
        rows = slice(b * PROJ_ROWS, (b + 1) * PROJ_ROWS)
        sub = jnp.dot(y_s[rows, :], w_out_ref[...], preferred_element_type=F32)
        project_v(xn_ref, b)
        h = _layer_norm(alpha * x_ref[rows, :] + sub, pg_ref[...], pb_ref[...])
        h_ref[rows, :] = h
        kv_ref[rows, :] = jnp.dot(h.astype(BF16), w_kv_ref[...], preferred_element_type=F32).astype(BF16)


def _sgu_layer(x2, w_in, ln_g, ln_b, w_spatial, bsp_b, w_out, w_kv, pg, pb, *, alpha, tm):
    t_tokens, d_model = x2.shape
    aw = w_out.shape[0]
    kvw = w_kv.shape[1]
    n_tiles = t_tokens // tm
    whole = pl.BlockSpec(memory_space=pltpu.VMEM)
    return pl.pallas_call(
        functools.partial(_sgu_kernel, alpha=alpha),
        grid=(n_tiles,),
        in_specs=[pl.BlockSpec((tm, d_model), lambda i: (i, 0)),
                  pl.BlockSpec((tm, d_model), lambda i: (jnp.minimum(i + 1, n_tiles - 1), 0)),
                  whole, whole, whole, whole, whole, whole, whole, whole, whole],
        out_specs=[pl.BlockSpec((tm, d_model), lambda i: (i, 0)),
                   pl.BlockSpec((tm, kvw), lambda i: (i, 0))],
        out_shape=[jax.ShapeDtypeStruct((t_tokens, d_model), F32),
                   jax.ShapeDtypeStruct((t_tokens, kvw), BF16)],
        scratch_shapes=[pltpu.VMEM((tm, aw), F32), pltpu.VMEM((tm, LANES), F32), pltpu.VMEM((tm, LANES), F32),
                        pltpu.VMEM((tm, aw), BF16)],
        compiler_params=pltpu.CompilerParams(
            dimension_semantics=("arbitrary",),
            vmem_limit_bytes=V7X_VMEM_BYTES * 7 // 8),
        name="sgu_layer",
    )(x2, x2, w_in, ln_g, ln_b, w_spatial, bsp_b, w_out, w_kv, pg, pb)


def _swa_kernel(h_ref, hn_ref, kv_ref, kvp_ref, w_in_ref, bias_ref, scale_ref, w_out_ref,
                pg_ref, pb_ref, o_ref, q_s, g_s, y_s, *, alpha, tiles_per_seq):
    tm, _ = h_ref.shape
    bw = y_s.shape[1]
    n_kv = bw // (HEAD_DIM * Q_PER_KV)
    pairs_per_kv = Q_PER_KV // 2
    i = pl.program_id(0)

    n_blocks = tm // PROJ_ROWS
    chunks_per_block = PROJ_ROWS // CHUNK

    def project_qz(src_ref, b):
        rows = slice(b * PROJ_ROWS, (b + 1) * PROJ_ROWS)
        hb = src_ref[rows, :].astype(BF16)
        q_s[rows, :] = (jnp.dot(hb, w_in_ref[:, :bw], preferred_element_type=F32)
                        * (HEAD_DIM ** -0.5)).astype(BF16)
        g_s[rows, :] = _silu(jnp.dot(hb, w_in_ref[:, bw:], preferred_element_type=F32))

    @pl.when(i == 0)
    def _():
        for b in range(n_blocks):
            project_qz(h_ref, b)

    as_bf16 = lambda a: a.astype(F32).astype(BF16)
    lo = as_bf16(lax.broadcasted_iota(jnp.int32, (2 * CHUNK, LANES), 1)) < HEAD_DIM
    key0 = as_bf16(lax.broadcasted_iota(jnp.int32, (2 * CHUNK, LANES), 0)) < 1
    zeros_k = jnp.zeros((2 * CHUNK, LANES), BF16)
    zeros_v = jnp.zeros((HEAD_DIM, 2 * CHUNK), BF16)
    zeros_d = jnp.zeros((DEN_ROWS, 2 * CHUNK), BF16)
    ones_d = jnp.ones((DEN_ROWS, 2 * CHUNK), BF16)
    first = (i % tiles_per_seq == 0).astype(jnp.int32)

    units = []
    for c in range(tm // CHUNK):
        rows = slice(c * CHUNK, (c + 1) * CHUNK)
        kv_prev = kvp_ref[...] if c == 0 else kv_ref[(c - 1) * CHUNK:c * CHUNK, :]
        band = jnp.concatenate([kv_prev, kv_ref[rows, :]], axis=0)
        kb, vb = band[:, :LANES], band[:, LANES:]
        kr = pltpu.roll(kb, HEAD_DIM, axis=1)
        v_t = jnp.where(key0, zeros_k, vb).astype(F32).T.astype(BF16)
        variant = first if c == 0 else 0
        for kvh in range(n_kv):
            k_src, k_alt = (kb, kr) if kvh == 0 else (kr, kb)
            k_cat = jnp.concatenate([jnp.where(lo, k_src, zeros_k), jnp.where(lo, zeros_k, k_alt)], axis=0)
            vk = v_t[kvh * HEAD_DIM:(kvh + 1) * HEAD_DIM]
            v_lhs = jnp.concatenate(
                [jnp.concatenate([vk, zeros_v], axis=1), jnp.concatenate([zeros_v, vk], axis=1),
                 jnp.concatenate([ones_d, zeros_d], axis=1), jnp.concatenate([zeros_d, ones_d], axis=1)],
                axis=0)
            pair0 = kvh * pairs_per_kv
            q_stack = jnp.concatenate(
                [q_s[rows, (pair0 + p) * LANES:(pair0 + p + 1) * LANES] for p in range(pairs_per_kv)], axis=0)
            s_t = lax.dot_general(k_cat, q_stack, (((1,), (1,)), ((), ())),
                                  preferred_element_type=F32)
            logits = [s_t[:, p * CHUNK:(p + 1) * CHUNK] * scale_ref[variant]
                      + bias_ref[variant, kvh, :, p * CHUNK:(p + 1) * CHUNK] for p in range(pairs_per_kv)]
            units.append((rows, pair0, logits, v_lhs))

    def attend(rows, pair0, logits, v_lhs):
        e_cols = []
        for logit in logits:
            halves = []
            for e in range(2):
                le = logit[e * 2 * CHUNK:(e + 1) * 2 * CHUNK]
                halves.append(jnp.exp2(le - jnp.max(le, axis=0, keepdims=True)).astype(BF16))
            e_cols.append(jnp.concatenate(halves, axis=0))
        pv_t = jnp.dot(v_lhs, jnp.concatenate(e_cols, axis=1), preferred_element_type=F32)
        n_num = 2 * HEAD_DIM
        sub = 8
        inv_e = 1.0 / pv_t[n_num:n_num + sub]
        inv_o = 1.0 / pv_t[n_num + DEN_ROWS:n_num + DEN_ROWS + sub]
        inv = jnp.concatenate([inv_e] * (HEAD_DIM // sub) + [inv_o] * (HEAD_DIM // sub), axis=0)
        o_t = pv_t[:n_num] * inv
        for p in range(pairs_per_kv):
            pc = slice((pair0 + p) * LANES, (pair0 + p + 1) * LANES)
            y_s[rows, pc] = (o_t[:, p * CHUNK:(p + 1) * CHUNK].T * g_s[rows, pc]).astype(BF16)

    units_per_block = chunks_per_block * n_kv
    for b in range(n_blocks):
        rows = slice(b * PROJ_ROWS, (b + 1) * PROJ_ROWS)
        for unit in units[b * units_per_block:(b + 1) * units_per_block]:
            attend(*unit)
        project_qz(hn_ref, b)
        sub = jnp.dot(y_s[rows, :], w_out_ref[...], preferred_element_type=F32)
        o_ref[rows, :] = _layer_norm(alpha * h_ref[rows, :] + sub, pg_ref[...], pb_ref[...])


def _swa_layer(h1, kv, w_in, bias_tab, scale_tab, w_out, pg, pb, *, alpha, tm, seq):
    t_tokens, d_model = h1.shape
    bw = w_out.shape[0]
    kvw2 = kv.shape[1]
    cpt = tm // CHUNK
    n_tiles = t_tokens // tm
    whole = pl.BlockSpec(memory_space=pltpu.VMEM)
    return pl.pallas_call(
        functools.partial(_swa_kernel, alpha=alpha, tiles_per_seq=seq // tm),
        grid=(n_tiles,),
        in_specs=[pl.BlockSpec((tm, d_model), lambda i: (i, 0)),
                  pl.BlockSpec((tm, d_model), lambda i: (jnp.minimum(i + 1, n_tiles - 1), 0)),
                  pl.BlockSpec((tm, kvw2), lambda i: (i, 0)),
                  pl.BlockSpec((CHUNK, kvw2), lambda i: (jnp.maximum(i * cpt - 1, 0), 0)),
                  whole, whole, whole, whole, whole, whole],
        out_specs=pl.BlockSpec((tm, d_model), lambda i: (i, 0)),
        out_shape=jax.ShapeDtypeStruct((t_tokens, d_model), F32),
        scratch_shapes=[pltpu.VMEM((tm, bw), BF16), pltpu.VMEM((tm, bw), F32), pltpu.VMEM((tm, bw), BF16)],
        compiler_params=pltpu.CompilerParams(
            dimension_semantics=("arbitrary",),
            vmem_limit_bytes=V7X_VMEM_BYTES * 7 // 8),
        name="swa_layer",
    )(h1, h1, kv, kv, w_in, bias_tab, scale_tab, w_out, pg, pb)


def kernel(x, w_in_a, sgu_ln_g, sgu_ln_b, w_spatial, b_spatial, w_out_a, w_kv, w_in_b, attn_sinks,
           rel_bias, w_out_b, post_ln_g, post_ln_b):
    bsz, seq, d_model = x.shape
    depth = post_ln_g.shape[0]
    assert w_in_a.shape[0] == 1 and w_in_b.shape[0] == 1 and depth == 2
    assert seq % CHUNK == 0 and w_spatial.shape[1:] == (A_GROUPS, CHUNK, CHUNK)
    assert w_kv.shape[1] == 2 * LANES and rel_bias.shape == (REL_BUCKETS, w_out_b.shape[1] // HEAD_DIM)
    alpha = (2.0 * depth) ** 0.25
    aw = w_out_a.shape[1]
    gd = aw // A_GROUPS
    tm = 4 * CHUNK

    x2 = x.reshape(bsz * seq, d_model)
    row = lambda a: a.reshape(1, -1)
    bsp_b = jnp.broadcast_to(b_spatial[0][:, :, None], (A_GROUPS, CHUNK, gd))
    h1, kv = _sgu_layer(x2, w_in_a[0].astype(BF16), row(sgu_ln_g[0]), row(sgu_ln_b[0]), w_spatial[0], bsp_b,
                        w_out_a[0].astype(BF16), w_kv.astype(BF16), row(post_ln_g[0]), row(post_ln_b[0]),
                        alpha=alpha, tm=tm)
    bias_tab = _rel_bias_table(rel_bias, attn_sinks[0])
    out = _swa_layer(h1, kv, w_in_b[0].astype(BF16), bias_tab, jnp.asarray(_band_scale_table()),
                     w_out_b[0].astype(BF16), row(post_ln_g[1]), row(post_ln_b[1]),
                     alpha=alpha, tm=tm, seq=seq)
    return out.reshape(bsz, seq, d_model)
```

```python
import functools
import math

import numpy as np
import jax
import jax.numpy as jnp
from jax import lax
from jax.experimental import pallas as pl
from jax.experimental.pallas import tpu as pltpu

F32 = jnp.float32
BF16 = jnp.bfloat16

CHUNK = 128
A_GROUPS = 8
HEAD_DIM = 64
Q_PER_KV = 8
REL_BUCKETS = 32
REL_MAX_DIST = 128
LN_EPS = 1e-5
NEG_INF = -1e30
LOG2E = math.log2(math.e)

LANES = 128
PROJ_ROWS = 2 * CHUNK
DEN_ROWS = 16
SPATIAL_LAG = 3
V7X_VMEM_BYTES = 64 * 1024 * 1024


def _bucket_lower_bounds():
    max_exact = REL_BUCKETS // 2
    d = np.arange(REL_MAX_DIST, dtype=np.int32)
    df = np.maximum(d, 1).astype(np.float32)
    large = max_exact + (np.log(df / np.float32(max_exact)) / np.float32(math.log(REL_MAX_DIST / max_exact))
                         * np.float32(REL_BUCKETS - max_exact)).astype(np.int32)
    large = np.minimum(large, REL_BUCKETS - 1)
    bucket = np.where(d < max_exact, d, large)
    assert np.all(np.diff(bucket) >= 0) and bucket[0] == 0
    return [int(np.argmax(bucket >= b)) if np.any(bucket >= b) else REL_MAX_DIST for b in range(REL_BUCKETS)]


def _band_scale_table():
    j = np.arange(2 * CHUNK)[:, None]
    t = np.arange(CHUNK)[None, :]
    d = t + CHUNK - j
    in_window = (d >= 0) & (d < CHUNK)
    first = in_window & (j >= CHUNK)
    tab = np.stack([np.tile(in_window, (2, 1)), np.tile(first, (2, 1))]).astype(np.float32)
    return tab * np.float32(LOG2E)


def _layer_norm(x, g, b):
    mu = jnp.mean(x, axis=-1, keepdims=True)
    d = x - mu
    var = jnp.mean(d * d, axis=-1, keepdims=True)
    return d * lax.rsqrt(var + LN_EPS) * g + b


def _silu(z):
    return z / (1.0 + jnp.exp(-z))


def _rel_bias_kernel(rb_ref, sink_ref, out_ref, *, lower_bounds, pairs_per_kv):
    pair = pl.program_id(0) * pairs_per_kv + pl.program_id(1)
    j = lax.broadcasted_iota(jnp.int32, (2 * CHUNK, CHUNK), 0)
    t = lax.broadcasted_iota(jnp.int32, (2 * CHUNK, CHUNK), 1)
    dist = t + CHUNK - j
    in_window = (dist >= 0) & (dist < CHUNK)
    d = jnp.clip(dist, 0, REL_MAX_DIST - 1)
    for e in range(2):
        h = 2 * pair + e
        acc = jnp.full((2 * CHUNK, CHUNK), rb_ref[0, h], F32)
        for b in range(1, REL_BUCKETS):
            if lower_bounds[b] < REL_MAX_DIST:
                acc = jnp.where(d >= lower_bounds[b], rb_ref[b, h], acc)
        hidden = jnp.where(j == 0, sink_ref[h], NEG_INF)
        for variant in range(2):
            visible = in_window & (j >= variant * CHUNK)
            out_ref[variant, 0, e * 2 * CHUNK:(e + 1) * 2 * CHUNK, :] = jnp.where(visible, acc, hidden) * LOG2E


def _rel_bias_table(rel_bias, sinks):
    n_heads = rel_bias.shape[1]
    n_kv = n_heads // Q_PER_KV
    pairs_per_kv = Q_PER_KV // 2
    smem = pl.BlockSpec(memory_space=pltpu.SMEM)
    return pl.pallas_call(
        functools.partial(_rel_bias_kernel, lower_bounds=_bucket_lower_bounds(), pairs_per_kv=pairs_per_kv),
        grid=(n_kv, pairs_per_kv),
        in_specs=[smem, smem],
        out_specs=pl.BlockSpec((2, 1, 4 * CHUNK, CHUNK), lambda k, p: (0, k, 0, p)),
        out_shape=jax.ShapeDtypeStruct((2, n_kv, 4 * CHUNK, pairs_per_kv * CHUNK), F32),
        name="rel_bias_table",
    )(rel_bias, sinks)


def _sgu_kernel(x_ref, xn_ref, w_in_ref, lng_ref, lnb_ref, ws_ref, bsp_ref, w_out_ref, w_kv_ref,
                pg_ref, pb_ref, h_ref, kv_ref, v_s, y_s, *, alpha):
    tm, _ = x_ref.shape
    aw = v_s.shape[1]
    gd = aw // A_GROUPS
    n_chunks = tm // CHUNK
    n_blocks = tm // PROJ_ROWS

    def project_v(src_ref, b):
        rows = slice(b * PROJ_ROWS, (b + 1) * PROJ_ROWS)
        v_s[rows, :] = jnp.dot(src_ref[rows, :].astype(BF16), w_in_ref[:, aw:2 * aw],
                               preferred_element_type=F32)

    @pl.when(pl.program_id(0) == 0)
    def _():
        for b in range(n_blocks):
            project_v(x_ref, b)

    xb = x_ref[...].astype(BF16)
    v = v_s[...]
    mu = jnp.mean(v, axis=-1, keepdims=True)
    rstd = lax.rsqrt(jnp.mean(jnp.square(v - mu), axis=-1, keepdims=True) + LN_EPS)

    row = lax.broadcasted_iota(jnp.int32, (CHUNK, CHUNK), 0)
    col = lax.broadcasted_iota(jnp.int32, (CHUNK, CHUNK), 1)
    tril = col <= row

    def gate_of(g):
        u = jnp.dot(xb, w_in_ref[:, g * gd:(g + 1) * gd], preferred_element_type=F32)
        z = jnp.dot(xb, w_in_ref[:, 2 * aw + g * gd:2 * aw + (g + 1) * gd], preferred_element_type=F32)
        return u * _silu(z)

    def spatial(g, gate):
        cols = slice(g * gd, (g + 1) * gd)
        ws = jnp.where(tril, ws_ref[g], 0.0).astype(BF16)
        vn = ((v_s[:, cols] - mu) * rstd * lng_ref[:, cols] + lnb_ref[:, cols]).astype(BF16)
        for c in range(n_chunks):
            rows = slice(c * CHUNK, (c + 1) * CHUNK)
            s = jnp.dot(ws, vn[rows], preferred_element_type=F32) + bsp_ref[g]
            y_s[rows, cols] = (gate[rows] * s).astype(BF16)

    gates = {g: gate_of(g) for g in range(SPATIAL_LAG)}
    for g in range(A_GROUPS):
        if g + SPATIAL_LAG < A_GROUPS:
            gates[g + SPATIAL_LAG] = gate_of(g + SPATIAL_LAG)
        spatial(g, gates.pop(g))

    def kv_of(b, h):
        for r in range(b * PROJ_ROWS, (b + 1) * PROJ_ROWS, CHUNK):
            hr = h[r - b * PROJ_ROWS:r - b * PROJ_ROWS + CHUNK]
            kv_ref[r:r + CHUNK, :] = jnp.dot(hr.astype(BF16), w_kv_ref[...],
                                             preferred_element_type=F32).astype(BF16)

    h_prev = None
    for b in range(n_blocks):
        rows = slice(b * PROJ_ROWS, (b + 1) * PROJ_ROWS)
        sub = jnp.dot(y_s[rows, :], w_out_ref[...], preferred_element_type=F32)
        if h_prev is not None:
            kv_of(b - 1, h_prev)
        project_v(xn_ref, b)
        h_prev = _layer_norm(alpha * x_ref[rows, :] + sub, pg_ref[...], pb_ref[...])
        h_ref[rows, :] = h_prev
    kv_of(n_blocks - 1, h_prev)


def _sgu_layer(x2, w_in, ln_g, ln_b, w_spatial, bsp_b, w_out, w_kv, pg, pb, *, alpha, tm):
    t_tokens, d_model = x2.shape
    aw = w_out.shape[0]
    kvw = w_kv.shape[1]
    n_tiles = t_tokens // tm
    whole = pl.BlockSpec(memory_space=pltpu.VMEM)
    return pl.pallas_call(
        functools.partial(_sgu_kernel, alpha=alpha),
        grid=(n_tiles,),
        in_specs=[pl.BlockSpec((tm, d_model), lambda i: (i, 0)),
                  pl.BlockSpec((tm, d_model), lambda i: (jnp.minimum(i + 1, n_tiles - 1), 0)),
                  whole, whole, whole, whole, whole, whole, whole, whole, whole],
        out_specs=[pl.BlockSpec((tm, d_model), lambda i: (i, 0)),
                   pl.BlockSpec((tm, kvw), lambda i: (i, 0))],
        out_shape=[jax.ShapeDtypeStruct((t_tokens, d_model), F32),
                   jax.ShapeDtypeStruct((t_tokens, kvw), BF16)],
        scratch_shapes=[pltpu.VMEM((tm, aw), F32), pltpu.VMEM((tm, aw), BF16)],
        compiler_params=pltpu.CompilerParams(
            dimension_semantics=("arbitrary",),
            vmem_limit_bytes=V7X_VMEM_BYTES * 7 // 8),
        name="sgu_layer",
    )(x2, x2, w_in, ln_g, ln_b, w_spatial, bsp_b, w_out, w_kv, pg, pb)


def _swa_kernel(h_ref, hn_ref, kv_ref, kvp_ref, w_in_ref, bias_ref, scale_ref, w_out_ref,
                pg_ref, pb_ref, o_ref, q_s, g_s, y_s, *, alpha, tiles_per_seq):
    tm, _ = h_ref.shape
    bw = y_s.shape[1]
    n_kv = bw // (HEAD_DIM * Q_PER_KV)
    pairs_per_kv = Q_PER_KV // 2
    i = pl.program_id(0)

    n_blocks = tm // PROJ_ROWS
    chunks_per_block = PROJ_ROWS // CHUNK

    def project_q(src_ref, b):
        rows = slice(b * PROJ_ROWS, (b + 1) * PROJ_ROWS)
        q_s[rows, :] = (jnp.dot(src_ref[rows, :].astype(BF16), w_in_ref[:, :bw], preferred_element_type=F32)
                        * (HEAD_DIM ** -0.5)).astype(BF16)

    def project_z(src_ref, b):
        rows = slice(b * PROJ_ROWS, (b + 1) * PROJ_ROWS)
        g_s[rows, :] = _silu(jnp.dot(src_ref[rows, :].astype(BF16), w_in_ref[:, bw:],
                                     preferred_element_type=F32))

    @pl.when(i == 0)
    def _():
        for b in range(n_blocks):
            project_q(h_ref, b)
            project_z(h_ref, b)

    as_bf16 = lambda a: a.astype(F32).astype(BF16)
    lo = as_bf16(lax.broadcasted_iota(jnp.int32, (2 * CHUNK, LANES), 1)) < HEAD_DIM
    key0 = as_bf16(lax.broadcasted_iota(jnp.int32, (2 * CHUNK, LANES), 0)) < 1
    zeros_k = jnp.zeros((2 * CHUNK, LANES), BF16)
    zeros_v = jnp.zeros((HEAD_DIM, 2 * CHUNK), BF16)
    zeros_d = jnp.zeros((DEN_ROWS, 2 * CHUNK), BF16)
    ones_d = jnp.ones((DEN_ROWS, 2 * CHUNK), BF16)
    first = (i % tiles_per_seq == 0).astype(jnp.int32)

    bands = {}

    def band_of(c):
        if c not in bands:
            rows = slice(c * CHUNK, (c + 1) * CHUNK)
            kv_prev = kvp_ref[...] if c == 0 else kv_ref[(c - 1) * CHUNK:c * CHUNK, :]
            band = jnp.concatenate([kv_prev, kv_ref[rows, :]], axis=0)
            kb, vb = band[:, :LANES], band[:, LANES:]
            kr = pltpu.roll(kb, HEAD_DIM, axis=1)
            v_t = jnp.where(key0, zeros_k, vb).astype(F32).T.astype(BF16)
            bands[c] = (kb, kr, v_t)
        return bands[c]

    def scores_of(c, kvh):
        rows = slice(c * CHUNK, (c + 1) * CHUNK)
        kb, kr, v_t = band_of(c)
        variant = first if c == 0 else 0
        k_src, k_alt = (kb, kr) if kvh == 0 else (kr, kb)
        k_cat = jnp.concatenate([jnp.where(lo, k_src, zeros_k), jnp.where(lo, zeros_k, k_alt)], axis=0)
        vk = v_t[kvh * HEAD_DIM:(kvh + 1) * HEAD_DIM]
        v_lhs = jnp.concatenate(
            [jnp.concatenate([vk, zeros_v], axis=1), jnp.concatenate([zeros_v, vk], axis=1),
             jnp.concatenate([ones_d, zeros_d], axis=1), jnp.concatenate([zeros_d, ones_d], axis=1)],
            axis=0)
        pair0 = kvh * pairs_per_kv
        q_stack = jnp.concatenate(
            [q_s[rows, (pair0 + p) * LANES:(pair0 + p + 1) * LANES] for p in range(pairs_per_kv)], axis=0)
        s_t = lax.dot_general(k_cat, q_stack, (((1,), (1,)), ((), ())),
                              preferred_element_type=F32)
        return rows, variant, kvh, s_t, v_lhs

    def attend(rows, variant, kvh, s_t, v_lhs):
        pair0 = kvh * pairs_per_kv
        e_cols = []
        for p in range(pairs_per_kv):
            qc = slice(p * CHUNK, (p + 1) * CHUNK)
            halves = []
            for e in range(2):
                kr_ = slice(e * 2 * CHUNK, (e + 1) * 2 * CHUNK)
                le = s_t[kr_, qc] * scale_ref[variant, kr_, :] + bias_ref[variant, kvh, kr_, qc]
                halves.append(jnp.exp2(le - jnp.max(le, axis=0, keepdims=True)).astype(BF16))
            e_cols.append(jnp.concatenate(halves, axis=0))
        pv_t = jnp.dot(v_lhs, jnp.concatenate(e_cols, axis=1), preferred_element_type=F32)
        n_num = 2 * HEAD_DIM
        sub = 8
        inv_e = 1.0 / pv_t[n_num:n_num + sub]
        inv_o = 1.0 / pv_t[n_num + DEN_ROWS:n_num + DEN_ROWS + sub]
        inv = jnp.concatenate([inv_e] * (HEAD_DIM // sub) + [inv_o] * (HEAD_DIM // sub), axis=0)
        o_t = pv_t[:n_num] * inv
        for p in range(pairs_per_kv):
            pc = slice((pair0 + p) * LANES, (pair0 + p + 1) * LANES)
            y_s[rows, pc] = (o_t[:, p * CHUNK:(p + 1) * CHUNK].T * g_s[rows, pc]).astype(BF16)

    block_units = [[(c, kvh) for c in range(b * chunks_per_block, (b + 1) * chunks_per_block)
                    for kvh in range(n_kv)] for b in range(n_blocks)]
    scored = [scores_of(*u) for u in block_units[0]]
    for b in range(n_blocks):
        rows = slice(b * PROJ_ROWS, (b + 1) * PROJ_ROWS)
        scored_next = []
        for j, unit in enumerate(scored):
            attend(*unit)
            if b + 1 < n_blocks:
                scored_next.append(scores_of(*block_units[b + 1][j]))
        scored = scored_next
        project_q(hn_ref, b)
        sub = jnp.dot(y_s[rows, :], w_out_ref[...], preferred_element_type=F32)
        project_z(hn_ref, b)
        o_ref[rows, :] = _layer_norm(alpha * h_ref[rows, :] + sub, pg_ref[...], pb_ref[...])


def _swa_layer(h1, kv, w_in, bias_tab, scale_tab, w_out, pg, pb, *, alpha, tm, seq):
    t_tokens, d_model = h1.shape
    bw = w_out.shape[0]
    kvw2 = kv.shape[1]
    cpt = tm // CHUNK
    n_tiles = t_tokens // tm
    whole = pl.BlockSpec(memory_space=pltpu.VMEM)
    return pl.pallas_call(
        functools.partial(_swa_kernel, alpha=alpha, tiles_per_seq=seq // tm),
        grid=(n_tiles,),
        in_specs=[pl.BlockSpec((tm, d_model), lambda i: (i, 0)),
                  pl.BlockSpec((tm, d_model), lambda i: (jnp.minimum(i + 1, n_tiles - 1), 0)),
                  pl.BlockSpec((tm, kvw2), lambda i: (i, 0)),
                  pl.BlockSpec((CHUNK, kvw2), lambda i: (jnp.maximum(i * cpt - 1, 0), 0)),
                  whole, whole, whole, whole, whole, whole],
        out_specs=pl.BlockSpec((tm, d_model), lambda i: (i, 0)),
        out_shape=jax.ShapeDtypeStruct((t_tokens, d_model), F32),
        scratch_shapes=[pltpu.VMEM((tm, bw), BF16), pltpu.VMEM((tm, bw), F32), pltpu.VMEM((tm, bw), BF16)],
        compiler_params=pltpu.CompilerParams(
            dimension_semantics=("arbitrary",),
            vmem_limit_bytes=V7X_VMEM_BYTES * 7 // 8),
        name="swa_layer",
    )(h1, h1, kv, kv, w_in, bias_tab, scale_tab, w_out, pg, pb)


def kernel(x, w_in_a, sgu_ln_g, sgu_ln_b, w_spatial, b_spatial, w_out_a, w_kv, w_in_b, attn_sinks,
           rel_bias, w_out_b, post_ln_g, post_ln_b):
    bsz, seq, d_model = x.shape
    depth = post_ln_g.shape[0]
    assert w_in_a.shape[0] == 1 and w_in_b.shape[0] == 1 and depth == 2
    assert seq % CHUNK == 0 and w_spatial.shape[1:] == (A_GROUPS, CHUNK, CHUNK)
    assert w_kv.shape[1] == 2 * LANES and rel_bias.shape == (REL_BUCKETS, w_out_b.shape[1] // HEAD_DIM)
    alpha = (2.0 * depth) ** 0.25
    aw = w_out_a.shape[1]
    gd = aw // A_GROUPS
    tm = 4 * CHUNK

    x2 = x.reshape(bsz * seq, d_model)
    row = lambda a: a.reshape(1, -1)
    bsp_b = jnp.broadcast_to(b_spatial[0][:, :, None], (A_GROUPS, CHUNK, gd))
    h1, kv = _sgu_layer(x2, w_in_a[0].astype(BF16), row(sgu_ln_g[0]), row(sgu_ln_b[0]), w_spatial[0], bsp_b,
                        w_out_a[0].astype(BF16), w_kv.astype(BF16), row(post_ln_g[0]), row(post_ln_b[0]),
                        alpha=alpha, tm=tm)
    bias_tab = _rel_bias_table(rel_bias, attn_sinks[0])
    out = _swa_layer(h1, kv, w_in_b[0].astype(BF16), bias_tab, jnp.asarray(_band_scale_table()),
                     w_out_b[0].astype(BF16), row(post_ln_g[1]), row(post_ln_b[1]),
                     alpha=alpha, tm=tm, seq=seq)
    return out.reshape(bsz, seq, d_model)
```

```python
import functools
import math

import numpy as np
import jax
import jax.numpy as jnp
from jax import lax
from jax.experimental import pallas as pl
from jax.experimental.pallas import tpu as pltpu

F32 = jnp.float32
BF16 = jnp.bfloat16

CHUNK = 128
A_GROUPS = 8
HEAD_DIM = 64
Q_PER_KV = 8
REL_BUCKETS = 32
REL_MAX_DIST = 128
LN_EPS = 1e-5
NEG_INF = -1e30
LOG2E = math.log2(math.e)

LANES = 128
PROJ_ROWS = 2 * CHUNK
DEN_ROWS = 16
SPATIAL_LAG = 3
V7X_VMEM_BYTES = 64 * 1024 * 1024
STAGE_BYTES = 1024 * 1024


def _bucket_lower_bounds():
    max_exact = REL_BUCKETS // 2
    d = np.arange(REL_MAX_DIST, dtype=np.int32)
    df = np.maximum(d, 1).astype(np.float32)
    large = max_exact + (np.log(df / np.float32(max_exact)) / np.float32(math.log(REL_MAX_DIST / max_exact))
                         * np.float32(REL_BUCKETS - max_exact)).astype(np.int32)
    large = np.minimum(large, REL_BUCKETS - 1)
    bucket = np.where(d < max_exact, d, large)
    assert np.all(np.diff(bucket) >= 0) and bucket[0] == 0
    return [int(np.argmax(bucket >= b)) if np.any(bucket >= b) else REL_MAX_DIST for b in range(REL_BUCKETS)]


def _band_scale_table():
    j = np.arange(2 * CHUNK)[:, None]
    t = np.arange(CHUNK)[None, :]
    d = t + CHUNK - j
    in_window = (d >= 0) & (d < CHUNK)
    first = in_window & (j >= CHUNK)
    tab = np.stack([np.tile(in_window, (2, 1)), np.tile(first, (2, 1))]).astype(np.float32)
    return tab * np.float32(LOG2E)


def _layer_norm(x, g, b):
    mu = jnp.mean(x, axis=-1, keepdims=True)
    d = x - mu
    var = jnp.mean(d * d, axis=-1, keepdims=True)
    return d * lax.rsqrt(var + LN_EPS) * g + b


def _silu(z):
    return z / (1.0 + jnp.exp(-z))


def _stage_rows(w_hbm):
    k, n = w_hbm.shape
    rows = 8
    while 2 * rows * n * 4 <= STAGE_BYTES and k % (2 * rows) == 0:
        rows *= 2
    assert k % rows == 0
    return rows


def _cast_weight(w_hbm, w_s, stage, sems, k):
    rows = stage.shape[1]
    n = w_hbm.shape[0] // rows

    def copy(c):
        return pltpu.make_async_copy(w_hbm.at[pl.ds(c * rows, rows), :], stage.at[c % 2], sems.at[k, c % 2])

    copy(0).start()
    for c in range(n):
        if c + 1 < n:
            copy(c + 1).start()
        copy(c).wait()
        w_s[c * rows:(c + 1) * rows, :] = stage[c % 2].astype(BF16)


def _rel_bias_kernel(rb_ref, sink_ref, out_ref, *, lower_bounds, pairs_per_kv):
    pair = pl.program_id(0) * pairs_per_kv + pl.program_id(1)
    j = lax.broadcasted_iota(jnp.int32, (2 * CHUNK, CHUNK), 0)
    t = lax.broadcasted_iota(jnp.int32, (2 * CHUNK, CHUNK), 1)
    dist = t + CHUNK - j
    in_window = (dist >= 0) & (dist < CHUNK)
    d = jnp.clip(dist, 0, REL_MAX_DIST - 1)
    for e in range(2):
        h = 2 * pair + e
        acc = jnp.full((2 * CHUNK, CHUNK), rb_ref[0, h], F32)
        for b in range(1, REL_BUCKETS):
            if lower_bounds[b] < REL_MAX_DIST:
                acc = jnp.where(d >= lower_bounds[b], rb_ref[b, h], acc)
        hidden = jnp.where(j == 0, sink_ref[h], NEG_INF)
        for variant in range(2):
            visible = in_window & (j >= variant * CHUNK)
            out_ref[variant, 0, e * 2 * CHUNK:(e + 1) * 2 * CHUNK, :] = jnp.where(visible, acc, hidden) * LOG2E


def _rel_bias_table(rel_bias, sinks):
    n_heads = rel_bias.shape[1]
    n_kv = n_heads // Q_PER_KV
    pairs_per_kv = Q_PER_KV // 2
    smem = pl.BlockSpec(memory_space=pltpu.SMEM)
    return pl.pallas_call(
        functools.partial(_rel_bias_kernel, lower_bounds=_bucket_lower_bounds(), pairs_per_kv=pairs_per_kv),
        grid=(n_kv, pairs_per_kv),
        in_specs=[smem, smem],
        out_specs=pl.BlockSpec((2, 1, 4 * CHUNK, CHUNK), lambda k, p: (0, k, 0, p)),
        out_shape=jax.ShapeDtypeStruct((2, n_kv, 4 * CHUNK, pairs_per_kv * CHUNK), F32),
        name="rel_bias_table",
    )(rel_bias, sinks)


def _sgu_kernel(x_ref, xn_ref, w_in_hbm, lng_ref, lnb_ref, ws_ref, bsp_ref, w_out_hbm, w_kv_hbm,
                pg_ref, pb_ref, h_ref, kv_ref, v_s, y_s, w_in_ref, w_out_ref, w_kv_ref,
                st_in, st_out, st_kv, sems, *, alpha):
    tm, _ = x_ref.shape
    aw = v_s.shape[1]
    gd = aw // A_GROUPS
    n_chunks = tm // CHUNK
    n_blocks = tm // PROJ_ROWS

    def project_v(src_ref, b):
        rows = slice(b * PROJ_ROWS, (b + 1) * PROJ_ROWS)
        v_s[rows, :] = jnp.dot(src_ref[rows, :].astype(BF16), w_in_ref[:, aw:2 * aw],
                               preferred_element_type=F32)

    @pl.when(pl.program_id(0) == 0)
    def _():
        _cast_weight(w_in_hbm, w_in_ref, st_in, sems, 0)
        _cast_weight(w_out_hbm, w_out_ref, st_out, sems, 1)
        _cast_weight(w_kv_hbm, w_kv_ref, st_kv, sems, 2)
        for b in range(n_blocks):
            project_v(x_ref, b)

    xb = x_ref[...].astype(BF16)
    v = v_s[...]
    mu = jnp.mean(v, axis=-1, keepdims=True)
    rstd = lax.rsqrt(jnp.mean(jnp.square(v - mu), axis=-1, keepdims=True) + LN_EPS)

    row = lax.broadcasted_iota(jnp.int32, (CHUNK, CHUNK), 0)
    col = lax.broadcasted_iota(jnp.int32, (CHUNK, CHUNK), 1)
    tril = col <= row

    def gate_of(g):
        u = jnp.dot(xb, w_in_ref[:, g * gd:(g + 1) * gd], preferred_element_type=F32)
        z = jnp.dot(xb, w_in_ref[:, 2 * aw + g * gd:2 * aw + (g + 1) * gd], preferred_element_type=F32)
        return u * _silu(z)

    def spatial(g, gate):
        cols = slice(g * gd, (g + 1) * gd)
        ws = jnp.where(tril, ws_ref[g], 0.0).astype(BF16)
        vn = ((v_s[:, cols] - mu) * rstd * lng_ref[:, cols] + lnb_ref[:, cols]).astype(BF16)
        for c in range(n_chunks):
            rows = slice(c * CHUNK, (c + 1) * CHUNK)
            s = jnp.dot(ws, vn[rows], preferred_element_type=F32) + bsp_ref[g]
            y_s[rows, cols] = (gate[rows] * s).astype(BF16)

    gates = {g: gate_of(g) for g in range(SPATIAL_LAG)}
    for g in range(A_GROUPS):
        if g + SPATIAL_LAG < A_GROUPS:
            gates[g + SPATIAL_LAG] = gate_of(g + SPATIAL_LAG)
        spatial(g, gates.pop(g))

    def kv_of(b, h):
        for r in range(b * PROJ_ROWS, (b + 1) * PROJ_ROWS, CHUNK):
            hr = h[r - b * PROJ_ROWS:r - b * PROJ_ROWS + CHUNK]
            kv_ref[r:r + CHUNK, :] = jnp.dot(hr.astype(BF16), w_kv_ref[...],
                                             preferred_element_type=F32).astype(BF16)

    h_prev = None
    for b in range(n_blocks):
        rows = slice(b * PROJ_ROWS, (b + 1) * PROJ_ROWS)
        sub = jnp.dot(y_s[rows, :], w_out_ref[...], preferred_element_type=F32)
        if h_prev is not None:
            kv_of(b - 1, h_prev)
        project_v(xn_ref, b)
        h_prev = _layer_norm(alpha * x_ref[rows, :] + sub, pg_ref[...], pb_ref[...])
        h_ref[rows, :] = h_prev
    kv_of(n_blocks - 1, h_prev)


def _sgu_layer(x2, w_in, ln_g, ln_b, w_spatial, bsp_b, w_out, w_kv, pg, pb, *, alpha, tm):
    t_tokens, d_model = x2.shape
    aw = w_out.shape[0]
    kvw = w_kv.shape[1]
    n_tiles = t_tokens // tm
    whole = pl.BlockSpec(memory_space=pltpu.VMEM)
    hbm = pl.BlockSpec(memory_space=pl.ANY)
    stage = lambda w: pltpu.VMEM((2, _stage_rows(w), w.shape[1]), F32)
    return pl.pallas_call(
        functools.partial(_sgu_kernel, alpha=alpha),
        grid=(n_tiles,),
        in_specs=[pl.BlockSpec((tm, d_model), lambda i: (i, 0)),
                  pl.BlockSpec((tm, d_model), lambda i: (jnp.minimum(i + 1, n_tiles - 1), 0)),
                  hbm, whole, whole, whole, whole, hbm, hbm, whole, whole],
        out_specs=[pl.BlockSpec((tm, d_model), lambda i: (i, 0)),
                   pl.BlockSpec((tm, kvw), lambda i: (i, 0))],
        out_shape=[jax.ShapeDtypeStruct((t_tokens, d_model), F32),
                   jax.ShapeDtypeStruct((t_tokens, kvw), BF16)],
        scratch_shapes=[pltpu.VMEM((tm, aw), F32), pltpu.VMEM((tm, aw), BF16),
                        pltpu.VMEM(w_in.shape, BF16), pltpu.VMEM(w_out.shape, BF16), pltpu.VMEM(w_kv.shape, BF16),
                        stage(w_in), stage(w_out), stage(w_kv), pltpu.SemaphoreType.DMA((3, 2))],
        compiler_params=pltpu.CompilerParams(
            dimension_semantics=("arbitrary",),
            vmem_limit_bytes=V7X_VMEM_BYTES * 7 // 8),
        name="sgu_layer",
    )(x2, x2, w_in, ln_g, ln_b, w_spatial, bsp_b, w_out, w_kv, pg, pb)


def _swa_kernel(h_ref, hn_ref, kv_ref, kvp_ref, w_in_hbm, bias_ref, scale_ref, w_out_hbm,
                pg_ref, pb_ref, o_ref, q_s, g_s, y_s, w_in_ref, w_out_ref, st_in, st_out, sems,
                *, alpha, tiles_per_seq):
    tm, _ = h_ref.shape
    bw = y_s.shape[1]
    n_kv = bw // (HEAD_DIM * Q_PER_KV)
    pairs_per_kv = Q_PER_KV // 2
    i = pl.program_id(0)

    n_blocks = tm // PROJ_ROWS
    chunks_per_block = PROJ_ROWS // CHUNK

    def project_q(src_ref, b):
        rows = slice(b * PROJ_ROWS, (b + 1) * PROJ_ROWS)
        q_s[rows, :] = (jnp.dot(src_ref[rows, :].astype(BF16), w_in_ref[:, :bw], preferred_element_type=F32)
                        * (HEAD_DIM ** -0.5)).astype(BF16)

    def project_z(src_ref, b):
        rows = slice(b * PROJ_ROWS, (b + 1) * PROJ_ROWS)
        g_s[rows, :] = _silu(jnp.dot(src_ref[rows, :].astype(BF16), w_in_ref[:, bw:],
                                     preferred_element_type=F32))

    @pl.when(i == 0)
    def _():
        _cast_weight(w_in_hbm, w_in_ref, st_in, sems, 0)
        _cast_weight(w_out_hbm, w_out_ref, st_out, sems, 1)
        for b in range(n_blocks):
            project_q(h_ref, b)
            project_z(h_ref, b)

    as_bf16 = lambda a: a.astype(F32).astype(BF16)
    lo = as_bf16(lax.broadcasted_iota(jnp.int32, (2 * CHUNK, LANES), 1)) < HEAD_DIM
    key0 = as_bf16(lax.broadcasted_iota(jnp.int32, (2 * CHUNK, LANES), 0)) < 1
    zeros_k = jnp.zeros((2 * CHUNK, LANES), BF16)
    zeros_v = jnp.zeros((HEAD_DIM, 2 * CHUNK), BF16)
    zeros_d = jnp.zeros((DEN_ROWS, 2 * CHUNK), BF16)
    ones_d = jnp.ones((DEN_ROWS, 2 * CHUNK), BF16)
    first = (i % tiles_per_seq == 0).astype(jnp.int32)

    bands = {}

    def band_of(c):
        if c not in bands:
            rows = slice(c * CHUNK, (c + 1) * CHUNK)
            kv_prev = kvp_ref[...] if c == 0 else kv_ref[(c - 1) * CHUNK:c * CHUNK, :]
            band = jnp.concatenate([kv_prev, kv_ref[rows, :]], axis=0)
            kb, vb = band[:, :LANES], band[:, LANES:]
            kr = pltpu.roll(kb, HEAD_DIM, axis=1)
            v_t = jnp.where(key0, zeros_k, vb).astype(F32).T.astype(BF16)
            bands[c] = (kb, kr, v_t)
        return bands[c]

    def scores_of(c, kvh):
        rows = slice(c * CHUNK, (c + 1) * CHUNK)
        kb, kr, v_t = band_of(c)
        variant = first if c == 0 else 0
        k_src, k_alt = (kb, kr) if kvh == 0 else (kr, kb)
        k_cat = jnp.concatenate([jnp.where(lo, k_src, zeros_k), jnp.where(lo, zeros_k, k_alt)], axis=0)
        vk = v_t[kvh * HEAD_DIM:(kvh + 1) * HEAD_DIM]
        v_lhs = jnp.concatenate(
            [jnp.concatenate([vk, zeros_v], axis=1), jnp.concatenate([zeros_v, vk], axis=1),
             jnp.concatenate([ones_d, zeros_d], axis=1), jnp.concatenate([zeros_d, ones_d], axis=1)],
            axis=0)
        pair0 = kvh * pairs_per_kv
        q_stack = jnp.concatenate(
            [q_s[rows, (pair0 + p) * LANES:(pair0 + p + 1) * LANES] for p in range(pairs_per_kv)], axis=0)
        s_t = lax.dot_general(k_cat, q_stack, (((1,), (1,)), ((), ())),
                              preferred_element_type=F32)
        return rows, variant, kvh, s_t, v_lhs

    def attend(rows, variant, kvh, s_t, v_lhs):
        pair0 = kvh * pairs_per_kv
        e_cols = []
        for p in range(pairs_per_kv):
            qc = slice(p * CHUNK, (p + 1) * CHUNK)
            halves = []
            for e in range(2):
                kr_ = slice(e * 2 * CHUNK, (e + 1) * 2 * CHUNK)
                le = s_t[kr_, qc] * scale_ref[variant, kr_, :] + bias_ref[variant, kvh, kr_, qc]
                halves.append(jnp.exp2(le - jnp.max(le, axis=0, keepdims=True)).astype(BF16))
            e_cols.append(jnp.concatenate(halves, axis=0))
        pv_t = jnp.dot(v_lhs, jnp.concatenate(e_cols, axis=1), preferred_element_type=F32)
        n_num = 2 * HEAD_DIM
        sub = 8
        inv_e = 1.0 / pv_t[n_num:n_num + sub]
        inv_o = 1.0 / pv_t[n_num + DEN_ROWS:n_num + DEN_ROWS + sub]
        inv = jnp.concatenate([inv_e] * (HEAD_DIM // sub) + [inv_o] * (HEAD_DIM // sub), axis=0)
        o_t = pv_t[:n_num] * inv
        for p in range(pairs_per_kv):
            pc = slice((pair0 + p) * LANES, (pair0 + p + 1) * LANES)
            y_s[rows, pc] = (o_t[:, p * CHUNK:(p + 1) * CHUNK].T * g_s[rows, pc]).astype(BF16)

    block_units = [[(c, kvh) for c in range(b * chunks_per_block, (b + 1) * chunks_per_block)
                    for kvh in range(n_kv)] for b in range(n_blocks)]
    scored = [scores_of(*u) for u in block_units[0]]
    for b in range(n_blocks):
        rows = slice(b * PROJ_ROWS, (b + 1) * PROJ_ROWS)
        scored_next = []
        for j, unit in enumerate(scored):
            attend(*unit)
            if b + 1 < n_blocks:
                scored_next.append(scores_of(*block_units[b + 1][j]))
        scored = scored_next
        project_q(hn_ref, b)
        sub = jnp.dot(y_s[rows, :], w_out_ref[...], preferred_element_type=F32)
        project_z(hn_ref, b)
        o_ref[rows, :] = _layer_norm(alpha * h_ref[rows, :] + sub, pg_ref[...], pb_ref[...])


def _swa_layer(h1, kv, w_in, bias_tab, scale_tab, w_out, pg, pb, *, alpha, tm, seq):
    t_tokens, d_model = h1.shape
    bw = w_out.shape[0]
    kvw2 = kv.shape[1]
    cpt = tm // CHUNK
    n_tiles = t_tokens // tm
    whole = pl.BlockSpec(memory_space=pltpu.VMEM)
    hbm = pl.BlockSpec(memory_space=pl.ANY)
    stage = lambda w: pltpu.VMEM((2, _stage_rows(w), w.shape[1]), F32)
    return pl.pallas_call(
        functools.partial(_swa_kernel, alpha=alpha, tiles_per_seq=seq // tm),
        grid=(n_tiles,),
        in_specs=[pl.BlockSpec((tm, d_model), lambda i: (i, 0)),
                  pl.BlockSpec((tm, d_model), lambda i: (jnp.minimum(i + 1, n_tiles - 1), 0)),
                  pl.BlockSpec((tm, kvw2), lambda i: (i, 0)),
                  pl.BlockSpec((CHUNK, kvw2), lambda i: (jnp.maximum(i * cpt - 1, 0), 0)),
                  hbm, whole, whole, hbm, whole, whole],
        out_specs=pl.BlockSpec((tm, d_model), lambda i: (i, 0)),
        out_shape=jax.ShapeDtypeStruct((t_tokens, d_model), F32),
        scratch_shapes=[pltpu.VMEM((tm, bw), BF16), pltpu.VMEM((tm, bw), F32), pltpu.VMEM((tm, bw), BF16),
                        pltpu.VMEM(w_in.shape, BF16), pltpu.VMEM(w_out.shape, BF16),
                        stage(w_in), stage(w_out), pltpu.SemaphoreType.DMA((2, 2))],
        compiler_params=pltpu.CompilerParams(
            dimension_semantics=("arbitrary",),
            vmem_limit_bytes=V7X_VMEM_BYTES * 7 // 8),
        name="swa_layer",
    )(h1, h1, kv, kv, w_in, bias_tab, scale_tab, w_out, pg, pb)


def kernel(x, w_in_a, sgu_ln_g, sgu_ln_b, w_spatial, b_spatial, w_out_a, w_kv, w_in_b, attn_sinks,
           rel_bias, w_out_b, post_ln_g, post_ln_b):
    bsz, seq, d_model = x.shape
    depth = post_ln_g.shape[0]
    assert w_in_a.shape[0] == 1 and w_in_b.shape[0] == 1 and depth == 2
    assert seq % CHUNK == 0 and w_spatial.shape[1:] == (A_GROUPS, CHUNK, CHUNK)
    assert w_kv.shape[1] == 2 * LANES and rel_bias.shape == (REL_BUCKETS, w_out_b.shape[1] // HEAD_DIM)
    alpha = (2.0 * depth) ** 0.25
    aw = w_out_a.shape[1]
    gd = aw // A_GROUPS
    tm = 4 * CHUNK

    x2 = x.reshape(bsz * seq, d_model)
    row = lambda a: a.reshape(1, -1)
    bsp_b = jnp.broadcast_to(b_spatial[0][:, :, None], (A_GROUPS, CHUNK, gd))
    h1, kv = _sgu_layer(x2, w_in_a[0], row(sgu_ln_g[0]), row(sgu_ln_b[0]), w_spatial[0], bsp_b,
                        w_out_a[0], w_kv, row(post_ln_g[0]), row(post_ln_b[0]),
                        alpha=alpha, tm=tm)
    bias_tab = _rel_bias_table(rel_bias, attn_sinks[0])
    out = _swa_layer(h1, kv, w_in_b[0], bias_tab, jnp.asarray(_band_scale_table()),
                     w_out_b[0], row(post_ln_g[1]), row(post_ln_b[1]),
                     alpha=alpha, tm=tm, seq=seq)
    return out.reshape(bsz, seq, d_model)
```

```python
import functools
import math

import numpy as np
import jax
import jax.numpy as jnp
from jax import lax
from jax.experimental import pallas as pl
from jax.experimental.pallas import tpu as pltpu

F32 = jnp.float32
BF16 = jnp.bfloat16

CHUNK = 128
A_GROUPS = 8
HEAD_DIM = 64
Q_PER_KV = 8
REL_BUCKETS = 32
REL_MAX_DIST = 128
LN_EPS = 1e-5
NEG_INF = -1e30
LOG2E = math.log2(math.e)

LANES = 128
PROJ_ROWS = 2 * CHUNK
DEN_ROWS = 16
SPATIAL_LAG = 3
V7X_VMEM_BYTES = 64 * 1024 * 1024
STAGE_BYTES = 512 * 1024
STAGE_SLOTS = 4


def _bucket_lower_bounds():
    max_exact = REL_BUCKETS // 2
    d = np.arange(REL_MAX_DIST, dtype=np.int32)
    df = np.maximum(d, 1).astype(np.float32)
    large = max_exact + (np.log(df / np.float32(max_exact)) / np.float32(math.log(REL_MAX_DIST / max_exact))
                         * np.float32(REL_BUCKETS - max_exact)).astype(np.int32)
    large = np.minimum(large, REL_BUCKETS - 1)
    bucket = np.where(d < max_exact, d, large)
    assert np.all(np.diff(bucket) >= 0) and bucket[0] == 0
    return [int(np.argmax(bucket >= b)) if np.any(bucket >= b) else REL_MAX_DIST for b in range(REL_BUCKETS)]


def _band_scale_table():
    j = np.arange(2 * CHUNK)[:, None]
    t = np.arange(CHUNK)[None, :]
    d = t + CHUNK - j
    in_window = (d >= 0) & (d < CHUNK)
    first = in_window & (j >= CHUNK)
    tab = np.stack([np.tile(in_window, (2, 1)), np.tile(first, (2, 1))]).astype(np.float32)
    return tab * np.float32(LOG2E)


def _layer_norm(x, g, b):
    mu = jnp.mean(x, axis=-1, keepdims=True)
    d = x - mu
    var = jnp.mean(d * d, axis=-1, keepdims=True)
    return d * lax.rsqrt(var + LN_EPS) * g + b


def _silu(z):
    return z / (1.0 + jnp.exp(-z))


def _stage_rows(w_hbm):
    k, n = w_hbm.shape
    rows = 8
    while 2 * rows * n * 4 <= STAGE_BYTES and k % (2 * rows) == 0:
        rows *= 2
    assert k % rows == 0
    return rows


def _cast_weight(w_hbm, w_s, stage, sems, k):
    rows = stage.shape[1]
    n = w_hbm.shape[0] // rows

    def copy(c):
        slot = c % STAGE_SLOTS
        return pltpu.make_async_copy(w_hbm.at[pl.ds(c * rows, rows), :], stage.at[slot], sems.at[k, slot])

    for c in range(min(STAGE_SLOTS - 1, n)):
        copy(c).start()
    for c in range(n):
        if c + STAGE_SLOTS - 1 < n:
            copy(c + STAGE_SLOTS - 1).start()
        copy(c).wait()
        w_s[c * rows:(c + 1) * rows, :] = stage[c % STAGE_SLOTS].astype(BF16)


def _rel_bias_kernel(rb_ref, sink_ref, out_ref, *, lower_bounds, pairs_per_kv):
    pair = pl.program_id(0) * pairs_per_kv + pl.program_id(1)
    j = lax.broadcasted_iota(jnp.int32, (2 * CHUNK, CHUNK), 0)
    t = lax.broadcasted_iota(jnp.int32, (2 * CHUNK, CHUNK), 1)
    dist = t + CHUNK - j
    in_window = (dist >= 0) & (dist < CHUNK)
    d = jnp.clip(dist, 0, REL_MAX_DIST - 1)
    for e in range(2):
        h = 2 * pair + e
        acc = jnp.full((2 * CHUNK, CHUNK), rb_ref[0, h], F32)
        for b in range(1, REL_BUCKETS):
            if lower_bounds[b] < REL_MAX_DIST:
                acc = jnp.where(d >= lower_bounds[b], rb_ref[b, h], acc)
        hidden = jnp.where(j == 0, sink_ref[h], NEG_INF)
        for variant in range(2):
            visible = in_window & (j >= variant * CHUNK)
            out_ref[variant, 0, e * 2 * CHUNK:(e + 1) * 2 * CHUNK, :] = jnp.where(visible, acc, hidden) * LOG2E


def _rel_bias_table(rel_bias, sinks):
    n_heads = rel_bias.shape[1]
    n_kv = n_heads // Q_PER_KV
    pairs_per_kv = Q_PER_KV // 2
    smem = pl.BlockSpec(memory_space=pltpu.SMEM)
    return pl.pallas_call(
        functools.partial(_rel_bias_kernel, lower_bounds=_bucket_lower_bounds(), pairs_per_kv=pairs_per_kv),
        grid=(n_kv, pairs_per_kv),
        in_specs=[smem, smem],
        out_specs=pl.BlockSpec((2, 1, 4 * CHUNK, CHUNK), lambda k, p: (0, k, 0, p)),
        out_shape=jax.ShapeDtypeStruct((2, n_kv, 4 * CHUNK, pairs_per_kv * CHUNK), F32),
        name="rel_bias_table",
    )(rel_bias, sinks)


def _sgu_kernel(x_ref, xn_ref, w_in_hbm, lng_ref, lnb_ref, ws_ref, bsp_ref, w_out_hbm, w_kv_hbm,
                pg_ref, pb_ref, h_ref, kv_ref, v_s, y_s, w_in_ref, w_out_ref, w_kv_ref,
                st_in, st_out, st_kv, sems, *, alpha):
    tm, _ = x_ref.shape
    aw = v_s.shape[1]
    gd = aw // A_GROUPS
    n_chunks = tm // CHUNK
    n_blocks = tm // PROJ_ROWS

    def project_v(src_ref, b):
        rows = slice(b * PROJ_ROWS, (b + 1) * PROJ_ROWS)
        v_s[rows, :] = jnp.dot(src_ref[rows, :].astype(BF16), w_in_ref[:, aw:2 * aw],
                               preferred_element_type=F32)

    @pl.when(pl.program_id(0) == 0)
    def _():
        _cast_weight(w_in_hbm, w_in_ref, st_in, sems, 0)
        _cast_weight(w_out_hbm, w_out_ref, st_out, sems, 1)
        _cast_weight(w_kv_hbm, w_kv_ref, st_kv, sems, 2)
        for b in range(n_blocks):
            project_v(x_ref, b)

    xb = x_ref[...].astype(BF16)
    v = v_s[...]
    mu = jnp.mean(v, axis=-1, keepdims=True)
    rstd = lax.rsqrt(jnp.mean(jnp.square(v - mu), axis=-1, keepdims=True) + LN_EPS)

    row = lax.broadcasted_iota(jnp.int32, (CHUNK, CHUNK), 0)
    col = lax.broadcasted_iota(jnp.int32, (CHUNK, CHUNK), 1)
    tril = col <= row

    def gate_of(g):
        u = jnp.dot(xb, w_in_ref[:, g * gd:(g + 1) * gd], preferred_element_type=F32)
        z = jnp.dot(xb, w_in_ref[:, 2 * aw + g * gd:2 * aw + (g + 1) * gd], preferred_element_type=F32)
        return u * _silu(z)

    def spatial(g, gate):
        cols = slice(g * gd, (g + 1) * gd)
        ws = jnp.where(tril, ws_ref[g], 0.0).astype(BF16)
        vn = ((v_s[:, cols] - mu) * rstd * lng_ref[:, cols] + lnb_ref[:, cols]).astype(BF16)
        for c in range(n_chunks):
            rows = slice(c * CHUNK, (c + 1) * CHUNK)
            s = jnp.dot(ws, vn[rows], preferred_element_type=F32) + bsp_ref[g]
            y_s[rows, cols] = (gate[rows] * s).astype(BF16)

    gates = {g: gate_of(g) for g in range(SPATIAL_LAG)}
    for g in range(A_GROUPS):
        if g + SPATIAL_LAG < A_GROUPS:
            gates[g + SPATIAL_LAG] = gate_of(g + SPATIAL_LAG)
        spatial(g, gates.pop(g))

    def kv_of(b, h):
        for r in range(b * PROJ_ROWS, (b + 1) * PROJ_ROWS, CHUNK):
            hr = h[r - b * PROJ_ROWS:r - b * PROJ_ROWS + CHUNK]
            kv_ref[r:r + CHUNK, :] = jnp.dot(hr.astype(BF16), w_kv_ref[...],
                                             preferred_element_type=F32).astype(BF16)

    h_prev = None
    for b in range(n_blocks):
        rows = slice(b * PROJ_ROWS, (b + 1) * PROJ_ROWS)
        sub = jnp.dot(y_s[rows, :], w_out_ref[...], preferred_element_type=F32)
        if h_prev is not None:
            kv_of(b - 1, h_prev)
        project_v(xn_ref, b)
        h_prev = _layer_norm(alpha * x_ref[rows, :] + sub, pg_ref[...], pb_ref[...])
        h_ref[rows, :] = h_prev
    kv_of(n_blocks - 1, h_prev)


def _sgu_layer(x2, w_in, ln_g, ln_b, w_spatial, bsp_b, w_out, w_kv, pg, pb, *, alpha, tm):
    t_tokens, d_model = x2.shape
    aw = w_out.shape[0]
    kvw = w_kv.shape[1]
    n_tiles = t_tokens // tm
    whole = pl.BlockSpec(memory_space=pltpu.VMEM)
    hbm = pl.BlockSpec(memory_space=pl.ANY)
    stage = lambda w: pltpu.VMEM((STAGE_SLOTS, _stage_rows(w), w.shape[1]), F32)
    return pl.pallas_call(
        functools.partial(_sgu_kernel, alpha=alpha),
        grid=(n_tiles,),
        in_specs=[pl.BlockSpec((tm, d_model), lambda i: (i, 0)),
                  pl.BlockSpec((tm, d_model), lambda i: (jnp.minimum(i + 1, n_tiles - 1), 0)),
                  hbm, whole, whole, whole, whole, hbm, hbm, whole, whole],
        out_specs=[pl.BlockSpec((tm, d_model), lambda i: (i, 0)),
                   pl.BlockSpec((tm, kvw), lambda i: (i, 0))],
        out_shape=[jax.ShapeDtypeStruct((t_tokens, d_model), F32),
                   jax.ShapeDtypeStruct((t_tokens, kvw), BF16)],
        scratch_shapes=[pltpu.VMEM((tm, aw), F32), pltpu.VMEM((tm, aw), BF16),
                        pltpu.VMEM(w_in.shape, BF16), pltpu.VMEM(w_out.shape, BF16), pltpu.VMEM(w_kv.shape, BF16),
                        stage(w_in), stage(w_out), stage(w_kv), pltpu.SemaphoreType.DMA((3, STAGE_SLOTS))],
        compiler_params=pltpu.CompilerParams(
            dimension_semantics=("arbitrary",),
            vmem_limit_bytes=V7X_VMEM_BYTES * 7 // 8),
        name="sgu_layer",
    )(x2, x2, w_in, ln_g, ln_b, w_spatial, bsp_b, w_out, w_kv, pg, pb)


def _swa_kernel(h_ref, hn_ref, kv_ref, kvp_ref, w_in_hbm, bias_ref, scale_ref, w_out_hbm,
                pg_ref, pb_ref, o_ref, q_s, g_s, y_s, w_in_ref, w_out_ref, st_in, st_out, sems,
                *, alpha, tiles_per_seq):
    tm, _ = h_ref.shape
    bw = y_s.shape[1]
    n_kv = bw // (HEAD_DIM * Q_PER_KV)
    pairs_per_kv = Q_PER_KV // 2
    i = pl.program_id(0)

    n_blocks = tm // PROJ_ROWS
    chunks_per_block = PROJ_ROWS // CHUNK

    def project_q(src_ref, b):
        rows = slice(b * PROJ_ROWS, (b + 1) * PROJ_ROWS)
        q_s[rows, :] = (jnp.dot(src_ref[rows, :].astype(BF16), w_in_ref[:, :bw], preferred_element_type=F32)
                        * (HEAD_DIM ** -0.5)).astype(BF16)

    def project_z(src_ref, b):
        rows = slice(b * PROJ_ROWS, (b + 1) * PROJ_ROWS)
        g_s[rows, :] = _silu(jnp.dot(src_ref[rows, :].astype(BF16), w_in_ref[:, bw:],
                                     preferred_element_type=F32))

    @pl.when(i == 0)
    def _():
        _cast_weight(w_in_hbm, w_in_ref, st_in, sems, 0)
        _cast_weight(w_out_hbm, w_out_ref, st_out, sems, 1)
        for b in range(n_blocks):
            project_q(h_ref, b)
            project_z(h_ref, b)

    as_bf16 = lambda a: a.astype(F32).astype(BF16)
    lo = as_bf16(lax.broadcasted_iota(jnp.int32, (2 * CHUNK, LANES), 1)) < HEAD_DIM
    key0 = as_bf16(lax.broadcasted_iota(jnp.int32, (2 * CHUNK, LANES), 0)) < 1
    zeros_k = jnp.zeros((2 * CHUNK, LANES), BF16)
    zeros_v = jnp.zeros((HEAD_DIM, 2 * CHUNK), BF16)
    zeros_d = jnp.zeros((DEN_ROWS, 2 * CHUNK), BF16)
    ones_d = jnp.ones((DEN_ROWS, 2 * CHUNK), BF16)
    first = (i % tiles_per_seq == 0).astype(jnp.int32)

    bands = {}

    def band_of(c):
        if c not in bands:
            rows = slice(c * CHUNK, (c + 1) * CHUNK)
            kv_prev = kvp_ref[...] if c == 0 else kv_ref[(c - 1) * CHUNK:c * CHUNK, :]
            band = jnp.concatenate([kv_prev, kv_ref[rows, :]], axis=0)
            kb, vb = band[:, :LANES], band[:, LANES:]
            kr = pltpu.roll(kb, HEAD_DIM, axis=1)
            v_t = jnp.where(key0, zeros_k, vb).astype(F32).T.astype(BF16)
            bands[c] = (kb, kr, v_t)
        return bands[c]

    def scores_of(c, kvh):
        rows = slice(c * CHUNK, (c + 1) * CHUNK)
        kb, kr, v_t = band_of(c)
        variant = first if c == 0 else 0
        k_src, k_alt = (kb, kr) if kvh == 0 else (kr, kb)
        k_cat = jnp.concatenate([jnp.where(lo, k_src, zeros_k), jnp.where(lo, zeros_k, k_alt)], axis=0)
        vk = v_t[kvh * HEAD_DIM:(kvh + 1) * HEAD_DIM]
        v_lhs = jnp.concatenate(
            [jnp.concatenate([vk, zeros_v], axis=1), jnp.concatenate([zeros_v, vk], axis=1),
             jnp.concatenate([ones_d, zeros_d], axis=1), jnp.concatenate([zeros_d, ones_d], axis=1)],
            axis=0)
        pair0 = kvh * pairs_per_kv
        q_stack = jnp.concatenate(
            [q_s[rows, (pair0 + p) * LANES:(pair0 + p + 1) * LANES] for p in range(pairs_per_kv)], axis=0)
        s_t = lax.dot_general(k_cat, q_stack, (((1,), (1,)), ((), ())),
                              preferred_element_type=F32)
        return rows, variant, kvh, s_t, v_lhs

    def attend(rows, variant, kvh, s_t, v_lhs):
        pair0 = kvh * pairs_per_kv
        e_cols = []
        for p in range(pairs_per_kv):
            qc = slice(p * CHUNK, (p + 1) * CHUNK)
            halves = []
            for e in range(2):
                kr_ = slice(e * 2 * CHUNK, (e + 1) * 2 * CHUNK)
                le = s_t[kr_, qc] * scale_ref[variant, kr_, :] + bias_ref[variant, kvh, kr_, qc]
                halves.append(jnp.exp2(le - jnp.max(le, axis=0, keepdims=True)).astype(BF16))
            e_cols.append(jnp.concatenate(halves, axis=0))
        pv_t = jnp.dot(v_lhs, jnp.concatenate(e_cols, axis=1), preferred_element_type=F32)
        n_num = 2 * HEAD_DIM
        sub = 8
        inv_e = 1.0 / pv_t[n_num:n_num + sub]
        inv_o = 1.0 / pv_t[n_num + DEN_ROWS:n_num + DEN_ROWS + sub]
        inv = jnp.concatenate([inv_e] * (HEAD_DIM // sub) + [inv_o] * (HEAD_DIM // sub), axis=0)
        o_t = pv_t[:n_num] * inv
        for p in range(pairs_per_kv):
            pc = slice((pair0 + p) * LANES, (pair0 + p + 1) * LANES)
            y_s[rows, pc] = (o_t[:, p * CHUNK:(p + 1) * CHUNK].T * g_s[rows, pc]).astype(BF16)

    block_units = [[(c, kvh) for c in range(b * chunks_per_block, (b + 1) * chunks_per_block)
                    for kvh in range(n_kv)] for b in range(n_blocks)]
    scored = [scores_of(*u) for u in block_units[0]]
    for b in range(n_blocks):
        rows = slice(b * PROJ_ROWS, (b + 1) * PROJ_ROWS)
        scored_next = []
        for j, unit in enumerate(scored):
            attend(*unit)
            if b + 1 < n_blocks:
                scored_next.append(scores_of(*block_units[b + 1][j]))
        scored = scored_next
        project_q(hn_ref, b)
        sub = jnp.dot(y_s[rows, :], w_out_ref[...], preferred_element_type=F32)
        project_z(hn_ref, b)
        o_ref[rows, :] = _layer_norm(alpha * h_ref[rows, :] + sub, pg_ref[...], pb_ref[...])


def _swa_layer(h1, kv, w_in, bias_tab, scale_tab, w_out, pg, pb, *, alpha, tm, seq):
    t_tokens, d_model = h1.shape
    bw = w_out.shape[0]
    kvw2 = kv.shape[1]
    cpt = tm // CHUNK
    n_tiles = t_tokens // tm
    whole = pl.BlockSpec(memory_space=pltpu.VMEM)
    hbm = pl.BlockSpec(memory_space=pl.ANY)
    stage = lambda w: pltpu.VMEM((STAGE_SLOTS, _stage_rows(w), w.shape[1]), F32)
    return pl.pallas_call(
        functools.partial(_swa_kernel, alpha=alpha, tiles_per_seq=seq // tm),
        grid=(n_tiles,),
        in_specs=[pl.BlockSpec((tm, d_model), lambda i: (i, 0)),
                  pl.BlockSpec((tm, d_model), lambda i: (jnp.minimum(i + 1, n_tiles - 1), 0)),
                  pl.BlockSpec((tm, kvw2), lambda i: (i, 0)),
                  pl.BlockSpec((CHUNK, kvw2), lambda i: (jnp.maximum(i * cpt - 1, 0), 0)),
                  hbm, whole, whole, hbm, whole, whole],
        out_specs=pl.BlockSpec((tm, d_model), lambda i: (i, 0)),
        out_shape=jax.ShapeDtypeStruct((t_tokens, d_model), F32),
        scratch_shapes=[pltpu.VMEM((tm, bw), BF16), pltpu.VMEM((tm, bw), F32), pltpu.VMEM((tm, bw), BF16),
                        pltpu.VMEM(w_in.shape, BF16), pltpu.VMEM(w_out.shape, BF16),
                        stage(w_in), stage(w_out), pltpu.SemaphoreType.DMA((2, STAGE_SLOTS))],
        compiler_params=pltpu.CompilerParams(
            dimension_semantics=("arbitrary",),
            vmem_limit_bytes=V7X_VMEM_BYTES * 7 // 8),
        name="swa_layer",
    )(h1, h1, kv, kv, w_in, bias_tab, scale_tab, w_out, pg, pb)


def kernel(x, w_in_a, sgu_ln_g, sgu_ln_b, w_spatial, b_spatial, w_out_a, w_kv, w_in_b, attn_sinks,
           rel_bias, w_out_b, post_ln_g, post_ln_b):
    bsz, seq, d_model = x.shape
    depth = post_ln_g.shape[0]
    assert w_in_a.shape[0] == 1 and w_in_b.shape[0] == 1 and depth == 2
    assert seq % CHUNK == 0 and w_spatial.shape[1:] == (A_GROUPS, CHUNK, CHUNK)
    assert w_kv.shape[1] == 2 * LANES and rel_bias.shape == (REL_BUCKETS, w_out_b.shape[1] // HEAD_DIM)
    alpha = (2.0 * depth) ** 0.25
    aw = w_out_a.shape[1]
    gd = aw // A_GROUPS
    tm = 4 * CHUNK

    x2 = x.reshape(bsz * seq, d_model)
    row = lambda a: a.reshape(1, -1)
    bsp_b = jnp.broadcast_to(b_spatial[0][:, :, None], (A_GROUPS, CHUNK, gd))
    h1, kv = _sgu_layer(x2, w_in_a[0], row(sgu_ln_g[0]), row(sgu_ln_b[0]), w_spatial[0], bsp_b,
                        w_out_a[0], w_kv, row(post_ln_g[0]), row(post_ln_b[0]),
                        alpha=alpha, tm=tm)
    bias_tab = _rel_bias_table(rel_bias, attn_sinks[0])
    out = _swa_layer(h1, kv, w_in_b[0], bias_tab, jnp.asarray(_band_scale_table()),
                     w_out_b[0], row(post_ln_g[1]), row(post_ln_b[1]),
                     alpha=alpha, tm=tm, seq=seq)
    return out.reshape(bsz, seq, d_model)
```

```python
import functools
import math

import numpy as np
import jax
import jax.numpy as jnp
from jax import lax
from jax.experimental import pallas as pl
from jax.experimental.pallas import tpu as pltpu

F32 = jnp.float32
BF16 = jnp.bfloat16

CHUNK = 128
A_GROUPS = 8
HEAD_DIM = 64
Q_PER_KV = 8
REL_BUCKETS = 32
REL_MAX_DIST = 128
LN_EPS = 1e-5
NEG_INF = -1e30
LOG2E = math.log2(math.e)

LANES = 128
PROJ_ROWS = 2 * CHUNK
SGU_ROWS = 4 * CHUNK
DEN_ROWS = 16
SPATIAL_LAG = 3
V7X_VMEM_BYTES = 64 * 1024 * 1024
STAGE_BYTES = 512 * 1024
STAGE_SLOTS = 4


def _bucket_lower_bounds():
    max_exact = REL_BUCKETS // 2
    d = np.arange(REL_MAX_DIST, dtype=np.int32)
    df = np.maximum(d, 1).astype(np.float32)
    large = max_exact + (np.log(df / np.float32(max_exact)) / np.float32(math.log(REL_MAX_DIST / max_exact))
                         * np.float32(REL_BUCKETS - max_exact)).astype(np.int32)
    large = np.minimum(large, REL_BUCKETS - 1)
    bucket = np.where(d < max_exact, d, large)
    assert np.all(np.diff(bucket) >= 0) and bucket[0] == 0
    return [int(np.argmax(bucket >= b)) if np.any(bucket >= b) else REL_MAX_DIST for b in range(REL_BUCKETS)]


def _band_scale_table():
    j = np.arange(2 * CHUNK)[:, None]
    t = np.arange(CHUNK)[None, :]
    d = t + CHUNK - j
    in_window = (d >= 0) & (d < CHUNK)
    first = in_window & (j >= CHUNK)
    tab = np.stack([np.tile(in_window, (2, 1)), np.tile(first, (2, 1))]).astype(np.float32)
    return tab * np.float32(LOG2E)


def _layer_norm(x, g, b):
    mu = jnp.mean(x, axis=-1, keepdims=True)
    d = x - mu
    var = jnp.mean(d * d, axis=-1, keepdims=True)
    return d * lax.rsqrt(var + LN_EPS) * g + b


def _silu(z):
    half = 0.5 * z
    return half + half * jnp.tanh(half)


def _stage_rows(w_hbm):
    k, n = w_hbm.shape
    rows = 8
    while 2 * rows * n * 4 <= STAGE_BYTES and k % (2 * rows) == 0:
        rows *= 2
    assert k % rows == 0
    return rows


def _cast_weight(w_hbm, w_s, stage, sems, k):
    rows = stage.shape[1]
    n = w_hbm.shape[0] // rows

    def copy(c):
        slot = c % STAGE_SLOTS
        return pltpu.make_async_copy(w_hbm.at[pl.ds(c * rows, rows), :], stage.at[slot], sems.at[k, slot])

    for c in range(min(STAGE_SLOTS - 1, n)):
        copy(c).start()
    for c in range(n):
        if c + STAGE_SLOTS - 1 < n:
            copy(c + STAGE_SLOTS - 1).start()
        copy(c).wait()
        w_s[c * rows:(c + 1) * rows, :] = stage[c % STAGE_SLOTS].astype(BF16)


def _rel_bias_kernel(rb_ref, sink_ref, out_ref, *, lower_bounds, pairs_per_kv):
    pair = pl.program_id(0) * pairs_per_kv + pl.program_id(1)
    j = lax.broadcasted_iota(jnp.int32, (2 * CHUNK, CHUNK), 0)
    t = lax.broadcasted_iota(jnp.int32, (2 * CHUNK, CHUNK), 1)
    dist = t + CHUNK - j
    in_window = (dist >= 0) & (dist < CHUNK)
    d = jnp.clip(dist, 0, REL_MAX_DIST - 1)
    for e in range(2):
        h = 2 * pair + e
        acc = jnp.full((2 * CHUNK, CHUNK), rb_ref[0, h], F32)
        for b in range(1, REL_BUCKETS):
            if lower_bounds[b] < REL_MAX_DIST:
                acc = jnp.where(d >= lower_bounds[b], rb_ref[b, h], acc)
        hidden = jnp.where(j == 0, sink_ref[h], NEG_INF)
        for variant in range(2):
            visible = in_window & (j >= variant * CHUNK)
            out_ref[variant, 0, e * 2 * CHUNK:(e + 1) * 2 * CHUNK, :] = jnp.where(visible, acc, hidden) * LOG2E


def _rel_bias_table(rel_bias, sinks):
    n_heads = rel_bias.shape[1]
    n_kv = n_heads // Q_PER_KV
    pairs_per_kv = Q_PER_KV // 2
    smem = pl.BlockSpec(memory_space=pltpu.SMEM)
    return pl.pallas_call(
        functools.partial(_rel_bias_kernel, lower_bounds=_bucket_lower_bounds(), pairs_per_kv=pairs_per_kv),
        grid=(n_kv, pairs_per_kv),
        in_specs=[smem, smem],
        out_specs=pl.BlockSpec((2, 1, 4 * CHUNK, CHUNK), lambda k, p: (0, k, 0, p)),
        out_shape=jax.ShapeDtypeStruct((2, n_kv, 4 * CHUNK, pairs_per_kv * CHUNK), F32),
        name="rel_bias_table",
    )(rel_bias, sinks)


def _sgu_kernel(x_ref, xn_ref, w_in_hbm, lng_ref, lnb_ref, ws_ref, bsp_ref, w_out_hbm, w_kv_hbm,
                pg_ref, pb_ref, h_ref, kv_ref, v_s, y_s, w_in_ref, w_out_ref, w_kv_ref,
                st_in, st_out, st_kv, sems, *, alpha):
    tm, _ = x_ref.shape
    aw = v_s.shape[1]
    gd = aw // A_GROUPS
    n_chunks = tm // CHUNK
    n_blocks = tm // SGU_ROWS

    def project_v(src_ref, b):
        rows = slice(b * SGU_ROWS, (b + 1) * SGU_ROWS)
        v_s[rows, :] = jnp.dot(src_ref[rows, :].astype(BF16), w_in_ref[:, aw:2 * aw],
                               preferred_element_type=F32)

    @pl.when(pl.program_id(0) == 0)
    def _():
        _cast_weight(w_in_hbm, w_in_ref, st_in, sems, 0)
        _cast_weight(w_out_hbm, w_out_ref, st_out, sems, 1)
        _cast_weight(w_kv_hbm, w_kv_ref, st_kv, sems, 2)
        for b in range(n_blocks):
            project_v(x_ref, b)

    xb = x_ref[...].astype(BF16)
    v = v_s[...]
    mu = jnp.mean(v, axis=-1, keepdims=True)
    rstd = lax.rsqrt(jnp.mean(jnp.square(v - mu), axis=-1, keepdims=True) + LN_EPS)

    row = lax.broadcasted_iota(jnp.int32, (CHUNK, CHUNK), 0)
    col = lax.broadcasted_iota(jnp.int32, (CHUNK, CHUNK), 1)
    tril = col <= row

    def gate_of(g):
        u = jnp.dot(xb, w_in_ref[:, g * gd:(g + 1) * gd], preferred_element_type=F32)
        z = jnp.dot(xb, w_in_ref[:, 2 * aw + g * gd:2 * aw + (g + 1) * gd], preferred_element_type=F32)
        return u * _silu(z)

    def spatial(g, gate):
        cols = slice(g * gd, (g + 1) * gd)
        ws = jnp.where(tril, ws_ref[g], 0.0).astype(BF16)
        vn = ((v_s[:, cols] - mu) * rstd * lng_ref[:, cols] + lnb_ref[:, cols]).astype(BF16)
        for c in range(n_chunks):
            rows = slice(c * CHUNK, (c + 1) * CHUNK)
            s = jnp.dot(ws, vn[rows], preferred_element_type=F32) + bsp_ref[g]
            y_s[rows, cols] = (gate[rows] * s).astype(BF16)

    gates = {g: gate_of(g) for g in range(SPATIAL_LAG)}
    for g in range(A_GROUPS):
        if g + SPATIAL_LAG < A_GROUPS:
            gates[g + SPATIAL_LAG] = gate_of(g + SPATIAL_LAG)
        spatial(g, gates.pop(g))

    def kv_of(b, h):
        for r in range(b * SGU_ROWS, (b + 1) * SGU_ROWS, CHUNK):
            hr = h[r - b * SGU_ROWS:r - b * SGU_ROWS + CHUNK]
            kv_ref[r:r + CHUNK, :] = jnp.dot(hr.astype(BF16), w_kv_ref[...],
                                             preferred_element_type=F32).astype(BF16)

    h_prev = None
    for b in range(n_blocks):
        rows = slice(b * SGU_ROWS, (b + 1) * SGU_ROWS)
        sub = jnp.dot(y_s[rows, :], w_out_ref[...], preferred_element_type=F32)
        if h_prev is not None:
            kv_of(b - 1, h_prev)
        project_v(xn_ref, b)
        h_prev = _layer_norm(alpha * x_ref[rows, :] + sub, pg_ref[...], pb_ref[...])
        h_ref[rows, :] = h_prev
    kv_of(n_blocks - 1, h_prev)


def _sgu_layer(x2, w_in, ln_g, ln_b, w_spatial, bsp_b, w_out, w_kv, pg, pb, *, alpha, tm):
    t_tokens, d_model = x2.shape
    aw = w_out.shape[0]
    kvw = w_kv.shape[1]
    n_tiles = t_tokens // tm
    whole = pl.BlockSpec(memory_space=pltpu.VMEM)
    hbm = pl.BlockSpec(memory_space=pl.ANY)
    stage = lambda w: pltpu.VMEM((STAGE_SLOTS, _stage_rows(w), w.shape[1]), F32)
    return pl.pallas_call(
        functools.partial(_sgu_kernel, alpha=alpha),
        grid=(n_tiles,),
        in_specs=[pl.BlockSpec((tm, d_model), lambda i: (i, 0)),
                  pl.BlockSpec((tm, d_model), lambda i: (jnp.minimum(i + 1, n_tiles - 1), 0)),
                  hbm, whole, whole, whole, whole, hbm, hbm, whole, whole],
        out_specs=[pl.BlockSpec((tm, d_model), lambda i: (i, 0)),
                   pl.BlockSpec((tm, kvw), lambda i: (i, 0))],
        out_shape=[jax.ShapeDtypeStruct((t_tokens, d_model), F32),
                   jax.ShapeDtypeStruct((t_tokens, kvw), BF16)],
        scratch_shapes=[pltpu.VMEM((tm, aw), F32), pltpu.VMEM((tm, aw), BF16),
                        pltpu.VMEM(w_in.shape, BF16), pltpu.VMEM(w_out.shape, BF16), pltpu.VMEM(w_kv.shape, BF16),
                        stage(w_in), stage(w_out), stage(w_kv), pltpu.SemaphoreType.DMA((3, STAGE_SLOTS))],
        compiler_params=pltpu.CompilerParams(
            dimension_semantics=("arbitrary",),
            vmem_limit_bytes=V7X_VMEM_BYTES * 7 // 8),
        name="sgu_layer",
    )(x2, x2, w_in, ln_g, ln_b, w_spatial, bsp_b, w_out, w_kv, pg, pb)


def _swa_kernel(h_ref, hn_ref, kv_ref, kvp_ref, w_in_hbm, bias_ref, scale_ref, w_out_hbm,
                pg_ref, pb_ref, o_ref, q_s, g_s, y_s, w_in_ref, w_out_ref, st_in, st_out, sems,
                *, alpha, tiles_per_seq):
    tm, _ = h_ref.shape
    bw = y_s.shape[1]
    n_kv = bw // (HEAD_DIM * Q_PER_KV)
    pairs_per_kv = Q_PER_KV // 2
    i = pl.program_id(0)

    n_blocks = tm // PROJ_ROWS
    chunks_per_block = PROJ_ROWS // CHUNK

    def project_q(src_ref, b):
        rows = slice(b * PROJ_ROWS, (b + 1) * PROJ_ROWS)
        q_s[rows, :] = (jnp.dot(src_ref[rows, :].astype(BF16), w_in_ref[:, :bw], preferred_element_type=F32)
                        * (HEAD_DIM ** -0.5)).astype(BF16)

    def project_z(src_ref, b):
        rows = slice(b * PROJ_ROWS, (b + 1) * PROJ_ROWS)
        g_s[rows, :] = _silu(jnp.dot(src_ref[rows, :].astype(BF16), w_in_ref[:, bw:],
                                     preferred_element_type=F32))

    @pl.when(i == 0)
    def _():
        _cast_weight(w_in_hbm, w_in_ref, st_in, sems, 0)
        _cast_weight(w_out_hbm, w_out_ref, st_out, sems, 1)
        for b in range(n_blocks):
            project_q(h_ref, b)
            project_z(h_ref, b)

    as_bf16 = lambda a: a.astype(F32).astype(BF16)
    lo = as_bf16(lax.broadcasted_iota(jnp.int32, (2 * CHUNK, LANES), 1)) < HEAD_DIM
    key0 = as_bf16(lax.broadcasted_iota(jnp.int32, (2 * CHUNK, LANES), 0)) < 1
    zeros_k = jnp.zeros((2 * CHUNK, LANES), BF16)
    zeros_v = jnp.zeros((HEAD_DIM, 2 * CHUNK), BF16)
    zeros_d = jnp.zeros((DEN_ROWS, 2 * CHUNK), BF16)
    ones_d = jnp.ones((DEN_ROWS, 2 * CHUNK), BF16)
    first = (i % tiles_per_seq == 0).astype(jnp.int32)

    bands = {}

    def band_of(c):
        if c not in bands:
            rows = slice(c * CHUNK, (c + 1) * CHUNK)
            kv_prev = kvp_ref[...] if c == 0 else kv_ref[(c - 1) * CHUNK:c * CHUNK, :]
            band = jnp.concatenate([kv_prev, kv_ref[rows, :]], axis=0)
            kb, vb = band[:, :LANES], band[:, LANES:]
            kr = pltpu.roll(kb, HEAD_DIM, axis=1)
            v_t = jnp.where(key0, zeros_k, vb).astype(F32).T.astype(BF16)
            bands[c] = (kb, kr, v_t)
        return bands[c]

    def scores_of(c, kvh):
        rows = slice(c * CHUNK, (c + 1) * CHUNK)
        kb, kr, v_t = band_of(c)
        variant = first if c == 0 else 0
        k_src, k_alt = (kb, kr) if kvh == 0 else (kr, kb)
        k_cat = jnp.concatenate([jnp.where(lo, k_src, zeros_k), jnp.where(lo, zeros_k, k_alt)], axis=0)
        vk = v_t[kvh * HEAD_DIM:(kvh + 1) * HEAD_DIM]
        v_lhs = jnp.concatenate(
            [jnp.concatenate([vk, zeros_v], axis=1), jnp.concatenate([zeros_v, vk], axis=1),
             jnp.concatenate([ones_d, zeros_d], axis=1), jnp.concatenate([zeros_d, ones_d], axis=1)],
            axis=0)
        pair0 = kvh * pairs_per_kv
        q_stack = jnp.concatenate(
            [q_s[rows, (pair0 + p) * LANES:(pair0 + p + 1) * LANES] for p in range(pairs_per_kv)], axis=0)
        s_t = lax.dot_general(k_cat, q_stack, (((1,), (1,)), ((), ())),
                              preferred_element_type=F32)
        return rows, variant, kvh, s_t, v_lhs

    def attend(rows, variant, kvh, s_t, v_lhs):
        pair0 = kvh * pairs_per_kv
        e_cols = []
        for p in range(pairs_per_kv):
            qc = slice(p * CHUNK, (p + 1) * CHUNK)
            halves = []
            for e in range(2):
                kr_ = slice(e * 2 * CHUNK, (e + 1) * 2 * CHUNK)
                le = s_t[kr_, qc] * scale_ref[variant, kr_, :] + bias_ref[variant, kvh, kr_, qc]
                halves.append(jnp.exp2(le - jnp.max(le, axis=0, keepdims=True)).astype(BF16))
            e_cols.append(jnp.concatenate(halves, axis=0))
        pv_t = jnp.dot(v_lhs, jnp.concatenate(e_cols, axis=1), preferred_element_type=F32)
        n_num = 2 * HEAD_DIM
        sub = 8
        inv_e = 1.0 / pv_t[n_num:n_num + sub]
        inv_o = 1.0 / pv_t[n_num + DEN_ROWS:n_num + DEN_ROWS + sub]
        inv = jnp.concatenate([inv_e] * (HEAD_DIM // sub) + [inv_o] * (HEAD_DIM // sub), axis=0)
        o_t = pv_t[:n_num] * inv
        for p in range(pairs_per_kv):
            pc = slice((pair0 + p) * LANES, (pair0 + p + 1) * LANES)
            y_s[rows, pc] = (o_t[:, p * CHUNK:(p + 1) * CHUNK].T * g_s[rows, pc]).astype(BF16)

    block_units = [[(c, kvh) for c in range(b * chunks_per_block, (b + 1) * chunks_per_block)
                    for kvh in range(n_kv)] for b in range(n_blocks)]
    scored = [scores_of(*u) for u in block_units[0]]
    for b in range(n_blocks):
        rows = slice(b * PROJ_ROWS, (b + 1) * PROJ_ROWS)
        scored_next = []
        for j, unit in enumerate(scored):
            attend(*unit)
            if b + 1 < n_blocks:
                scored_next.append(scores_of(*block_units[b + 1][j]))
        scored = scored_next
        project_q(hn_ref, b)
        sub = jnp.dot(y_s[rows, :], w_out_ref[...], preferred_element_type=F32)
        project_z(hn_ref, b)
        o_ref[rows, :] = _layer_norm(alpha * h_ref[rows, :] + sub, pg_ref[...], pb_ref[...])


def _swa_layer(h1, kv, w_in, bias_tab, scale_tab, w_out, pg, pb, *, alpha, tm, seq):
    t_tokens, d_model = h1.shape
    bw = w_out.shape[0]
    kvw2 = kv.shape[1]
    cpt = tm // CHUNK
    n_tiles = t_tokens // tm
    whole = pl.BlockSpec(memory_space=pltpu.VMEM)
    hbm = pl.BlockSpec(memory_space=pl.ANY)
    stage = lambda w: pltpu.VMEM((STAGE_SLOTS, _stage_rows(w), w.shape[1]), F32)
    return pl.pallas_call(
        functools.partial(_swa_kernel, alpha=alpha, tiles_per_seq=seq // tm),
        grid=(n_tiles,),
        in_specs=[pl.BlockSpec((tm, d_model), lambda i: (i, 0)),
                  pl.BlockSpec((tm, d_model), lambda i: (jnp.minimum(i + 1, n_tiles - 1), 0)),
                  pl.BlockSpec((tm, kvw2), lambda i: (i, 0)),
                  pl.BlockSpec((CHUNK, kvw2), lambda i: (jnp.maximum(i * cpt - 1, 0), 0)),
                  hbm, whole, whole, hbm, whole, whole],
        out_specs=pl.BlockSpec((tm, d_model), lambda i: (i, 0)),
        out_shape=jax.ShapeDtypeStruct((t_tokens, d_model), F32),
        scratch_shapes=[pltpu.VMEM((tm, bw), BF16), pltpu.VMEM((tm, bw), F32), pltpu.VMEM((tm, bw), BF16),
                        pltpu.VMEM(w_in.shape, BF16), pltpu.VMEM(w_out.shape, BF16),
                        stage(w_in), stage(w_out), pltpu.SemaphoreType.DMA((2, STAGE_SLOTS))],
        compiler_params=pltpu.CompilerParams(
            dimension_semantics=("arbitrary",),
            vmem_limit_bytes=V7X_VMEM_BYTES * 7 // 8),
        name="swa_layer",
    )(h1, h1, kv, kv, w_in, bias_tab, scale_tab, w_out, pg, pb)


def kernel(x, w_in_a, sgu_ln_g, sgu_ln_b, w_spatial, b_spatial, w_out_a, w_kv, w_in_b, attn_sinks,
           rel_bias, w_out_b, post_ln_g, post_ln_b):
    bsz, seq, d_model = x.shape
    depth = post_ln_g.shape[0]
    assert w_in_a.shape[0] == 1 and w_in_b.shape[0] == 1 and depth == 2
    assert seq % CHUNK == 0 and w_spatial.shape[1:] == (A_GROUPS, CHUNK, CHUNK)
    assert w_kv.shape[1] == 2 * LANES and rel_bias.shape == (REL_BUCKETS, w_out_b.shape[1] // HEAD_DIM)
    alpha = (2.0 * depth) ** 0.25
    aw = w_out_a.shape[1]
    gd = aw // A_GROUPS
    tm = 4 * CHUNK

    x2 = x.reshape(bsz * seq, d_model)
    row = lambda a: a.reshape(1, -1)
    bsp_b = jnp.broadcast_to(b_spatial[0][:, :, None], (A_GROUPS, CHUNK, gd))
    h1, kv = _sgu_layer(x2, w_in_a[0], row(sgu_ln_g[0]), row(sgu_ln_b[0]), w_spatial[0], bsp_b,
                        w_out_a[0], w_kv, row(post_ln_g[0]), row(post_ln_b[0]),
                        alpha=alpha, tm=tm)
    bias_tab = _rel_bias_table(rel_bias, attn_sinks[0])
    out = _swa_layer(h1, kv, w_in_b[0], bias_tab, jnp.asarray(_band_scale_table()),
                     w_out_b[0], row(post_ln_g[1]), row(post_ln_b[1]),
                     alpha=alpha, tm=tm, seq=seq)
    return out.reshape(bsz, seq, d_model)
```

```python
import functools
import math

import numpy as np
import jax
import jax.numpy as jnp
from jax import lax
from jax.experimental import pallas as pl
from jax.experimental.pallas import tpu as pltpu

F32 = jnp.float32
BF16 = jnp.bfloat16

CHUNK = 128
A_GROUPS = 8
HEAD_DIM = 64
Q_PER_KV = 8
REL_BUCKETS = 32
REL_MAX_DIST = 128
LN_EPS = 1e-5
NEG_INF = -1e30
LOG2E = math.log2(math.e)

LANES = 128
PROJ_ROWS = 2 * CHUNK
SGU_ROWS = 4 * CHUNK
DEN_ROWS = 16
SPATIAL_LAG = 1
V7X_VMEM_BYTES = 64 * 1024 * 1024
STAGE_BYTES = 512 * 1024
STAGE_SLOTS = 4


def _bucket_lower_bounds():
    max_exact = REL_BUCKETS // 2
    d = np.arange(REL_MAX_DIST, dtype=np.int32)
    df = np.maximum(d, 1).astype(np.float32)
    large = max_exact + (np.log(df / np.float32(max_exact)) / np.float32(math.log(REL_MAX_DIST / max_exact))
                         * np.float32(REL_BUCKETS - max_exact)).astype(np.int32)
    large = np.minimum(large, REL_BUCKETS - 1)
    bucket = np.where(d < max_exact, d, large)
    assert np.all(np.diff(bucket) >= 0) and bucket[0] == 0
    return [int(np.argmax(bucket >= b)) if np.any(bucket >= b) else REL_MAX_DIST for b in range(REL_BUCKETS)]


def _band_scale_table():
    j = np.arange(2 * CHUNK)[:, None]
    t = np.arange(CHUNK)[None, :]
    d = t + CHUNK - j
    in_window = (d >= 0) & (d < CHUNK)
    first = in_window & (j >= CHUNK)
    tab = np.stack([np.tile(in_window, (2, 1)), np.tile(first, (2, 1))]).astype(np.float32)
    return tab * np.float32(LOG2E)


def _layer_norm(x, g, b):
    mu = jnp.mean(x, axis=-1, keepdims=True)
    d = x - mu
    var = jnp.mean(d * d, axis=-1, keepdims=True)
    return d * lax.rsqrt(var + LN_EPS) * g + b


def _silu(z):
    half = 0.5 * z
    return half + half * jnp.tanh(half)


def _stage_rows(w_hbm):
    k, n = w_hbm.shape
    rows = 8
    while 2 * rows * n * 4 <= STAGE_BYTES and k % (2 * rows) == 0:
        rows *= 2
    assert k % rows == 0
    return rows


def _cast_weight(w_hbm, w_s, stage, sems, k):
    rows = stage.shape[1]
    n = w_hbm.shape[0] // rows

    def copy(c):
        slot = c % STAGE_SLOTS
        return pltpu.make_async_copy(w_hbm.at[pl.ds(c * rows, rows), :], stage.at[slot], sems.at[k, slot])

    for c in range(min(STAGE_SLOTS - 1, n)):
        copy(c).start()
    for c in range(n):
        if c + STAGE_SLOTS - 1 < n:
            copy(c + STAGE_SLOTS - 1).start()
        copy(c).wait()
        w_s[c * rows:(c + 1) * rows, :] = stage[c % STAGE_SLOTS].astype(BF16)


def _rel_bias_kernel(rb_ref, sink_ref, out_ref, *, lower_bounds, pairs_per_kv):
    pair = pl.program_id(0) * pairs_per_kv + pl.program_id(1)
    j = lax.broadcasted_iota(jnp.int32, (2 * CHUNK, CHUNK), 0)
    t = lax.broadcasted_iota(jnp.int32, (2 * CHUNK, CHUNK), 1)
    dist = t + CHUNK - j
    in_window = (dist >= 0) & (dist < CHUNK)
    d = jnp.clip(dist, 0, REL_MAX_DIST - 1)
    for e in range(2):
        h = 2 * pair + e
        acc = jnp.full((2 * CHUNK, CHUNK), rb_ref[0, h], F32)
        for b in range(1, REL_BUCKETS):
            if lower_bounds[b] < REL_MAX_DIST:
                acc = jnp.where(d >= lower_bounds[b], rb_ref[b, h], acc)
        hidden = jnp.where(j == 0, sink_ref[h], NEG_INF)
        for variant in range(2):
            visible = in_window & (j >= variant * CHUNK)
            out_ref[variant, 0, e * 2 * CHUNK:(e + 1) * 2 * CHUNK, :] = jnp.where(visible, acc, hidden) * LOG2E


def _rel_bias_table(rel_bias, sinks):
    n_heads = rel_bias.shape[1]
    n_kv = n_heads // Q_PER_KV
    pairs_per_kv = Q_PER_KV // 2
    smem = pl.BlockSpec(memory_space=pltpu.SMEM)
    return pl.pallas_call(
        functools.partial(_rel_bias_kernel, lower_bounds=_bucket_lower_bounds(), pairs_per_kv=pairs_per_kv),
        grid=(n_kv, pairs_per_kv),
        in_specs=[smem, smem],
        out_specs=pl.BlockSpec((2, 1, 4 * CHUNK, CHUNK), lambda k, p: (0, k, 0, p)),
        out_shape=jax.ShapeDtypeStruct((2, n_kv, 4 * CHUNK, pairs_per_kv * CHUNK), F32),
        name="rel_bias_table",
    )(rel_bias, sinks)


def _sgu_kernel(x_ref, xn_ref, w_in_hbm, lng_ref, lnb_ref, ws_ref, bsp_ref, w_out_hbm, w_kv_hbm,
                pg_ref, pb_ref, h_ref, kv_ref, v_s, mu_s, rstd_s, y_s, w_in_ref, w_out_ref, w_kv_ref,
                st_in, st_out, st_kv, sems, *, alpha):
    tm, _ = x_ref.shape
    aw = v_s.shape[1]
    gd = aw // A_GROUPS
    n_chunks = tm // CHUNK
    n_blocks = tm // SGU_ROWS

    def project_v(src_ref, b):
        rows = slice(b * SGU_ROWS, (b + 1) * SGU_ROWS)
        v = jnp.dot(src_ref[rows, :].astype(BF16), w_in_ref[:, aw:2 * aw], preferred_element_type=F32)
        v_s[rows, :] = v
        mu = jnp.mean(v, axis=-1, keepdims=True)
        var = jnp.maximum(jnp.mean(v * v, axis=-1, keepdims=True) - mu * mu, 0.0)
        mu_s[rows, :] = jnp.broadcast_to(mu, (SGU_ROWS, LANES))
        rstd_s[rows, :] = jnp.broadcast_to(lax.rsqrt(var + LN_EPS), (SGU_ROWS, LANES))

    @pl.when(pl.program_id(0) == 0)
    def _():
        _cast_weight(w_in_hbm, w_in_ref, st_in, sems, 0)
        _cast_weight(w_out_hbm, w_out_ref, st_out, sems, 1)
        _cast_weight(w_kv_hbm, w_kv_ref, st_kv, sems, 2)
        for b in range(n_blocks):
            project_v(x_ref, b)

    xb = x_ref[...].astype(BF16)
    lane_tiles = gd // LANES
    mu = jnp.concatenate([mu_s[...]] * lane_tiles, axis=1)
    rstd = jnp.concatenate([rstd_s[...]] * lane_tiles, axis=1)

    row = lax.broadcasted_iota(jnp.int32, (CHUNK, CHUNK), 0)
    col = lax.broadcasted_iota(jnp.int32, (CHUNK, CHUNK), 1)
    tril = col <= row

    def gate_of(g):
        u = jnp.dot(xb, w_in_ref[:, g * gd:(g + 1) * gd], preferred_element_type=F32)
        z = jnp.dot(xb, w_in_ref[:, 2 * aw + g * gd:2 * aw + (g + 1) * gd], preferred_element_type=F32)
        return u * _silu(z)

    def spatial(g, gate):
        cols = slice(g * gd, (g + 1) * gd)
        ws = jnp.where(tril, ws_ref[g], 0.0).astype(BF16)
        vn = ((v_s[:, cols] - mu) * rstd * lng_ref[:, cols] + lnb_ref[:, cols]).astype(BF16)
        for c in range(n_chunks):
            rows = slice(c * CHUNK, (c + 1) * CHUNK)
            s = jnp.dot(ws, vn[rows], preferred_element_type=F32) + bsp_ref[g]
            y_s[rows, cols] = (gate[rows] * s).astype(BF16)

    gates = {g: gate_of(g) for g in range(SPATIAL_LAG)}
    for g in range(A_GROUPS):
        if g + SPATIAL_LAG < A_GROUPS:
            gates[g + SPATIAL_LAG] = gate_of(g + SPATIAL_LAG)
        spatial(g, gates.pop(g))

    def kv_of(b, h):
        for r in range(b * SGU_ROWS, (b + 1) * SGU_ROWS, CHUNK):
            hr = h[r - b * SGU_ROWS:r - b * SGU_ROWS + CHUNK]
            kv_ref[r:r + CHUNK, :] = jnp.dot(hr.astype(BF16), w_kv_ref[...],
                                             preferred_element_type=F32).astype(BF16)

    h_prev = None
    for b in range(n_blocks):
        rows = slice(b * SGU_ROWS, (b + 1) * SGU_ROWS)
        sub = jnp.dot(y_s[rows, :], w_out_ref[...], preferred_element_type=F32)
        if h_prev is not None:
            kv_of(b - 1, h_prev)
        project_v(xn_ref, b)
        h_prev = _layer_norm(alpha * x_ref[rows, :] + sub, pg_ref[...], pb_ref[...])
        h_ref[rows, :] = h_prev
    kv_of(n_blocks - 1, h_prev)


def _sgu_layer(x2, w_in, ln_g, ln_b, w_spatial, bsp_b, w_out, w_kv, pg, pb, *, alpha, tm):
    t_tokens, d_model = x2.shape
    aw = w_out.shape[0]
    kvw = w_kv.shape[1]
    n_tiles = t_tokens // tm
    whole = pl.BlockSpec(memory_space=pltpu.VMEM)
    hbm = pl.BlockSpec(memory_space=pl.ANY)
    stage = lambda w: pltpu.VMEM((STAGE_SLOTS, _stage_rows(w), w.shape[1]), F32)
    return pl.pallas_call(
        functools.partial(_sgu_kernel, alpha=alpha),
        grid=(n_tiles,),
        in_specs=[pl.BlockSpec((tm, d_model), lambda i: (i, 0)),
                  pl.BlockSpec((tm, d_model), lambda i: (jnp.minimum(i + 1, n_tiles - 1), 0)),
                  hbm, whole, whole, whole, whole, hbm, hbm, whole, whole],
        out_specs=[pl.BlockSpec((tm, d_model), lambda i: (i, 0)),
                   pl.BlockSpec((tm, kvw), lambda i: (i, 0))],
        out_shape=[jax.ShapeDtypeStruct((t_tokens, d_model), F32),
                   jax.ShapeDtypeStruct((t_tokens, kvw), BF16)],
        scratch_shapes=[pltpu.VMEM((tm, aw), F32), pltpu.VMEM((tm, LANES), F32), pltpu.VMEM((tm, LANES), F32),
                        pltpu.VMEM((tm, aw), BF16),
                        pltpu.VMEM(w_in.shape, BF16), pltpu.VMEM(w_out.shape, BF16), pltpu.VMEM(w_kv.shape, BF16),
                        stage(w_in), stage(w_out), stage(w_kv), pltpu.SemaphoreType.DMA((3, STAGE_SLOTS))],
        compiler_params=pltpu.CompilerParams(
            dimension_semantics=("arbitrary",),
            vmem_limit_bytes=V7X_VMEM_BYTES * 7 // 8),
        name="sgu_layer",
    )(x2, x2, w_in, ln_g, ln_b, w_spatial, bsp_b, w_out, w_kv, pg, pb)


def _swa_kernel(h_ref, hn_ref, kv_ref, kvp_ref, w_in_hbm, bias_ref, scale_ref, w_out_hbm,
                pg_ref, pb_ref, o_ref, q_s, g_s, y_s, w_in_ref, w_out_ref, st_in, st_out, sems,
                *, alpha, tiles_per_seq):
    tm, _ = h_ref.shape
    bw = y_s.shape[1]
    n_kv = bw // (HEAD_DIM * Q_PER_KV)
    pairs_per_kv = Q_PER_KV // 2
    i = pl.program_id(0)

    n_blocks = tm // PROJ_ROWS
    chunks_per_block = PROJ_ROWS // CHUNK

    def project_q(src_ref, b):
        rows = slice(b * PROJ_ROWS, (b + 1) * PROJ_ROWS)
        q_s[rows, :] = (jnp.dot(src_ref[rows, :].astype(BF16), w_in_ref[:, :bw], preferred_element_type=F32)
                        * (HEAD_DIM ** -0.5)).astype(BF16)

    def project_z(src_ref, b):
        rows = slice(b * PROJ_ROWS, (b + 1) * PROJ_ROWS)
        g_s[rows, :] = _silu(jnp.dot(src_ref[rows, :].astype(BF16), w_in_ref[:, bw:],
                                     preferred_element_type=F32))

    @pl.when(i == 0)
    def _():
        _cast_weight(w_in_hbm, w_in_ref, st_in, sems, 0)
        _cast_weight(w_out_hbm, w_out_ref, st_out, sems, 1)
        for b in range(n_blocks):
            project_q(h_ref, b)
            project_z(h_ref, b)

    as_bf16 = lambda a: a.astype(F32).astype(BF16)
    lo = as_bf16(lax.broadcasted_iota(jnp.int32, (2 * CHUNK, LANES), 1)) < HEAD_DIM
    key0 = as_bf16(lax.broadcasted_iota(jnp.int32, (2 * CHUNK, LANES), 0)) < 1
    zeros_k = jnp.zeros((2 * CHUNK, LANES), BF16)
    zeros_v = jnp.zeros((HEAD_DIM, 2 * CHUNK), BF16)
    zeros_d = jnp.zeros((DEN_ROWS, 2 * CHUNK), BF16)
    ones_d = jnp.ones((DEN_ROWS, 2 * CHUNK), BF16)
    first = (i % tiles_per_seq == 0).astype(jnp.int32)

    bands = {}

    def band_of(c):
        if c not in bands:
            rows = slice(c * CHUNK, (c + 1) * CHUNK)
            kv_prev = kvp_ref[...] if c == 0 else kv_ref[(c - 1) * CHUNK:c * CHUNK, :]
            band = jnp.concatenate([kv_prev, kv_ref[rows, :]], axis=0)
            kb, vb = band[:, :LANES], band[:, LANES:]
            kr = pltpu.roll(kb, HEAD_DIM, axis=1)
            v_t = jnp.where(key0, zeros_k, vb).astype(F32).T.astype(BF16)
            bands[c] = (kb, kr, v_t)
        return bands[c]

    def scores_of(c, kvh):
        rows = slice(c * CHUNK, (c + 1) * CHUNK)
        kb, kr, v_t = band_of(c)
        variant = first if c == 0 else 0
        k_src, k_alt = (kb, kr) if kvh == 0 else (kr, kb)
        k_cat = jnp.concatenate([jnp.where(lo, k_src, zeros_k), jnp.where(lo, zeros_k, k_alt)], axis=0)
        vk = v_t[kvh * HEAD_DIM:(kvh + 1) * HEAD_DIM]
        v_lhs = jnp.concatenate(
            [jnp.concatenate([vk, zeros_v], axis=1), jnp.concatenate([zeros_v, vk], axis=1),
             jnp.concatenate([ones_d, zeros_d], axis=1), jnp.concatenate([zeros_d, ones_d], axis=1)],
            axis=0)
        pair0 = kvh * pairs_per_kv
        q_stack = jnp.concatenate(
            [q_s[rows, (pair0 + p) * LANES:(pair0 + p + 1) * LANES] for p in range(pairs_per_kv)], axis=0)
        s_t = lax.dot_general(k_cat, q_stack, (((1,), (1,)), ((), ())),
                              preferred_element_type=F32)
        return rows, variant, kvh, s_t, v_lhs

    def attend(rows, variant, kvh, s_t, v_lhs):
        pair0 = kvh * pairs_per_kv
        e_cols = []
        for p in range(pairs_per_kv):
            qc = slice(p * CHUNK, (p + 1) * CHUNK)
            halves = []
            for e in range(2):
                kr_ = slice(e * 2 * CHUNK, (e + 1) * 2 * CHUNK)
                le = s_t[kr_, qc] * scale_ref[variant, kr_, :] + bias_ref[variant, kvh, kr_, qc]
                halves.append(jnp.exp2(le - jnp.max(le, axis=0, keepdims=True)).astype(BF16))
            e_cols.append(jnp.concatenate(halves, axis=0))
        pv_t = jnp.dot(v_lhs, jnp.concatenate(e_cols, axis=1), preferred_element_type=F32)
        n_num = 2 * HEAD_DIM
        sub = 8
        inv_e = 1.0 / pv_t[n_num:n_num + sub]
        inv_o = 1.0 / pv_t[n_num + DEN_ROWS:n_num + DEN_ROWS + sub]
        inv = jnp.concatenate([inv_e] * (HEAD_DIM // sub) + [inv_o] * (HEAD_DIM // sub), axis=0)
        o_t = pv_t[:n_num] * inv
        for p in range(pairs_per_kv):
            pc = slice((pair0 + p) * LANES, (pair0 + p + 1) * LANES)
            y_s[rows, pc] = (o_t[:, p * CHUNK:(p + 1) * CHUNK].T * g_s[rows, pc]).astype(BF16)

    block_units = [[(c, kvh) for c in range(b * chunks_per_block, (b + 1) * chunks_per_block)
                    for kvh in range(n_kv)] for b in range(n_blocks)]
    scored = [scores_of(*u) for u in block_units[0]]
    for b in range(n_blocks):
        rows = slice(b * PROJ_ROWS, (b + 1) * PROJ_ROWS)
        scored_next = []
        for j, unit in enumerate(scored):
            attend(*unit)
            if b + 1 < n_blocks:
                scored_next.append(scores_of(*block_units[b + 1][j]))
        scored = scored_next
        project_q(hn_ref, b)
        sub = jnp.dot(y_s[rows, :], w_out_ref[...], preferred_element_type=F32)
        project_z(hn_ref, b)
        o_ref[rows, :] = _layer_norm(alpha * h_ref[rows, :] + sub, pg_ref[...], pb_ref[...])


def _swa_layer(h1, kv, w_in, bias_tab, scale_tab, w_out, pg, pb, *, alpha, tm, seq):
    t_tokens, d_model = h1.shape
    bw = w_out.shape[0]
    kvw2 = kv.shape[1]
    cpt = tm // CHUNK
    n_tiles = t_tokens // tm
    whole = pl.BlockSpec(memory_space=pltpu.VMEM)
    hbm = pl.BlockSpec(memory_space=pl.ANY)
    stage = lambda w: pltpu.VMEM((STAGE_SLOTS, _stage_rows(w), w.shape[1]), F32)
    return pl.pallas_call(
        functools.partial(_swa_kernel, alpha=alpha, tiles_per_seq=seq // tm),
        grid=(n_tiles,),
        in_specs=[pl.BlockSpec((tm, d_model), lambda i: (i, 0)),
                  pl.BlockSpec((tm, d_model), lambda i: (jnp.minimum(i + 1, n_tiles - 1), 0)),
                  pl.BlockSpec((tm, kvw2), lambda i: (i, 0)),
                  pl.BlockSpec((CHUNK, kvw2), lambda i: (jnp.maximum(i * cpt - 1, 0), 0)),
                  hbm, whole, whole, hbm, whole, whole],
        out_specs=pl.BlockSpec((tm, d_model), lambda i: (i, 0)),
        out_shape=jax.ShapeDtypeStruct((t_tokens, d_model), F32),
        scratch_shapes=[pltpu.VMEM((tm, bw), BF16), pltpu.VMEM((tm, bw), F32), pltpu.VMEM((tm, bw), BF16),
                        pltpu.VMEM(w_in.shape, BF16), pltpu.VMEM(w_out.shape, BF16),
                        stage(w_in), stage(w_out), pltpu.SemaphoreType.DMA((2, STAGE_SLOTS))],
        compiler_params=pltpu.CompilerParams(
            dimension_semantics=("arbitrary",),
            vmem_limit_bytes=V7X_VMEM_BYTES * 7 // 8),
        name="swa_layer",
    )(h1, h1, kv, kv, w_in, bias_tab, scale_tab, w_out, pg, pb)


def kernel(x, w_in_a, sgu_ln_g, sgu_ln_b, w_spatial, b_spatial, w_out_a, w_kv, w_in_b, attn_sinks,
           rel_bias, w_out_b, post_ln_g, post_ln_b):
    bsz, seq, d_model = x.shape
    depth = post_ln_g.shape[0]
    assert w_in_a.shape[0] == 1 and w_in_b.shape[0] == 1 and depth == 2
    assert seq % CHUNK == 0 and w_spatial.shape[1:] == (A_GROUPS, CHUNK, CHUNK)
    assert w_kv.shape[1] == 2 * LANES and rel_bias.shape == (REL_BUCKETS, w_out_b.shape[1] // HEAD_DIM)
    alpha = (2.0 * depth) ** 0.25
    aw = w_out_a.shape[1]
    gd = aw // A_GROUPS
    tm = 4 * CHUNK

    x2 = x.reshape(bsz * seq, d_model)
    row = lambda a: a.reshape(1, -1)
    bsp_b = jnp.broadcast_to(b_spatial[0][:, :, None], (A_GROUPS, CHUNK, gd))
    h1, kv = _sgu_layer(x2, w_in_a[0], row(sgu_ln_g[0]), row(sgu_ln_b[0]), w_spatial[0], bsp_b,
                        w_out_a[0], w_kv, row(post_ln_g[0]), row(post_ln_b[0]),
                        alpha=alpha, tm=tm)
    bias_tab = _rel_bias_table(rel_bias, attn_sinks[0])
    out = _swa_layer(h1, kv, w_in_b[0], bias_tab, jnp.asarray(_band_scale_table()),
                     w_out_b[0], row(post_ln_g[1]), row(post_ln_b[1]),
                     alpha=alpha, tm=tm, seq=seq)
    return out.reshape(bsz, seq, d_model)
```

```python
import functools
import math

import numpy as np
import jax
import jax.numpy as jnp
from jax import lax
from jax.experimental import pallas as pl
from jax.experimental.pallas import tpu as pltpu

F32 = jnp.float32
BF16 = jnp.bfloat16

CHUNK = 128
A_GROUPS = 8
HEAD_DIM = 64
Q_PER_KV = 8
REL_BUCKETS = 32
REL_MAX_DIST = 128
LN_EPS = 1e-5
NEG_INF = -1e30
LOG2E = math.log2(math.e)

LANES = 128
PROJ_ROWS = 2 * CHUNK
SGU_ROWS = 4 * CHUNK
DEN_ROWS = 16
SPATIAL_LAG = 1
V7X_VMEM_BYTES = 64 * 1024 * 1024
STAGE_BYTES = 512 * 1024
STAGE_SLOTS = 4


def _bucket_lower_bounds():
    max_exact = REL_BUCKETS // 2
    d = np.arange(REL_MAX_DIST, dtype=np.int32)
    df = np.maximum(d, 1).astype(np.float32)
    large = max_exact + (np.log(df / np.float32(max_exact)) / np.float32(math.log(REL_MAX_DIST / max_exact))
                         * np.float32(REL_BUCKETS - max_exact)).astype(np.int32)
    large = np.minimum(large, REL_BUCKETS - 1)
    bucket = np.where(d < max_exact, d, large)
    assert np.all(np.diff(bucket) >= 0) and bucket[0] == 0
    return [int(np.argmax(bucket >= b)) if np.any(bucket >= b) else REL_MAX_DIST for b in range(REL_BUCKETS)]


def _band_scale_table():
    j = np.arange(2 * CHUNK)[:, None]
    t = np.arange(CHUNK)[None, :]
    d = t + CHUNK - j
    in_window = (d >= 0) & (d < CHUNK)
    first = in_window & (j >= CHUNK)
    tab = np.stack([np.tile(in_window, (2, 1)), np.tile(first, (2, 1))]).astype(np.float32)
    return tab * np.float32(LOG2E)


def _layer_norm(x, g, b):
    mu = jnp.mean(x, axis=-1, keepdims=True)
    d = x - mu
    var = jnp.mean(d * d, axis=-1, keepdims=True)
    return d * lax.rsqrt(var + LN_EPS) * g + b


def _silu(z):
    half = 0.5 * z
    return half + half * jnp.tanh(half)


def _stage_rows(w_hbm):
    k, n = w_hbm.shape
    rows = 8
    while 2 * rows * n * 4 <= STAGE_BYTES and k % (2 * rows) == 0:
        rows *= 2
    assert k % rows == 0
    return rows


def _cast_weight(w_hbm, w_s, stage, sems, k):
    rows = stage.shape[1]
    n = w_hbm.shape[0] // rows

    def copy(c):
        slot = c % STAGE_SLOTS
        return pltpu.make_async_copy(w_hbm.at[pl.ds(c * rows, rows), :], stage.at[slot], sems.at[k, slot])

    for c in range(min(STAGE_SLOTS - 1, n)):
        copy(c).start()
    for c in range(n):
        if c + STAGE_SLOTS - 1 < n:
            copy(c + STAGE_SLOTS - 1).start()
        copy(c).wait()
        w_s[c * rows:(c + 1) * rows, :] = stage[c % STAGE_SLOTS].astype(BF16)


def _rel_bias_kernel(rb_ref, sink_ref, out_ref, *, lower_bounds, pairs_per_kv):
    pair = pl.program_id(0) * pairs_per_kv + pl.program_id(1)
    j = lax.broadcasted_iota(jnp.int32, (2 * CHUNK, CHUNK), 0)
    t = lax.broadcasted_iota(jnp.int32, (2 * CHUNK, CHUNK), 1)
    dist = t + CHUNK - j
    in_window = (dist >= 0) & (dist < CHUNK)
    d = jnp.clip(dist, 0, REL_MAX_DIST - 1)
    for e in range(2):
        h = 2 * pair + e
        acc = jnp.full((2 * CHUNK, CHUNK), rb_ref[0, h], F32)
        for b in range(1, REL_BUCKETS):
            if lower_bounds[b] < REL_MAX_DIST:
                acc = jnp.where(d >= lower_bounds[b], rb_ref[b, h], acc)
        hidden = jnp.where(j == 0, sink_ref[h], NEG_INF)
        for variant in range(2):
            visible = in_window & (j >= variant * CHUNK)
            out_ref[variant, 0, e * 2 * CHUNK:(e + 1) * 2 * CHUNK, :] = jnp.where(visible, acc, hidden) * LOG2E


def _rel_bias_table(rel_bias, sinks):
    n_heads = rel_bias.shape[1]
    n_kv = n_heads // Q_PER_KV
    pairs_per_kv = Q_PER_KV // 2
    smem = pl.BlockSpec(memory_space=pltpu.SMEM)
    return pl.pallas_call(
        functools.partial(_rel_bias_kernel, lower_bounds=_bucket_lower_bounds(), pairs_per_kv=pairs_per_kv),
        grid=(n_kv, pairs_per_kv),
        in_specs=[smem, smem],
        out_specs=pl.BlockSpec((2, 1, 4 * CHUNK, CHUNK), lambda k, p: (0, k, 0, p)),
        out_shape=jax.ShapeDtypeStruct((2, n_kv, 4 * CHUNK, pairs_per_kv * CHUNK), F32),
        name="rel_bias_table",
    )(rel_bias, sinks)


def _sgu_kernel(x_ref, xn_ref, w_in_hbm, lng_ref, lnb_ref, ws_ref, bsp_ref, w_out_hbm, w_kv_hbm,
                pg_ref, pb_ref, h_ref, kv_ref, v_s, mu_s, rstd_s, y_s, w_in_ref, w_out_ref, w_kv_ref,
                st_in, st_out, st_kv, sems, *, alpha):
    tm, _ = x_ref.shape
    aw = v_s.shape[1]
    gd = aw // A_GROUPS
    n_chunks = tm // CHUNK
    n_blocks = tm // SGU_ROWS

    def project_v(src_ref, b):
        rows = slice(b * SGU_ROWS, (b + 1) * SGU_ROWS)
        v = jnp.dot(src_ref[rows, :].astype(BF16), w_in_ref[:, aw:2 * aw], preferred_element_type=F32)
        v_s[rows, :] = v
        mu = jnp.mean(v, axis=-1, keepdims=True)
        var = jnp.maximum(jnp.mean(v * v, axis=-1, keepdims=True) - mu * mu, 0.0)
        mu_s[rows, :] = jnp.broadcast_to(mu, (SGU_ROWS, LANES))
        rstd_s[rows, :] = jnp.broadcast_to(lax.rsqrt(var + LN_EPS), (SGU_ROWS, LANES))

    @pl.when(pl.program_id(0) == 0)
    def _():
        _cast_weight(w_in_hbm, w_in_ref, st_in, sems, 0)
        _cast_weight(w_out_hbm, w_out_ref, st_out, sems, 1)
        _cast_weight(w_kv_hbm, w_kv_ref, st_kv, sems, 2)
        for b in range(n_blocks):
            project_v(x_ref, b)

    xb = x_ref[...].astype(BF16)
    lane_tiles = gd // LANES
    mu = jnp.concatenate([mu_s[...]] * lane_tiles, axis=1)
    rstd = jnp.concatenate([rstd_s[...]] * lane_tiles, axis=1)

    row = lax.broadcasted_iota(jnp.int32, (CHUNK, CHUNK), 0)
    col = lax.broadcasted_iota(jnp.int32, (CHUNK, CHUNK), 1)
    tril = col <= row

    def gate_of(g):
        u = jnp.dot(xb, w_in_ref[:, g * gd:(g + 1) * gd], preferred_element_type=F32)
        z = jnp.dot(xb, w_in_ref[:, 2 * aw + g * gd:2 * aw + (g + 1) * gd], preferred_element_type=F32)
        return u * _silu(z)

    def spatial(g, gate):
        cols = slice(g * gd, (g + 1) * gd)
        ws = jnp.where(tril, ws_ref[g], 0.0).astype(BF16)
        vn = ((v_s[:, cols] - mu) * rstd * lng_ref[:, cols] + lnb_ref[:, cols]).astype(BF16)
        for c in range(n_chunks):
            rows = slice(c * CHUNK, (c + 1) * CHUNK)
            s = jnp.dot(ws, vn[rows], preferred_element_type=F32) + bsp_ref[g]
            y_s[rows, cols] = (gate[rows] * s).astype(BF16)

    gates = {g: gate_of(g) for g in range(SPATIAL_LAG)}
    for g in range(A_GROUPS):
        if g + SPATIAL_LAG < A_GROUPS:
            gates[g + SPATIAL_LAG] = gate_of(g + SPATIAL_LAG)
        spatial(g, gates.pop(g))

    def kv_of(b, h):
        for r in range(b * SGU_ROWS, (b + 1) * SGU_ROWS, CHUNK):
            hr = h[r - b * SGU_ROWS:r - b * SGU_ROWS + CHUNK]
            kv_ref[r:r + CHUNK, :] = jnp.dot(hr.astype(BF16), w_kv_ref[...],
                                             preferred_element_type=F32).astype(BF16)

    h_prev = None
    for b in range(n_blocks):
        rows = slice(b * SGU_ROWS, (b + 1) * SGU_ROWS)
        sub = jnp.dot(y_s[rows, :], w_out_ref[...], preferred_element_type=F32)
        if h_prev is not None:
            kv_of(b - 1, h_prev)
        project_v(xn_ref, b)
        h_prev = _layer_norm(alpha * x_ref[rows, :] + sub, pg_ref[...], pb_ref[...])
        h_ref[rows, :] = h_prev
    kv_of(n_blocks - 1, h_prev)


def _sgu_layer(x2, w_in, ln_g, ln_b, w_spatial, bsp_b, w_out, w_kv, pg, pb, *, alpha, tm):
    t_tokens, d_model = x2.shape
    aw = w_out.shape[0]
    kvw = w_kv.shape[1]
    n_tiles = t_tokens // tm
    whole = pl.BlockSpec(memory_space=pltpu.VMEM)
    hbm = pl.BlockSpec(memory_space=pl.ANY)
    stage = lambda w: pltpu.VMEM((STAGE_SLOTS, _stage_rows(w), w.shape[1]), F32)
    return pl.pallas_call(
        functools.partial(_sgu_kernel, alpha=alpha),
        grid=(n_tiles,),
        in_specs=[pl.BlockSpec((tm, d_model), lambda i: (i, 0)),
                  pl.BlockSpec((tm, d_model), lambda i: (jnp.minimum(i + 1, n_tiles - 1), 0)),
                  hbm, whole, whole, whole, whole, hbm, hbm, whole, whole],
        out_specs=[pl.BlockSpec((tm, d_model), lambda i: (i, 0)),
                   pl.BlockSpec((tm, kvw), lambda i: (i, 0))],
        out_shape=[jax.ShapeDtypeStruct((t_tokens, d_model), F32),
                   jax.ShapeDtypeStruct((t_tokens, kvw), BF16)],
        scratch_shapes=[pltpu.VMEM((tm, aw), F32), pltpu.VMEM((tm, LANES), F32), pltpu.VMEM((tm, LANES), F32),
                        pltpu.VMEM((tm, aw), BF16),
                        pltpu.VMEM(w_in.shape, BF16), pltpu.VMEM(w_out.shape, BF16), pltpu.VMEM(w_kv.shape, BF16),
                        stage(w_in), stage(w_out), stage(w_kv), pltpu.SemaphoreType.DMA((3, STAGE_SLOTS))],
        compiler_params=pltpu.CompilerParams(
            dimension_semantics=("arbitrary",),
            vmem_limit_bytes=V7X_VMEM_BYTES * 7 // 8),
        name="sgu_layer",
    )(x2, x2, w_in, ln_g, ln_b, w_spatial, bsp_b, w_out, w_kv, pg, pb)


def _swa_kernel(h_ref, hn_ref, kv_ref, kvp_ref, w_in_hbm, bias_ref, scale_ref, w_out_hbm,
                pg_ref, pb_ref, o_ref, q_s, g_s, y_s, s_s, w_in_ref, w_out_ref, st_in, st_out, sems,
                *, alpha, tiles_per_seq):
    tm, _ = h_ref.shape
    bw = y_s.shape[1]
    n_kv = bw // (HEAD_DIM * Q_PER_KV)
    pairs_per_kv = Q_PER_KV // 2
    i = pl.program_id(0)

    n_blocks = tm // PROJ_ROWS
    chunks_per_block = PROJ_ROWS // CHUNK

    def project_q(src_ref, b):
        rows = slice(b * PROJ_ROWS, (b + 1) * PROJ_ROWS)
        q_s[rows, :] = (jnp.dot(src_ref[rows, :].astype(BF16), w_in_ref[:, :bw], preferred_element_type=F32)
                        * (HEAD_DIM ** -0.5)).astype(BF16)

    def project_z(src_ref, b):
        rows = slice(b * PROJ_ROWS, (b + 1) * PROJ_ROWS)
        g_s[rows, :] = _silu(jnp.dot(src_ref[rows, :].astype(BF16), w_in_ref[:, bw:],
                                     preferred_element_type=F32))

    @pl.when(i == 0)
    def _():
        _cast_weight(w_in_hbm, w_in_ref, st_in, sems, 0)
        _cast_weight(w_out_hbm, w_out_ref, st_out, sems, 1)
        for b in range(n_blocks):
            project_q(h_ref, b)
            project_z(h_ref, b)

    as_bf16 = lambda a: a.astype(F32).astype(BF16)
    lo = as_bf16(lax.broadcasted_iota(jnp.int32, (2 * CHUNK, LANES), 1)) < HEAD_DIM
    key0 = as_bf16(lax.broadcasted_iota(jnp.int32, (2 * CHUNK, LANES), 0)) < 1
    zeros_k = jnp.zeros((2 * CHUNK, LANES), BF16)
    zeros_v = jnp.zeros((HEAD_DIM, 2 * CHUNK), BF16)
    zeros_d = jnp.zeros((DEN_ROWS, 2 * CHUNK), BF16)
    ones_d = jnp.ones((DEN_ROWS, 2 * CHUNK), BF16)
    first = (i % tiles_per_seq == 0).astype(jnp.int32)

    bands = {}

    def band_of(c):
        if c not in bands:
            rows = slice(c * CHUNK, (c + 1) * CHUNK)
            kv_prev = kvp_ref[...] if c == 0 else kv_ref[(c - 1) * CHUNK:c * CHUNK, :]
            band = jnp.concatenate([kv_prev, kv_ref[rows, :]], axis=0)
            kb, vb = band[:, :LANES], band[:, LANES:]
            kr = pltpu.roll(kb, HEAD_DIM, axis=1)
            v_t = jnp.where(key0, zeros_k, vb).astype(F32).T.astype(BF16)
            bands[c] = (kb, kr, v_t)
        return bands[c]

    def scores_of(c, kvh, slot):
        rows = slice(c * CHUNK, (c + 1) * CHUNK)
        kb, kr, v_t = band_of(c)
        variant = first if c == 0 else 0
        k_src, k_alt = (kb, kr) if kvh == 0 else (kr, kb)
        k_cat = jnp.concatenate([jnp.where(lo, k_src, zeros_k), jnp.where(lo, zeros_k, k_alt)], axis=0)
        vk = v_t[kvh * HEAD_DIM:(kvh + 1) * HEAD_DIM]
        v_lhs = jnp.concatenate(
            [jnp.concatenate([vk, zeros_v], axis=1), jnp.concatenate([zeros_v, vk], axis=1),
             jnp.concatenate([ones_d, zeros_d], axis=1), jnp.concatenate([zeros_d, ones_d], axis=1)],
            axis=0)
        pair0 = kvh * pairs_per_kv
        q_stack = jnp.concatenate(
            [q_s[rows, (pair0 + p) * LANES:(pair0 + p + 1) * LANES] for p in range(pairs_per_kv)], axis=0)
        s_s[slot] = lax.dot_general(k_cat, q_stack, (((1,), (1,)), ((), ())),
                                    preferred_element_type=F32)
        return rows, variant, kvh, slot, v_lhs

    def attend(rows, variant, kvh, slot, v_lhs):
        pair0 = kvh * pairs_per_kv
        e_cols = []
        for p in range(pairs_per_kv):
            qc = slice(p * CHUNK, (p + 1) * CHUNK)
            halves = []
            for e in range(2):
                kr_ = slice(e * 2 * CHUNK, (e + 1) * 2 * CHUNK)
                le = s_s[slot, kr_, qc] * scale_ref[variant, kr_, :] + bias_ref[variant, kvh, kr_, qc]
                halves.append(jnp.exp2(le - jnp.max(le, axis=0, keepdims=True)).astype(BF16))
            e_cols.append(jnp.concatenate(halves, axis=0))
        pv_t = jnp.dot(v_lhs, jnp.concatenate(e_cols, axis=1), preferred_element_type=F32)
        n_num = 2 * HEAD_DIM
        sub = 8
        inv_e = 1.0 / pv_t[n_num:n_num + sub]
        inv_o = 1.0 / pv_t[n_num + DEN_ROWS:n_num + DEN_ROWS + sub]
        inv = jnp.concatenate([inv_e] * (HEAD_DIM // sub) + [inv_o] * (HEAD_DIM // sub), axis=0)
        o_t = pv_t[:n_num] * inv
        for p in range(pairs_per_kv):
            pc = slice((pair0 + p) * LANES, (pair0 + p + 1) * LANES)
            y_s[rows, pc] = (o_t[:, p * CHUNK:(p + 1) * CHUNK].T * g_s[rows, pc]).astype(BF16)

    block_units = [[(c, kvh) for c in range(b * chunks_per_block, (b + 1) * chunks_per_block)
                    for kvh in range(n_kv)] for b in range(n_blocks)]
    units_per_block = chunks_per_block * n_kv
    slot_of = lambda b, j: (b % 2) * units_per_block + j
    scored = [scores_of(*u, slot_of(0, j)) for j, u in enumerate(block_units[0])]
    for b in range(n_blocks):
        rows = slice(b * PROJ_ROWS, (b + 1) * PROJ_ROWS)
        scored_next = []
        for j, unit in enumerate(scored):
            attend(*unit)
            if b + 1 < n_blocks:
                scored_next.append(scores_of(*block_units[b + 1][j], slot_of(b + 1, j)))
        scored = scored_next
        project_q(hn_ref, b)
        sub = jnp.dot(y_s[rows, :], w_out_ref[...], preferred_element_type=F32)
        project_z(hn_ref, b)
        o_ref[rows, :] = _layer_norm(alpha * h_ref[rows, :] + sub, pg_ref[...], pb_ref[...])


def _swa_layer(h1, kv, w_in, bias_tab, scale_tab, w_out, pg, pb, *, alpha, tm, seq):
    t_tokens, d_model = h1.shape
    bw = w_out.shape[0]
    kvw2 = kv.shape[1]
    cpt = tm // CHUNK
    n_tiles = t_tokens // tm
    whole = pl.BlockSpec(memory_space=pltpu.VMEM)
    hbm = pl.BlockSpec(memory_space=pl.ANY)
    stage = lambda w: pltpu.VMEM((STAGE_SLOTS, _stage_rows(w), w.shape[1]), F32)
    return pl.pallas_call(
        functools.partial(_swa_kernel, alpha=alpha, tiles_per_seq=seq // tm),
        grid=(n_tiles,),
        in_specs=[pl.BlockSpec((tm, d_model), lambda i: (i, 0)),
                  pl.BlockSpec((tm, d_model), lambda i: (jnp.minimum(i + 1, n_tiles - 1), 0)),
                  pl.BlockSpec((tm, kvw2), lambda i: (i, 0)),
                  pl.BlockSpec((CHUNK, kvw2), lambda i: (jnp.maximum(i * cpt - 1, 0), 0)),
                  hbm, whole, whole, hbm, whole, whole],
        out_specs=pl.BlockSpec((tm, d_model), lambda i: (i, 0)),
        out_shape=jax.ShapeDtypeStruct((t_tokens, d_model), F32),
        scratch_shapes=[pltpu.VMEM((tm, bw), BF16), pltpu.VMEM((tm, bw), F32), pltpu.VMEM((tm, bw), BF16),
                        pltpu.VMEM((2 * (PROJ_ROWS // CHUNK) * (bw // (HEAD_DIM * Q_PER_KV)), 4 * CHUNK, 4 * CHUNK), F32),
                        pltpu.VMEM(w_in.shape, BF16), pltpu.VMEM(w_out.shape, BF16),
                        stage(w_in), stage(w_out), pltpu.SemaphoreType.DMA((2, STAGE_SLOTS))],
        compiler_params=pltpu.CompilerParams(
            dimension_semantics=("arbitrary",),
            vmem_limit_bytes=V7X_VMEM_BYTES * 7 // 8),
        name="swa_layer",
    )(h1, h1, kv, kv, w_in, bias_tab, scale_tab, w_out, pg, pb)


def kernel(x, w_in_a, sgu_ln_g, sgu_ln_b, w_spatial, b_spatial, w_out_a, w_kv, w_in_b, attn_sinks,
           rel_bias, w_out_b, post_ln_g, post_ln_b):
    bsz, seq, d_model = x.shape
    depth = post_ln_g.shape[0]
    assert w_in_a.shape[0] == 1 and w_in_b.shape[0] == 1 and depth == 2
    assert seq % CHUNK == 0 and w_spatial.shape[1:] == (A_GROUPS, CHUNK, CHUNK)
    assert w_kv.shape[1] == 2 * LANES and rel_bias.shape == (REL_BUCKETS, w_out_b.shape[1] // HEAD_DIM)
    alpha = (2.0 * depth) ** 0.25
    aw = w_out_a.shape[1]
    gd = aw // A_GROUPS
    tm = 4 * CHUNK

    x2 = x.reshape(bsz * seq, d_model)
    row = lambda a: a.reshape(1, -1)
    bsp_b = jnp.broadcast_to(b_spatial[0][:, :, None], (A_GROUPS, CHUNK, gd))
    h1, kv = _sgu_layer(x2, w_in_a[0], row(sgu_ln_g[0]), row(sgu_ln_b[0]), w_spatial[0], bsp_b,
                        w_out_a[0], w_kv, row(post_ln_g[0]), row(post_ln_b[0]),
                        alpha=alpha, tm=tm)
    bias_tab = _rel_bias_table(rel_bias, attn_sinks[0])
    out = _swa_layer(h1, kv, w_in_b[0], bias_tab, jnp.asarray(_band_scale_table()),
                     w_out_b[0], row(post_ln_g[1]), row(post_ln_b[1]),
                     alpha=alpha, tm=tm, seq=seq)
    return out.reshape(bsz, seq, d_model)
```

```python
import functools
import math

import numpy as np
import jax
import jax.numpy as jnp
from jax import lax
from jax.experimental import pallas as pl
from jax.experimental.pallas import tpu as pltpu

F32 = jnp.float32
BF16 = jnp.bfloat16

CHUNK = 128
A_GROUPS = 8
HEAD_DIM = 64
Q_PER_KV = 8
REL_BUCKETS = 32
REL_MAX_DIST = 128
LN_EPS = 1e-5
NEG_INF = -1e30
LOG2E = math.log2(math.e)

LANES = 128
PROJ_ROWS = 2 * CHUNK
SGU_ROWS = 4 * CHUNK
DEN_ROWS = 16
SPATIAL_LAG = 1
V7X_VMEM_BYTES = 64 * 1024 * 1024
STAGE_BYTES = 256 * 1024
STAGE_SLOTS = 8


def _bucket_lower_bounds():
    max_exact = REL_BUCKETS // 2
    d = np.arange(REL_MAX_DIST, dtype=np.int32)
    df = np.maximum(d, 1).astype(np.float32)
    large = max_exact + (np.log(df / np.float32(max_exact)) / np.float32(math.log(REL_MAX_DIST / max_exact))
                         * np.float32(REL_BUCKETS - max_exact)).astype(np.int32)
    large = np.minimum(large, REL_BUCKETS - 1)
    bucket = np.where(d < max_exact, d, large)
    assert np.all(np.diff(bucket) >= 0) and bucket[0] == 0
    return [int(np.argmax(bucket >= b)) if np.any(bucket >= b) else REL_MAX_DIST for b in range(REL_BUCKETS)]


def _band_scale_table():
    j = np.arange(2 * CHUNK)[:, None]
    t = np.arange(CHUNK)[None, :]
    d = t + CHUNK - j
    in_window = (d >= 0) & (d < CHUNK)
    first = in_window & (j >= CHUNK)
    tab = np.stack([np.tile(in_window, (2, 1)), np.tile(first, (2, 1))]).astype(np.float32)
    return tab * np.float32(LOG2E)


def _layer_norm(x, g, b):
    mu = jnp.mean(x, axis=-1, keepdims=True)
    d = x - mu
    var = jnp.mean(d * d, axis=-1, keepdims=True)
    return d * lax.rsqrt(var + LN_EPS) * g + b


def _silu(z):
    half = 0.5 * z
    return half + half * jnp.tanh(half)


def _stage_rows(w_hbm):
    k, n = w_hbm.shape
    rows = 8
    while 2 * rows * n * 4 <= STAGE_BYTES and k % (2 * rows) == 0:
        rows *= 2
    assert k % rows == 0
    return rows


def _cast_weight(w_hbm, w_s, stage, sems, k):
    rows = stage.shape[1]
    n = w_hbm.shape[0] // rows

    def copy(c):
        slot = c % STAGE_SLOTS
        return pltpu.make_async_copy(w_hbm.at[pl.ds(c * rows, rows), :], stage.at[slot], sems.at[k, slot])

    for c in range(min(STAGE_SLOTS - 1, n)):
        copy(c).start()
    for c in range(n):
        if c + STAGE_SLOTS - 1 < n:
            copy(c + STAGE_SLOTS - 1).start()
        copy(c).wait()
        w_s[c * rows:(c + 1) * rows, :] = stage[c % STAGE_SLOTS].astype(BF16)


def _rel_bias_kernel(rb_ref, sink_ref, out_ref, *, lower_bounds, pairs_per_kv):
    pair = pl.program_id(0) * pairs_per_kv + pl.program_id(1)
    j = lax.broadcasted_iota(jnp.int32, (2 * CHUNK, CHUNK), 0)
    t = lax.broadcasted_iota(jnp.int32, (2 * CHUNK, CHUNK), 1)
    dist = t + CHUNK - j
    in_window = (dist >= 0) & (dist < CHUNK)
    d = jnp.clip(dist, 0, REL_MAX_DIST - 1)
    for e in range(2):
        h = 2 * pair + e
        acc = jnp.full((2 * CHUNK, CHUNK), rb_ref[0, h], F32)
        for b in range(1, REL_BUCKETS):
            if lower_bounds[b] < REL_MAX_DIST:
                acc = jnp.where(d >= lower_bounds[b], rb_ref[b, h], acc)
        hidden = jnp.where(j == 0, sink_ref[h], NEG_INF)
        for variant in range(2):
            visible = in_window & (j >= variant * CHUNK)
            out_ref[variant, 0, e * 2 * CHUNK:(e + 1) * 2 * CHUNK, :] = jnp.where(visible, acc, hidden) * LOG2E


def _rel_bias_table(rel_bias, sinks):
    n_heads = rel_bias.shape[1]
    n_kv = n_heads // Q_PER_KV
    pairs_per_kv = Q_PER_KV // 2
    smem = pl.BlockSpec(memory_space=pltpu.SMEM)
    return pl.pallas_call(
        functools.partial(_rel_bias_kernel, lower_bounds=_bucket_lower_bounds(), pairs_per_kv=pairs_per_kv),
        grid=(n_kv, pairs_per_kv),
        in_specs=[smem, smem],
        out_specs=pl.BlockSpec((2, 1, 4 * CHUNK, CHUNK), lambda k, p: (0, k, 0, p)),
        out_shape=jax.ShapeDtypeStruct((2, n_kv, 4 * CHUNK, pairs_per_kv * CHUNK), F32),
        name="rel_bias_table",
    )(rel_bias, sinks)


def _sgu_kernel(x_ref, xn_ref, w_in_hbm, lng_ref, lnb_ref, ws_ref, bsp_ref, w_out_hbm, w_kv_hbm,
                pg_ref, pb_ref, h_ref, kv_ref, v_s, mu_s, rstd_s, y_s, w_in_ref, w_out_ref, w_kv_ref,
                st_in, st_out, st_kv, sems, *, alpha):
    tm, _ = x_ref.shape
    aw = v_s.shape[1]
    gd = aw // A_GROUPS
    n_chunks = tm // CHUNK
    n_blocks = tm // SGU_ROWS

    def project_v(src_ref, b):
        rows = slice(b * SGU_ROWS, (b + 1) * SGU_ROWS)
        v = jnp.dot(src_ref[rows, :].astype(BF16), w_in_ref[:, aw:2 * aw], preferred_element_type=F32)
        v_s[rows, :] = v
        mu = jnp.mean(v, axis=-1, keepdims=True)
        var = jnp.maximum(jnp.mean(v * v, axis=-1, keepdims=True) - mu * mu, 0.0)
        mu_s[rows, :] = jnp.broadcast_to(mu, (SGU_ROWS, LANES))
        rstd_s[rows, :] = jnp.broadcast_to(lax.rsqrt(var + LN_EPS), (SGU_ROWS, LANES))

    @pl.when(pl.program_id(0) == 0)
    def _():
        _cast_weight(w_in_hbm, w_in_ref, st_in, sems, 0)
        _cast_weight(w_out_hbm, w_out_ref, st_out, sems, 1)
        _cast_weight(w_kv_hbm, w_kv_ref, st_kv, sems, 2)
        for b in range(n_blocks):
            project_v(x_ref, b)

    xb = x_ref[...].astype(BF16)
    lane_tiles = gd // LANES
    mu = jnp.concatenate([mu_s[...]] * lane_tiles, axis=1)
    rstd = jnp.concatenate([rstd_s[...]] * lane_tiles, axis=1)

    row = lax.broadcasted_iota(jnp.int32, (CHUNK, CHUNK), 0)
    col = lax.broadcasted_iota(jnp.int32, (CHUNK, CHUNK), 1)
    tril = col <= row

    def gate_of(g):
        u = jnp.dot(xb, w_in_ref[:, g * gd:(g + 1) * gd], preferred_element_type=F32)
        z = jnp.dot(xb, w_in_ref[:, 2 * aw + g * gd:2 * aw + (g + 1) * gd], preferred_element_type=F32)
        return u * _silu(z)

    def spatial(g, gate):
        cols = slice(g * gd, (g + 1) * gd)
        ws = jnp.where(tril, ws_ref[g], 0.0).astype(BF16)
        vn = ((v_s[:, cols] - mu) * rstd * lng_ref[:, cols] + lnb_ref[:, cols]).astype(BF16)
        for c in range(n_chunks):
            rows = slice(c * CHUNK, (c + 1) * CHUNK)
            s = jnp.dot(ws, vn[rows], preferred_element_type=F32) + bsp_ref[g]
            y_s[rows, cols] = (gate[rows] * s).astype(BF16)

    gates = {g: gate_of(g) for g in range(SPATIAL_LAG)}
    for g in range(A_GROUPS):
        if g + SPATIAL_LAG < A_GROUPS:
            gates[g + SPATIAL_LAG] = gate_of(g + SPATIAL_LAG)
        spatial(g, gates.pop(g))

    def kv_of(b, h):
        for r in range(b * SGU_ROWS, (b + 1) * SGU_ROWS, CHUNK):
            hr = h[r - b * SGU_ROWS:r - b * SGU_ROWS + CHUNK]
            kv_ref[r:r + CHUNK, :] = jnp.dot(hr.astype(BF16), w_kv_ref[...],
                                             preferred_element_type=F32).astype(BF16)

    h_prev = None
    for b in range(n_blocks):
        rows = slice(b * SGU_ROWS, (b + 1) * SGU_ROWS)
        sub = jnp.dot(y_s[rows, :], w_out_ref[...], preferred_element_type=F32)
        if h_prev is not None:
            kv_of(b - 1, h_prev)
        project_v(xn_ref, b)
        h_prev = _layer_norm(alpha * x_ref[rows, :] + sub, pg_ref[...], pb_ref[...])
        h_ref[rows, :] = h_prev
    kv_of(n_blocks - 1, h_prev)


def _sgu_layer(x2, w_in, ln_g, ln_b, w_spatial, bsp_b, w_out, w_kv, pg, pb, *, alpha, tm):
    t_tokens, d_model = x2.shape
    aw = w_out.shape[0]
    kvw = w_kv.shape[1]
    n_tiles = t_tokens // tm
    whole = pl.BlockSpec(memory_space=pltpu.VMEM)
    hbm = pl.BlockSpec(memory_space=pl.ANY)
    stage = lambda w: pltpu.VMEM((STAGE_SLOTS, _stage_rows(w), w.shape[1]), F32)
    return pl.pallas_call(
        functools.partial(_sgu_kernel, alpha=alpha),
        grid=(n_tiles,),
        in_specs=[pl.BlockSpec((tm, d_model), lambda i: (i, 0)),
                  pl.BlockSpec((tm, d_model), lambda i: (jnp.minimum(i + 1, n_tiles - 1), 0)),
                  hbm, whole, whole, whole, whole, hbm, hbm, whole, whole],
        out_specs=[pl.BlockSpec((tm, d_model), lambda i: (i, 0)),
                   pl.BlockSpec((tm, kvw), lambda i: (i, 0))],
        out_shape=[jax.ShapeDtypeStruct((t_tokens, d_model), F32),
                   jax.ShapeDtypeStruct((t_tokens, kvw), BF16)],
        scratch_shapes=[pltpu.VMEM((tm, aw), F32), pltpu.VMEM((tm, LANES), F32), pltpu.VMEM((tm, LANES), F32),
                        pltpu.VMEM((tm, aw), BF16),
                        pltpu.VMEM(w_in.shape, BF16), pltpu.VMEM(w_out.shape, BF16), pltpu.VMEM(w_kv.shape, BF16),
                        stage(w_in), stage(w_out), stage(w_kv), pltpu.SemaphoreType.DMA((3, STAGE_SLOTS))],
        compiler_params=pltpu.CompilerParams(
            dimension_semantics=("arbitrary",),
            vmem_limit_bytes=V7X_VMEM_BYTES * 7 // 8),
        name="sgu_layer",
    )(x2, x2, w_in, ln_g, ln_b, w_spatial, bsp_b, w_out, w_kv, pg, pb)


def _swa_kernel(h_ref, hn_ref, kv_ref, kvp_ref, w_in_hbm, bias_ref, scale_ref, w_out_hbm,
                pg_ref, pb_ref, o_ref, q_s, g_s, y_s, s_s, w_in_ref, w_out_ref, st_in, st_out, sems,
                *, alpha, tiles_per_seq):
    tm, _ = h_ref.shape
    bw = y_s.shape[1]
    n_kv = bw // (HEAD_DIM * Q_PER_KV)
    pairs_per_kv = Q_PER_KV // 2
    i = pl.program_id(0)

    n_blocks = tm // PROJ_ROWS
    chunks_per_block = PROJ_ROWS // CHUNK

    def project_q(src_ref, b):
        rows = slice(b * PROJ_ROWS, (b + 1) * PROJ_ROWS)
        q_s[rows, :] = (jnp.dot(src_ref[rows, :].astype(BF16), w_in_ref[:, :bw], preferred_element_type=F32)
                        * (HEAD_DIM ** -0.5)).astype(BF16)

    def project_z(src_ref, b):
        rows = slice(b * PROJ_ROWS, (b + 1) * PROJ_ROWS)
        g_s[rows, :] = _silu(jnp.dot(src_ref[rows, :].astype(BF16), w_in_ref[:, bw:],
                                     preferred_element_type=F32))

    @pl.when(i == 0)
    def _():
        _cast_weight(w_in_hbm, w_in_ref, st_in, sems, 0)
        _cast_weight(w_out_hbm, w_out_ref, st_out, sems, 1)
        for b in range(n_blocks):
            project_q(h_ref, b)
            project_z(h_ref, b)

    as_bf16 = lambda a: a.astype(F32).astype(BF16)
    lo = as_bf16(lax.broadcasted_iota(jnp.int32, (2 * CHUNK, LANES), 1)) < HEAD_DIM
    key0 = as_bf16(lax.broadcasted_iota(jnp.int32, (2 * CHUNK, LANES), 0)) < 1
    zeros_k = jnp.zeros((2 * CHUNK, LANES), BF16)
    zeros_v = jnp.zeros((HEAD_DIM, 2 * CHUNK), BF16)
    zeros_d = jnp.zeros((DEN_ROWS, 2 * CHUNK), BF16)
    ones_d = jnp.ones((DEN_ROWS, 2 * CHUNK), BF16)
    first = (i % tiles_per_seq == 0).astype(jnp.int32)

    bands = {}

    def band_of(c):
        if c not in bands:
            rows = slice(c * CHUNK, (c + 1) * CHUNK)
            kv_prev = kvp_ref[...] if c == 0 else kv_ref[(c - 1) * CHUNK:c * CHUNK, :]
            band = jnp.concatenate([kv_prev, kv_ref[rows, :]], axis=0)
            kb, vb = band[:, :LANES], band[:, LANES:]
            kr = pltpu.roll(kb, HEAD_DIM, axis=1)
            v_t = jnp.where(key0, zeros_k, vb).astype(F32).T.astype(BF16)
            bands[c] = (kb, kr, v_t)
        return bands[c]

    def scores_of(c, kvh, slot):
        rows = slice(c * CHUNK, (c + 1) * CHUNK)
        kb, kr, v_t = band_of(c)
        variant = first if c == 0 else 0
        k_src, k_alt = (kb, kr) if kvh == 0 else (kr, kb)
        k_cat = jnp.concatenate([jnp.where(lo, k_src, zeros_k), jnp.where(lo, zeros_k, k_alt)], axis=0)
        vk = v_t[kvh * HEAD_DIM:(kvh + 1) * HEAD_DIM]
        v_lhs = jnp.concatenate(
            [jnp.concatenate([vk, zeros_v], axis=1), jnp.concatenate([zeros_v, vk], axis=1),
             jnp.concatenate([ones_d, zeros_d], axis=1), jnp.concatenate([zeros_d, ones_d], axis=1)],
            axis=0)
        pair0 = kvh * pairs_per_kv
        q_stack = jnp.concatenate(
            [q_s[rows, (pair0 + p) * LANES:(pair0 + p + 1) * LANES] for p in range(pairs_per_kv)], axis=0)
        s_s[slot] = lax.dot_general(k_cat, q_stack, (((1,), (1,)), ((), ())),
                                    preferred_element_type=F32)
        return rows, variant, kvh, slot, v_lhs

    def attend(rows, variant, kvh, slot, v_lhs):
        pair0 = kvh * pairs_per_kv
        e_cols = []
        for p in range(pairs_per_kv):
            qc = slice(p * CHUNK, (p + 1) * CHUNK)
            halves = []
            for e in range(2):
                kr_ = slice(e * 2 * CHUNK, (e + 1) * 2 * CHUNK)
                le = s_s[slot, kr_, qc] * scale_ref[variant, kr_, :] + bias_ref[variant, kvh, kr_, qc]
                halves.append(jnp.exp2(le - jnp.max(le, axis=0, keepdims=True)).astype(BF16))
            e_cols.append(jnp.concatenate(halves, axis=0))
        pv_t = jnp.dot(v_lhs, jnp.concatenate(e_cols, axis=1), preferred_element_type=F32)
        n_num = 2 * HEAD_DIM
        sub = 8
        inv_e = 1.0 / pv_t[n_num:n_num + sub]
        inv_o = 1.0 / pv_t[n_num + DEN_ROWS:n_num + DEN_ROWS + sub]
        inv = jnp.concatenate([inv_e] * (HEAD_DIM // sub) + [inv_o] * (HEAD_DIM // sub), axis=0)
        o_t = pv_t[:n_num] * inv
        for p in range(pairs_per_kv):
            pc = slice((pair0 + p) * LANES, (pair0 + p + 1) * LANES)
            y_s[rows, pc] = (o_t[:, p * CHUNK:(p + 1) * CHUNK].T * g_s[rows, pc]).astype(BF16)

    block_units = [[(c, kvh) for c in range(b * chunks_per_block, (b + 1) * chunks_per_block)
                    for kvh in range(n_kv)] for b in range(n_blocks)]
    units_per_block = chunks_per_block * n_kv
    slot_of = lambda b, j: (b % 2) * units_per_block + j
    scored = [scores_of(*u, slot_of(0, j)) for j, u in enumerate(block_units[0])]
    for b in range(n_blocks):
        rows = slice(b * PROJ_ROWS, (b + 1) * PROJ_ROWS)
        scored_next = []
        for j, unit in enumerate(scored):
            attend(*unit)
            if b + 1 < n_blocks:
                scored_next.append(scores_of(*block_units[b + 1][j], slot_of(b + 1, j)))
        scored = scored_next
        project_q(hn_ref, b)
        sub = jnp.dot(y_s[rows, :], w_out_ref[...], preferred_element_type=F32)
        project_z(hn_ref, b)
        o_ref[rows, :] = _layer_norm(alpha * h_ref[rows, :] + sub, pg_ref[...], pb_ref[...])


def _swa_layer(h1, kv, w_in, bias_tab, scale_tab, w_out, pg, pb, *, alpha, tm, seq):
    t_tokens, d_model = h1.shape
    bw = w_out.shape[0]
    kvw2 = kv.shape[1]
    cpt = tm // CHUNK
    n_tiles = t_tokens // tm
    whole = pl.BlockSpec(memory_space=pltpu.VMEM)
    hbm = pl.BlockSpec(memory_space=pl.ANY)
    stage = lambda w: pltpu.VMEM((STAGE_SLOTS, _stage_rows(w), w.shape[1]), F32)
    return pl.pallas_call(
        functools.partial(_swa_kernel, alpha=alpha, tiles_per_seq=seq // tm),
        grid=(n_tiles,),
        in_specs=[pl.BlockSpec((tm, d_model), lambda i: (i, 0)),
                  pl.BlockSpec((tm, d_model), lambda i: (jnp.minimum(i + 1, n_tiles - 1), 0)),
                  pl.BlockSpec((tm, kvw2), lambda i: (i, 0)),
                  pl.BlockSpec((CHUNK, kvw2), lambda i: (jnp.maximum(i * cpt - 1, 0), 0)),
                  hbm, whole, whole, hbm, whole, whole],
        out_specs=pl.BlockSpec((tm, d_model), lambda i: (i, 0)),
        out_shape=jax.ShapeDtypeStruct((t_tokens, d_model), F32),
        scratch_shapes=[pltpu.VMEM((tm, bw), BF16), pltpu.VMEM((tm, bw), F32), pltpu.VMEM((tm, bw), BF16),
                        pltpu.VMEM((2 * (PROJ_ROWS // CHUNK) * (bw // (HEAD_DIM * Q_PER_KV)), 4 * CHUNK, 4 * CHUNK), F32),
                        pltpu.VMEM(w_in.shape, BF16), pltpu.VMEM(w_out.shape, BF16),
                        stage(w_in), stage(w_out), pltpu.SemaphoreType.DMA((2, STAGE_SLOTS))],
        compiler_params=pltpu.CompilerParams(
            dimension_semantics=("arbitrary",),
            vmem_limit_bytes=V7X_VMEM_BYTES * 7 // 8),
        name="swa_layer",
    )(h1, h1, kv, kv, w_in, bias_tab, scale_tab, w_out, pg, pb)


def kernel(x, w_in_a, sgu_ln_g, sgu_ln_b, w_spatial, b_spatial, w_out_a, w_kv, w_in_b, attn_sinks,
           rel_bias, w_out_b, post_ln_g, post_ln_b):
    bsz, seq, d_model = x.shape
    depth = post_ln_g.shape[0]
    assert w_in_a.shape[0] == 1 and w_in_b.shape[0] == 1 and depth == 2
    assert seq % CHUNK == 0 and w_spatial.shape[1:] == (A_GROUPS, CHUNK, CHUNK)
    assert w_kv.shape[1] == 2 * LANES and rel_bias.shape == (REL_BUCKETS, w_out_b.shape[1] // HEAD_DIM)
    alpha = (2.0 * depth) ** 0.25
    aw = w_out_a.shape[1]
    gd = aw // A_GROUPS
    tm = 4 * CHUNK

    x2 = x.reshape(bsz * seq, d_model)
    row = lambda a: a.reshape(1, -1)
    bsp_b = jnp.broadcast_to(b_spatial[0][:, :, None], (A_GROUPS, CHUNK, gd))
    h1, kv = _sgu_layer(x2, w_in_a[0], row(sgu_ln_g[0]), row(sgu_ln_b[0]), w_spatial[0], bsp_b,
                        w_out_a[0], w_kv, row(post_ln_g[0]), row(post_ln_b[0]),
                        alpha=alpha, tm=tm)
    bias_tab = _rel_bias_table(rel_bias, attn_sinks[0])
    out = _swa_layer(h1, kv, w_in_b[0], bias_tab, jnp.asarray(_band_scale_table()),
                     w_out_b[0], row(post_ln_g[1]), row(post_ln_b[1]),
                     alpha=alpha, tm=tm, seq=seq)
    return out.reshape(bsz, seq, d_model)
```

```python
import functools
import math

import numpy as np
import jax
import jax.numpy as jnp
from jax import lax
from jax.experimental import pallas as pl
from jax.experimental.pallas import tpu as pltpu

F32 = jnp.float32
BF16 = jnp.bfloat16

CHUNK = 128
A_GROUPS = 8
HEAD_DIM = 64
Q_PER_KV = 8
REL_BUCKETS = 32
REL_MAX_DIST = 128
LN_EPS = 1e-5
NEG_INF = -1e30
LOG2E = math.log2(math.e)

LANES = 128
PROJ_ROWS = 2 * CHUNK
SGU_ROWS = 4 * CHUNK
DEN_ROWS = 16
SPATIAL_LAG = 1
V7X_VMEM_BYTES = 64 * 1024 * 1024
STAGE_BYTES = 256 * 1024
STAGE_SLOTS = 8


def _bucket_lower_bounds():
    max_exact = REL_BUCKETS // 2
    d = np.arange(REL_MAX_DIST, dtype=np.int32)
    df = np.maximum(d, 1).astype(np.float32)
    large = max_exact + (np.log(df / np.float32(max_exact)) / np.float32(math.log(REL_MAX_DIST / max_exact))
                         * np.float32(REL_BUCKETS - max_exact)).astype(np.int32)
    large = np.minimum(large, REL_BUCKETS - 1)
    bucket = np.where(d < max_exact, d, large)
    assert np.all(np.diff(bucket) >= 0) and bucket[0] == 0
    return [int(np.argmax(bucket >= b)) if np.any(bucket >= b) else REL_MAX_DIST for b in range(REL_BUCKETS)]


def _band_scale_table():
    j = np.arange(2 * CHUNK)[:, None]
    t = np.arange(CHUNK)[None, :]
    d = t + CHUNK - j
    in_window = (d >= 0) & (d < CHUNK)
    first = in_window & (j >= CHUNK)
    tab = np.stack([np.tile(in_window, (2, 1)), np.tile(first, (2, 1))]).astype(np.float32)
    return tab * np.float32(LOG2E)


def _layer_norm(x, g, b):
    mu = jnp.mean(x, axis=-1, keepdims=True)
    d = x - mu
    var = jnp.mean(d * d, axis=-1, keepdims=True)
    return d * lax.rsqrt(var + LN_EPS) * g + b


def _silu(z):
    half = 0.5 * z
    return half + half * jnp.tanh(half)


def _stage_rows(w_hbm):
    k, n = w_hbm.shape
    rows = 8
    while 2 * rows * n * 4 <= STAGE_BYTES and k % (2 * rows) == 0:
        rows *= 2
    assert k % rows == 0
    return rows


def _cast_weight(w_hbm, w_s, stage, sems, k):
    rows = stage.shape[1]
    n = w_hbm.shape[0] // rows

    def copy(c):
        slot = c % STAGE_SLOTS
        return pltpu.make_async_copy(w_hbm.at[pl.ds(c * rows, rows), :], stage.at[slot], sems.at[k, slot])

    for c in range(min(STAGE_SLOTS - 1, n)):
        copy(c).start()
    for c in range(n):
        if c + STAGE_SLOTS - 1 < n:
            copy(c + STAGE_SLOTS - 1).start()
        copy(c).wait()
        w_s[c * rows:(c + 1) * rows, :] = stage[c % STAGE_SLOTS].astype(BF16)


def _rel_bias_kernel(rb_ref, sink_ref, out_ref, *, lower_bounds, pairs_per_kv):
    pair = pl.program_id(0) * pairs_per_kv + pl.program_id(1)
    j = lax.broadcasted_iota(jnp.int32, (2 * CHUNK, CHUNK), 0)
    t = lax.broadcasted_iota(jnp.int32, (2 * CHUNK, CHUNK), 1)
    dist = t + CHUNK - j
    in_window = (dist >= 0) & (dist < CHUNK)
    d = jnp.clip(dist, 0, REL_MAX_DIST - 1)
    for e in range(2):
        h = 2 * pair + e
        acc = jnp.full((2 * CHUNK, CHUNK), rb_ref[0, h], F32)
        for b in range(1, REL_BUCKETS):
            if lower_bounds[b] < REL_MAX_DIST:
                acc = jnp.where(d >= lower_bounds[b], rb_ref[b, h], acc)
        hidden = jnp.where(j == 0, sink_ref[h], NEG_INF)
        for variant in range(2):
            visible = in_window & (j >= variant * CHUNK)
            out_ref[variant, 0, e * 2 * CHUNK:(e + 1) * 2 * CHUNK, :] = jnp.where(visible, acc, hidden) * LOG2E


def _rel_bias_table(rel_bias, sinks):
    n_heads = rel_bias.shape[1]
    n_kv = n_heads // Q_PER_KV
    pairs_per_kv = Q_PER_KV // 2
    smem = pl.BlockSpec(memory_space=pltpu.SMEM)
    return pl.pallas_call(
        functools.partial(_rel_bias_kernel, lower_bounds=_bucket_lower_bounds(), pairs_per_kv=pairs_per_kv),
        grid=(n_kv, pairs_per_kv),
        in_specs=[smem, smem],
        out_specs=pl.BlockSpec((2, 1, 4 * CHUNK, CHUNK), lambda k, p: (0, k, 0, p)),
        out_shape=jax.ShapeDtypeStruct((2, n_kv, 4 * CHUNK, pairs_per_kv * CHUNK), F32),
        name="rel_bias_table",
    )(rel_bias, sinks)


def _sgu_kernel(x_ref, xn_ref, w_in_hbm, lng_ref, lnb_ref, ws_ref, bsp_ref, w_out_hbm, w_kv_hbm,
                pg_ref, pb_ref, h_ref, kv_ref, v_s, mu_s, rstd_s, y_s, w_in_ref, w_out_ref, w_kv_ref,
                st_in, st_out, st_kv, sems, *, alpha):
    tm, _ = x_ref.shape
    aw = v_s.shape[1]
    gd = aw // A_GROUPS
    n_chunks = tm // CHUNK
    n_blocks = tm // SGU_ROWS

    def project_v(src_ref, b):
        rows = slice(b * SGU_ROWS, (b + 1) * SGU_ROWS)
        v = jnp.dot(src_ref[rows, :].astype(BF16), w_in_ref[:, aw:2 * aw], preferred_element_type=F32)
        v_s[rows, :] = v
        mu = jnp.mean(v, axis=-1, keepdims=True)
        var = jnp.maximum(jnp.mean(v * v, axis=-1, keepdims=True) - mu * mu, 0.0)
        mu_s[rows, :] = jnp.broadcast_to(mu, (SGU_ROWS, LANES))
        rstd_s[rows, :] = jnp.broadcast_to(lax.rsqrt(var + LN_EPS), (SGU_ROWS, LANES))

    @pl.when(pl.program_id(0) == 0)
    def _():
        _cast_weight(w_in_hbm, w_in_ref, st_in, sems, 0)
        _cast_weight(w_out_hbm, w_out_ref, st_out, sems, 1)
        _cast_weight(w_kv_hbm, w_kv_ref, st_kv, sems, 2)
        for b in range(n_blocks):
            project_v(x_ref, b)

    xb = x_ref[...].astype(BF16)
    lane_tiles = gd // LANES
    mu = jnp.concatenate([mu_s[...]] * lane_tiles, axis=1)
    rstd = jnp.concatenate([rstd_s[...]] * lane_tiles, axis=1)

    row = lax.broadcasted_iota(jnp.int32, (CHUNK, CHUNK), 0)
    col = lax.broadcasted_iota(jnp.int32, (CHUNK, CHUNK), 1)
    tril = col <= row

    def gate_of(g):
        u = jnp.dot(xb, w_in_ref[:, g * gd:(g + 1) * gd], preferred_element_type=F32)
        z = jnp.dot(xb, w_in_ref[:, 2 * aw + g * gd:2 * aw + (g + 1) * gd], preferred_element_type=F32)
        return u * _silu(z)

    def spatial(g, gate):
        cols = slice(g * gd, (g + 1) * gd)
        ws = jnp.where(tril, ws_ref[g], 0.0).astype(BF16)
        vn = ((v_s[:, cols] - mu) * rstd * lng_ref[:, cols] + lnb_ref[:, cols]).astype(BF16)
        for c in range(n_chunks):
            rows = slice(c * CHUNK, (c + 1) * CHUNK)
            s = jnp.dot(ws, vn[rows], preferred_element_type=F32) + bsp_ref[g]
            y_s[rows, cols] = (gate[rows] * s).astype(BF16)

    gates = {g: gate_of(g) for g in range(SPATIAL_LAG)}
    for g in range(A_GROUPS):
        if g + SPATIAL_LAG < A_GROUPS:
            gates[g + SPATIAL_LAG] = gate_of(g + SPATIAL_LAG)
        spatial(g, gates.pop(g))

    out_blocks = [slice(r, r + PROJ_ROWS) for r in range(0, tm, PROJ_ROWS)]
    subs = [jnp.dot(y_s[rows, :], w_out_ref[...], preferred_element_type=F32) for rows in out_blocks]
    for b in range(n_blocks):
        project_v(xn_ref, b)
    for rows, sub in zip(out_blocks, subs):
        h = _layer_norm(alpha * x_ref[rows, :] + sub, pg_ref[...], pb_ref[...])
        h_ref[rows, :] = h
        for r in range(0, PROJ_ROWS, CHUNK):
            kv_ref[rows.start + r:rows.start + r + CHUNK, :] = jnp.dot(
                h[r:r + CHUNK].astype(BF16), w_kv_ref[...], preferred_element_type=F32).astype(BF16)


def _sgu_layer(x2, w_in, ln_g, ln_b, w_spatial, bsp_b, w_out, w_kv, pg, pb, *, alpha, tm):
    t_tokens, d_model = x2.shape
    aw = w_out.shape[0]
    kvw = w_kv.shape[1]
    n_tiles = t_tokens // tm
    whole = pl.BlockSpec(memory_space=pltpu.VMEM)
    hbm = pl.BlockSpec(memory_space=pl.ANY)
    stage = lambda w: pltpu.VMEM((STAGE_SLOTS, _stage_rows(w), w.shape[1]), F32)
    return pl.pallas_call(
        functools.partial(_sgu_kernel, alpha=alpha),
        grid=(n_tiles,),
        in_specs=[pl.BlockSpec((tm, d_model), lambda i: (i, 0)),
                  pl.BlockSpec((tm, d_model), lambda i: (jnp.minimum(i + 1, n_tiles - 1), 0)),
                  hbm, whole, whole, whole, whole, hbm, hbm, whole, whole],
        out_specs=[pl.BlockSpec((tm, d_model), lambda i: (i, 0)),
                   pl.BlockSpec((tm, kvw), lambda i: (i, 0))],
        out_shape=[jax.ShapeDtypeStruct((t_tokens, d_model), F32),
                   jax.ShapeDtypeStruct((t_tokens, kvw), BF16)],
        scratch_shapes=[pltpu.VMEM((tm, aw), F32), pltpu.VMEM((tm, LANES), F32), pltpu.VMEM((tm, LANES), F32),
                        pltpu.VMEM((tm, aw), BF16),
                        pltpu.VMEM(w_in.shape, BF16), pltpu.VMEM(w_out.shape, BF16), pltpu.VMEM(w_kv.shape, BF16),
                        stage(w_in), stage(w_out), stage(w_kv), pltpu.SemaphoreType.DMA((3, STAGE_SLOTS))],
        compiler_params=pltpu.CompilerParams(
            dimension_semantics=("arbitrary",),
            vmem_limit_bytes=V7X_VMEM_BYTES * 7 // 8),
        name="sgu_layer",
    )(x2, x2, w_in, ln_g, ln_b, w_spatial, bsp_b, w_out, w_kv, pg, pb)


def _swa_kernel(h_ref, hn_ref, kv_ref, kvp_ref, w_in_hbm, bias_ref, scale_ref, w_out_hbm,
                pg_ref, pb_ref, o_ref, q_s, g_s, y_s, s_s, w_in_ref, w_out_ref, st_in, st_out, sems,
                *, alpha, tiles_per_seq):
    tm, _ = h_ref.shape
    bw = y_s.shape[1]
    n_kv = bw // (HEAD_DIM * Q_PER_KV)
    pairs_per_kv = Q_PER_KV // 2
    i = pl.program_id(0)

    n_blocks = tm // PROJ_ROWS
    chunks_per_block = PROJ_ROWS // CHUNK

    def project_q(src_ref, b):
        rows = slice(b * PROJ_ROWS, (b + 1) * PROJ_ROWS)
        q_s[rows, :] = (jnp.dot(src_ref[rows, :].astype(BF16), w_in_ref[:, :bw], preferred_element_type=F32)
                        * (HEAD_DIM ** -0.5)).astype(BF16)

    def project_z(src_ref, b):
        rows = slice(b * PROJ_ROWS, (b + 1) * PROJ_ROWS)
        g_s[rows, :] = _silu(jnp.dot(src_ref[rows, :].astype(BF16), w_in_ref[:, bw:],
                                     preferred_element_type=F32))

    @pl.when(i == 0)
    def _():
        _cast_weight(w_in_hbm, w_in_ref, st_in, sems, 0)
        _cast_weight(w_out_hbm, w_out_ref, st_out, sems, 1)
        for b in range(n_blocks):
            project_q(h_ref, b)
            project_z(h_ref, b)

    as_bf16 = lambda a: a.astype(F32).astype(BF16)
    lo = as_bf16(lax.broadcasted_iota(jnp.int32, (2 * CHUNK, LANES), 1)) < HEAD_DIM
    key0 = as_bf16(lax.broadcasted_iota(jnp.int32, (2 * CHUNK, LANES), 0)) < 1
    zeros_k = jnp.zeros((2 * CHUNK, LANES), BF16)
    zeros_v = jnp.zeros((HEAD_DIM, 2 * CHUNK), BF16)
    zeros_d = jnp.zeros((DEN_ROWS, 2 * CHUNK), BF16)
    ones_d = jnp.ones((DEN_ROWS, 2 * CHUNK), BF16)
    first = (i % tiles_per_seq == 0).astype(jnp.int32)

    bands = {}

    def band_of(c):
        if c not in bands:
            rows = slice(c * CHUNK, (c + 1) * CHUNK)
            kv_prev = kvp_ref[...] if c == 0 else kv_ref[(c - 1) * CHUNK:c * CHUNK, :]
            band = jnp.concatenate([kv_prev, kv_ref[rows, :]], axis=0)
            kb, vb = band[:, :LANES], band[:, LANES:]
            kr = pltpu.roll(kb, HEAD_DIM, axis=1)
            v_t = jnp.where(key0, zeros_k, vb).astype(F32).T.astype(BF16)
            bands[c] = (kb, kr, v_t)
        return bands[c]

    def scores_of(c, kvh, slot):
        rows = slice(c * CHUNK, (c + 1) * CHUNK)
        kb, kr, v_t = band_of(c)
        variant = first if c == 0 else 0
        k_src, k_alt = (kb, kr) if kvh == 0 else (kr, kb)
        k_cat = jnp.concatenate([jnp.where(lo, k_src, zeros_k), jnp.where(lo, zeros_k, k_alt)], axis=0)
        vk = v_t[kvh * HEAD_DIM:(kvh + 1) * HEAD_DIM]
        v_lhs = jnp.concatenate(
            [jnp.concatenate([vk, zeros_v], axis=1), jnp.concatenate([zeros_v, vk], axis=1),
             jnp.concatenate([ones_d, zeros_d], axis=1), jnp.concatenate([zeros_d, ones_d], axis=1)],
            axis=0)
        pair0 = kvh * pairs_per_kv
        q_stack = jnp.concatenate(
            [q_s[rows, (pair0 + p) * LANES:(pair0 + p + 1) * LANES] for p in range(pairs_per_kv)], axis=0)
        s_s[slot] = lax.dot_general(k_cat, q_stack, (((1,), (1,)), ((), ())),
                                    preferred_element_type=F32)
        return rows, variant, kvh, slot, v_lhs

    def attend(rows, variant, kvh, slot, v_lhs):
        pair0 = kvh * pairs_per_kv
        e_cols = []
        for p in range(pairs_per_kv):
            qc = slice(p * CHUNK, (p + 1) * CHUNK)
            halves = []
            for e in range(2):
                kr_ = slice(e * 2 * CHUNK, (e + 1) * 2 * CHUNK)
                le = s_s[slot, kr_, qc] * scale_ref[variant, kr_, :] + bias_ref[variant, kvh, kr_, qc]
                halves.append(jnp.exp2(le - jnp.max(le, axis=0, keepdims=True)).astype(BF16))
            e_cols.append(jnp.concatenate(halves, axis=0))
        pv_t = jnp.dot(v_lhs, jnp.concatenate(e_cols, axis=1), preferred_element_type=F32)
        n_num = 2 * HEAD_DIM
        sub = 8
        inv_e = 1.0 / pv_t[n_num:n_num + sub]
        inv_o = 1.0 / pv_t[n_num + DEN_ROWS:n_num + DEN_ROWS + sub]
        inv = jnp.concatenate([inv_e] * (HEAD_DIM // sub) + [inv_o] * (HEAD_DIM // sub), axis=0)
        o_t = pv_t[:n_num] * inv
        for p in range(pairs_per_kv):
            pc = slice((pair0 + p) * LANES, (pair0 + p + 1) * LANES)
            y_s[rows, pc] = (o_t[:, p * CHUNK:(p + 1) * CHUNK].T * g_s[rows, pc]).astype(BF16)

    block_units = [[(c, kvh) for c in range(b * chunks_per_block, (b + 1) * chunks_per_block)
                    for kvh in range(n_kv)] for b in range(n_blocks)]
    units_per_block = chunks_per_block * n_kv
    slot_of = lambda b, j: (b % 2) * units_per_block + j
    scored = [scores_of(*u, slot_of(0, j)) for j, u in enumerate(block_units[0])]
    for b in range(n_blocks):
        rows = slice(b * PROJ_ROWS, (b + 1) * PROJ_ROWS)
        scored_next = []
        for j, unit in enumerate(scored):
            attend(*unit)
            if b + 1 < n_blocks:
                scored_next.append(scores_of(*block_units[b + 1][j], slot_of(b + 1, j)))
        scored = scored_next
        project_q(hn_ref, b)
        sub = jnp.dot(y_s[rows, :], w_out_ref[...], preferred_element_type=F32)
        project_z(hn_ref, b)
        o_ref[rows, :] = _layer_norm(alpha * h_ref[rows, :] + sub, pg_ref[...], pb_ref[...])


def _swa_layer(h1, kv, w_in, bias_tab, scale_tab, w_out, pg, pb, *, alpha, tm, seq):
    t_tokens, d_model = h1.shape
    bw = w_out.shape[0]
    kvw2 = kv.shape[1]
    cpt = tm // CHUNK
    n_tiles = t_tokens // tm
    whole = pl.BlockSpec(memory_space=pltpu.VMEM)
    hbm = pl.BlockSpec(memory_space=pl.ANY)
    stage = lambda w: pltpu.VMEM((STAGE_SLOTS, _stage_rows(w), w.shape[1]), F32)
    return pl.pallas_call(
        functools.partial(_swa_kernel, alpha=alpha, tiles_per_seq=seq // tm),
        grid=(n_tiles,),
        in_specs=[pl.BlockSpec((tm, d_model), lambda i: (i, 0)),
                  pl.BlockSpec((tm, d_model), lambda i: (jnp.minimum(i + 1, n_tiles - 1), 0)),
                  pl.BlockSpec((tm, kvw2), lambda i: (i, 0)),
                  pl.BlockSpec((CHUNK, kvw2), lambda i: (jnp.maximum(i * cpt - 1, 0), 0)),
                  hbm, whole, whole, hbm, whole, whole],
        out_specs=pl.BlockSpec((tm, d_model), lambda i: (i, 0)),
        out_shape=jax.ShapeDtypeStruct((t_tokens, d_model), F32),
        scratch_shapes=[pltpu.VMEM((tm, bw), BF16), pltpu.VMEM((tm, bw), F32), pltpu.VMEM((tm, bw), BF16),
                        pltpu.VMEM((2 * (PROJ_ROWS // CHUNK) * (bw // (HEAD_DIM * Q_PER_KV)), 4 * CHUNK, 4 * CHUNK), F32),
                        pltpu.VMEM(w_in.shape, BF16), pltpu.VMEM(w_out.shape, BF16),
                        stage(w_in), stage(w_out), pltpu.SemaphoreType.DMA((2, STAGE_SLOTS))],
        compiler_params=pltpu.CompilerParams(
            dimension_semantics=("arbitrary",),
            vmem_limit_bytes=V7X_VMEM_BYTES * 7 // 8),
        name="swa_layer",
    )(h1, h1, kv, kv, w_in, bias_tab, scale_tab, w_out, pg, pb)


def kernel(x, w_in_a, sgu_ln_g, sgu_ln_b, w_spatial, b_spatial, w_out_a, w_kv, w_in_b, attn_sinks,
           rel_bias, w_out_b, post_ln_g, post_ln_b):
    bsz, seq, d_model = x.shape
    depth = post_ln_g.shape[0]
    assert w_in_a.shape[0] == 1 and w_in_b.shape[0] == 1 and depth == 2
    assert seq % CHUNK == 0 and w_spatial.shape[1:] == (A_GROUPS, CHUNK, CHUNK)
    assert w_kv.shape[1] == 2 * LANES and rel_bias.shape == (REL_BUCKETS, w_out_b.shape[1] // HEAD_DIM)
    alpha = (2.0 * depth) ** 0.25
    aw = w_out_a.shape[1]
    gd = aw // A_GROUPS
    tm = 4 * CHUNK

    x2 = x.reshape(bsz * seq, d_model)
    row = lambda a: a.reshape(1, -1)
    bsp_b = jnp.broadcast_to(b_spatial[0][:, :, None], (A_GROUPS, CHUNK, gd))
    h1, kv = _sgu_layer(x2, w_in_a[0], row(sgu_ln_g[0]), row(sgu_ln_b[0]), w_spatial[0], bsp_b,
                        w_out_a[0], w_kv, row(post_ln_g[0]), row(post_ln_b[0]),
                        alpha=alpha, tm=tm)
    bias_tab = _rel_bias_table(rel_bias, attn_sinks[0])
    out = _swa_layer(h1, kv, w_in_b[0], bias_tab, jnp.asarray(_band_scale_table()),
                     w_out_b[0], row(post_ln_g[1]), row(post_ln_b[1]),
                     alpha=alpha, tm=tm, seq=seq)
    return out.reshape(bsz, seq, d_model)
```

```python
import functools
import math

import numpy as np
import jax
import jax.numpy as jnp
from jax import lax
from jax.experimental import pallas as pl
from jax.experimental.pallas import tpu as pltpu

F32 = jnp.float32
BF16 = jnp.bfloat16

CHUNK = 128
A_GROUPS = 8
HEAD_DIM = 64
Q_PER_KV = 8
REL_BUCKETS = 32
REL_MAX_DIST = 128
LN_EPS = 1e-5
NEG_INF = -1e30
LOG2E = math.log2(math.e)

LANES = 128
SUBLANES = 8
PROJ_ROWS = 2 * CHUNK
TOKEN_TILE = 4 * CHUNK
SGU_ROWS = TOKEN_TILE
DEN_ROWS = 16
SPATIAL_LAG = 1
V7X_VMEM_BYTES = 64 * 1024 * 1024
COMPILER_TEMP_BYTES = 8 * 1024 * 1024
STAGE_BYTES = 256 * 1024
STAGE_SLOTS = 8


def _bucket_lower_bounds():
    max_exact = REL_BUCKETS // 2
    d = np.arange(REL_MAX_DIST, dtype=np.int32)
    df = np.maximum(d, 1).astype(np.float32)
    large = max_exact + (np.log(df / np.float32(max_exact)) / np.float32(math.log(REL_MAX_DIST / max_exact))
                         * np.float32(REL_BUCKETS - max_exact)).astype(np.int32)
    large = np.minimum(large, REL_BUCKETS - 1)
    bucket = np.where(d < max_exact, d, large)
    assert np.all(np.diff(bucket) >= 0) and bucket[0] == 0
    return [int(np.argmax(bucket >= b)) if np.any(bucket >= b) else REL_MAX_DIST for b in range(REL_BUCKETS)]


def _band_scale_table():
    j = np.arange(2 * CHUNK)[:, None]
    t = np.arange(CHUNK)[None, :]
    d = t + CHUNK - j
    in_window = (d >= 0) & (d < CHUNK)
    first = in_window & (j >= CHUNK)
    tab = np.stack([np.tile(in_window, (2, 1)), np.tile(first, (2, 1))]).astype(np.float32)
    return tab * np.float32(LOG2E)


def _layer_norm(x, g, b):
    mu = jnp.mean(x, axis=-1, keepdims=True)
    d = x - mu
    var = jnp.mean(d * d, axis=-1, keepdims=True)
    return d * lax.rsqrt(var + LN_EPS) * g + b


def _silu(z):
    half = 0.5 * z
    return half + half * jnp.tanh(half)


def _tile_bytes(shape, dtype):
    itemsize = jnp.dtype(dtype).itemsize
    sublanes = SUBLANES * 4 // itemsize
    *lead, rows, cols = (1,) + tuple(shape)
    return math.prod(lead) * (-(-rows // sublanes) * sublanes) * (-(-cols // LANES) * LANES) * itemsize


def _vmem_limit(resident, pipelined, scratch, live_values):
    size = lambda entries: sum(_tile_bytes(shape, dtype) for shape, dtype in entries)
    total = size(resident) + 2 * size(pipelined) + size(scratch) + size(live_values) + COMPILER_TEMP_BYTES
    assert total <= V7X_VMEM_BYTES, total
    return total


def _stage_rows(w_hbm):
    k, n = w_hbm.shape
    rows = SUBLANES
    while 2 * rows * n * jnp.dtype(F32).itemsize <= STAGE_BYTES and k % (2 * rows) == 0:
        rows *= 2
    assert k % rows == 0
    return rows


def _cast_weight(w_hbm, w_s, stage, sems, k):
    rows = stage.shape[1]
    n = w_hbm.shape[0] // rows

    def copy(c):
        slot = c % STAGE_SLOTS
        return pltpu.make_async_copy(w_hbm.at[pl.ds(c * rows, rows), :], stage.at[slot], sems.at[k, slot])

    for c in range(min(STAGE_SLOTS - 1, n)):
        copy(c).start()
    for c in range(n):
        if c + STAGE_SLOTS - 1 < n:
            copy(c + STAGE_SLOTS - 1).start()
        copy(c).wait()
        w_s[c * rows:(c + 1) * rows, :] = stage[c % STAGE_SLOTS].astype(BF16)


def _rel_bias_kernel(rb_ref, sink_ref, out_ref, *, lower_bounds, pairs_per_kv):
    pair = pl.program_id(0) * pairs_per_kv + pl.program_id(1)
    j = lax.broadcasted_iota(jnp.int32, (2 * CHUNK, CHUNK), 0)
    t = lax.broadcasted_iota(jnp.int32, (2 * CHUNK, CHUNK), 1)
    dist = t + CHUNK - j
    in_window = (dist >= 0) & (dist < CHUNK)
    d = jnp.clip(dist, 0, REL_MAX_DIST - 1)
    for e in range(2):
        h = 2 * pair + e
        acc = jnp.full((2 * CHUNK, CHUNK), rb_ref[0, h], F32)
        for b in range(1, REL_BUCKETS):
            if lower_bounds[b] < REL_MAX_DIST:
                acc = jnp.where(d >= lower_bounds[b], rb_ref[b, h], acc)
        hidden = jnp.where(j == 0, sink_ref[h], NEG_INF)
        for variant in range(2):
            visible = in_window & (j >= variant * CHUNK)
            out_ref[variant, 0, e * 2 * CHUNK:(e + 1) * 2 * CHUNK, :] = jnp.where(visible, acc, hidden) * LOG2E


def _rel_bias_table(rel_bias, sinks):
    n_heads = rel_bias.shape[1]
    n_kv = n_heads // Q_PER_KV
    pairs_per_kv = Q_PER_KV // 2
    smem = pl.BlockSpec(memory_space=pltpu.SMEM)
    return pl.pallas_call(
        functools.partial(_rel_bias_kernel, lower_bounds=_bucket_lower_bounds(), pairs_per_kv=pairs_per_kv),
        grid=(n_kv, pairs_per_kv),
        in_specs=[smem, smem],
        out_specs=pl.BlockSpec((2, 1, 4 * CHUNK, CHUNK), lambda k, p: (0, k, 0, p)),
        out_shape=jax.ShapeDtypeStruct((2, n_kv, 4 * CHUNK, pairs_per_kv * CHUNK), F32),
        name="rel_bias_table",
    )(rel_bias, sinks)


def _sgu_kernel(x_ref, xn_ref, w_in_hbm, lng_ref, lnb_ref, ws_ref, bsp_ref, w_out_hbm, w_kv_hbm,
                pg_ref, pb_ref, h_ref, kv_ref, v_s, mu_s, rstd_s, y_s, w_in_ref, w_out_ref, w_kv_ref,
                st_in, st_out, st_kv, sems, *, alpha):
    tm, _ = x_ref.shape
    aw = v_s.shape[1]
    gd = aw // A_GROUPS
    n_chunks = tm // CHUNK
    n_blocks = tm // SGU_ROWS

    def project_v(src_ref, b):
        rows = slice(b * SGU_ROWS, (b + 1) * SGU_ROWS)
        v = jnp.dot(src_ref[rows, :].astype(BF16), w_in_ref[:, aw:2 * aw], preferred_element_type=F32)
        v_s[rows, :] = v
        mu = jnp.mean(v, axis=-1, keepdims=True)
        var = jnp.maximum(jnp.mean(v * v, axis=-1, keepdims=True) - mu * mu, 0.0)
        mu_s[rows, :] = jnp.broadcast_to(mu, (SGU_ROWS, LANES))
        rstd_s[rows, :] = jnp.broadcast_to(lax.rsqrt(var + LN_EPS), (SGU_ROWS, LANES))

    @pl.when(pl.program_id(0) == 0)
    def _():
        _cast_weight(w_in_hbm, w_in_ref, st_in, sems, 0)
        _cast_weight(w_out_hbm, w_out_ref, st_out, sems, 1)
        _cast_weight(w_kv_hbm, w_kv_ref, st_kv, sems, 2)
        for b in range(n_blocks):
            project_v(x_ref, b)

    xb = x_ref[...].astype(BF16)
    lane_tiles = gd // LANES
    mu = jnp.concatenate([mu_s[...]] * lane_tiles, axis=1)
    rstd = jnp.concatenate([rstd_s[...]] * lane_tiles, axis=1)

    row = lax.broadcasted_iota(jnp.int32, (CHUNK, CHUNK), 0)
    col = lax.broadcasted_iota(jnp.int32, (CHUNK, CHUNK), 1)
    tril = col <= row

    def gate_of(g):
        u = jnp.dot(xb, w_in_ref[:, g * gd:(g + 1) * gd], preferred_element_type=F32)
        z = jnp.dot(xb, w_in_ref[:, 2 * aw + g * gd:2 * aw + (g + 1) * gd], preferred_element_type=F32)
        return u * _silu(z)

    def spatial(g, gate):
        cols = slice(g * gd, (g + 1) * gd)
        ws = jnp.where(tril, ws_ref[g], 0.0).astype(BF16)
        vn = ((v_s[:, cols] - mu) * rstd * lng_ref[:, cols] + lnb_ref[:, cols]).astype(BF16)
        for c in range(n_chunks):
            rows = slice(c * CHUNK, (c + 1) * CHUNK)
            s = jnp.dot(ws, vn[rows], preferred_element_type=F32) + bsp_ref[g]
            y_s[rows, cols] = (gate[rows] * s).astype(BF16)

    gates = {g: gate_of(g) for g in range(SPATIAL_LAG)}
    for g in range(A_GROUPS):
        if g + SPATIAL_LAG < A_GROUPS:
            gates[g + SPATIAL_LAG] = gate_of(g + SPATIAL_LAG)
        spatial(g, gates.pop(g))

    def kv_of(b, h):
        for r in range(b * SGU_ROWS, (b + 1) * SGU_ROWS, CHUNK):
            hr = h[r - b * SGU_ROWS:r - b * SGU_ROWS + CHUNK]
            kv_ref[r:r + CHUNK, :] = jnp.dot(hr.astype(BF16), w_kv_ref[...],
                                             preferred_element_type=F32).astype(BF16)

    h_prev = None
    for b in range(n_blocks):
        rows = slice(b * SGU_ROWS, (b + 1) * SGU_ROWS)
        sub = jnp.dot(y_s[rows, :], w_out_ref[...], preferred_element_type=F32)
        if h_prev is not None:
            kv_of(b - 1, h_prev)
        project_v(xn_ref, b)
        h_prev = _layer_norm(alpha * x_ref[rows, :] + sub, pg_ref[...], pb_ref[...])
        h_ref[rows, :] = h_prev
    kv_of(n_blocks - 1, h_prev)


def _sgu_layer(x2, w_in, ln_g, ln_b, w_spatial, bsp_b, w_out, w_kv, pg, pb, *, alpha, tm):
    t_tokens, d_model = x2.shape
    aw = w_out.shape[0]
    gd = aw // A_GROUPS
    kvw = w_kv.shape[1]
    n_tiles = t_tokens // tm
    whole = pl.BlockSpec(memory_space=pltpu.VMEM)
    hbm = pl.BlockSpec(memory_space=pl.ANY)
    stage = lambda w: ((STAGE_SLOTS, _stage_rows(w), w.shape[1]), F32)
    scratch = [((tm, aw), F32), ((tm, LANES), F32), ((tm, LANES), F32), ((tm, aw), BF16),
               (w_in.shape, BF16), (w_out.shape, BF16), (w_kv.shape, BF16), stage(w_in), stage(w_out), stage(w_kv)]
    vmem = _vmem_limit(
        resident=[(a.shape, a.dtype) for a in (ln_g, ln_b, w_spatial, bsp_b, pg, pb)],
        pipelined=[((tm, d_model), F32)] * 3 + [((tm, kvw), BF16)],
        scratch=scratch,
        live_values=[((tm, gd), F32)] * (3 + SPATIAL_LAG) + [((tm, d_model), F32)] * 2)
    return pl.pallas_call(
        functools.partial(_sgu_kernel, alpha=alpha),
        grid=(n_tiles,),
        in_specs=[pl.BlockSpec((tm, d_model), lambda i: (i, 0)),
                  pl.BlockSpec((tm, d_model), lambda i: (jnp.minimum(i + 1, n_tiles - 1), 0)),
                  hbm, whole, whole, whole, whole, hbm, hbm, whole, whole],
        out_specs=[pl.BlockSpec((tm, d_model), lambda i: (i, 0)),
                   pl.BlockSpec((tm, kvw), lambda i: (i, 0))],
        out_shape=[jax.ShapeDtypeStruct((t_tokens, d_model), F32),
                   jax.ShapeDtypeStruct((t_tokens, kvw), BF16)],
        scratch_shapes=[pltpu.VMEM(shape, dtype) for shape, dtype in scratch]
        + [pltpu.SemaphoreType.DMA((3, STAGE_SLOTS))],
        compiler_params=pltpu.CompilerParams(dimension_semantics=("arbitrary",), vmem_limit_bytes=vmem),
        name="sgu_layer",
    )(x2, x2, w_in, ln_g, ln_b, w_spatial, bsp_b, w_out, w_kv, pg, pb)


def _swa_kernel(h_ref, hn_ref, kv_ref, kvp_ref, w_in_hbm, bias_ref, scale_ref, w_out_hbm,
                pg_ref, pb_ref, o_ref, q_s, g_s, y_s, s_s, w_in_ref, w_out_ref, st_in, st_out, sems,
                *, alpha, tiles_per_seq):
    tm, _ = h_ref.shape
    bw = y_s.shape[1]
    n_kv = bw // (HEAD_DIM * Q_PER_KV)
    pairs_per_kv = Q_PER_KV // 2
    i = pl.program_id(0)

    n_blocks = tm // PROJ_ROWS
    chunks_per_block = PROJ_ROWS // CHUNK

    def project_q(src_ref, b):
        rows = slice(b * PROJ_ROWS, (b + 1) * PROJ_ROWS)
        q_s[rows, :] = (jnp.dot(src_ref[rows, :].astype(BF16), w_in_ref[:, :bw], preferred_element_type=F32)
                        * (HEAD_DIM ** -0.5)).astype(BF16)

    def project_z(src_ref, b):
        rows = slice(b * PROJ_ROWS, (b + 1) * PROJ_ROWS)
        g_s[rows, :] = _silu(jnp.dot(src_ref[rows, :].astype(BF16), w_in_ref[:, bw:],
                                     preferred_element_type=F32))

    @pl.when(i == 0)
    def _():
        _cast_weight(w_in_hbm, w_in_ref, st_in, sems, 0)
        _cast_weight(w_out_hbm, w_out_ref, st_out, sems, 1)
        for b in range(n_blocks):
            project_q(h_ref, b)
            project_z(h_ref, b)

    as_bf16 = lambda a: a.astype(F32).astype(BF16)
    lo = as_bf16(lax.broadcasted_iota(jnp.int32, (2 * CHUNK, LANES), 1)) < HEAD_DIM
    key0 = as_bf16(lax.broadcasted_iota(jnp.int32, (2 * CHUNK, LANES), 0)) < 1
    zeros_k = jnp.zeros((2 * CHUNK, LANES), BF16)
    zeros_v = jnp.zeros((HEAD_DIM, 2 * CHUNK), BF16)
    zeros_d = jnp.zeros((DEN_ROWS, 2 * CHUNK), BF16)
    ones_d = jnp.ones((DEN_ROWS, 2 * CHUNK), BF16)
    first = (i % tiles_per_seq == 0).astype(jnp.int32)

    bands = {}

    def band_of(c):
        if c not in bands:
            rows = slice(c * CHUNK, (c + 1) * CHUNK)
            kv_prev = kvp_ref[...] if c == 0 else kv_ref[(c - 1) * CHUNK:c * CHUNK, :]
            band = jnp.concatenate([kv_prev, kv_ref[rows, :]], axis=0)
            kb, vb = band[:, :LANES], band[:, LANES:]
            kr = pltpu.roll(kb, HEAD_DIM, axis=1)
            v_t = jnp.where(key0, zeros_k, vb).astype(F32).T.astype(BF16)
            bands[c] = (kb, kr, v_t)
        return bands[c]

    def scores_of(c, kvh, slot):
        rows = slice(c * CHUNK, (c + 1) * CHUNK)
        kb, kr, v_t = band_of(c)
        variant = first if c == 0 else 0
        k_src, k_alt = (kb, kr) if kvh == 0 else (kr, kb)
        k_cat = jnp.concatenate([jnp.where(lo, k_src, zeros_k), jnp.where(lo, zeros_k, k_alt)], axis=0)
        vk = v_t[kvh * HEAD_DIM:(kvh + 1) * HEAD_DIM]
        v_lhs = jnp.concatenate(
            [jnp.concatenate([vk, zeros_v], axis=1), jnp.concatenate([zeros_v, vk], axis=1),
             jnp.concatenate([ones_d, zeros_d], axis=1), jnp.concatenate([zeros_d, ones_d], axis=1)],
            axis=0)
        pair0 = kvh * pairs_per_kv
        q_stack = jnp.concatenate(
            [q_s[rows, (pair0 + p) * LANES:(pair0 + p + 1) * LANES] for p in range(pairs_per_kv)], axis=0)
        s_s[slot] = lax.dot_general(k_cat, q_stack, (((1,), (1,)), ((), ())),
                                    preferred_element_type=F32)
        return rows, variant, kvh, slot, v_lhs

    def attend(rows, variant, kvh, slot, v_lhs):
        pair0 = kvh * pairs_per_kv
        e_cols = []
        for p in range(pairs_per_kv):
            qc = slice(p * CHUNK, (p + 1) * CHUNK)
            halves = []
            for e in range(2):
                kr_ = slice(e * 2 * CHUNK, (e + 1) * 2 * CHUNK)
                le = s_s[slot, kr_, qc] * scale_ref[variant, kr_, :] + bias_ref[variant, kvh, kr_, qc]
                halves.append(jnp.exp2(le - jnp.max(le, axis=0, keepdims=True)).astype(BF16))
            e_cols.append(jnp.concatenate(halves, axis=0))
        pv_t = jnp.dot(v_lhs, jnp.concatenate(e_cols, axis=1), preferred_element_type=F32)
        n_num = 2 * HEAD_DIM
        inv_e = 1.0 / pv_t[n_num:n_num + SUBLANES]
        inv_o = 1.0 / pv_t[n_num + DEN_ROWS:n_num + DEN_ROWS + SUBLANES]
        inv = jnp.concatenate([inv_e] * (HEAD_DIM // SUBLANES) + [inv_o] * (HEAD_DIM // SUBLANES), axis=0)
        o_t = pv_t[:n_num] * inv
        for p in range(pairs_per_kv):
            pc = slice((pair0 + p) * LANES, (pair0 + p + 1) * LANES)
            y_s[rows, pc] = (o_t[:, p * CHUNK:(p + 1) * CHUNK].T * g_s[rows, pc]).astype(BF16)

    block_units = [[(c, kvh) for c in range(b * chunks_per_block, (b + 1) * chunks_per_block)
                    for kvh in range(n_kv)] for b in range(n_blocks)]
    units_per_block = chunks_per_block * n_kv
    slot_of = lambda b, j: (b % 2) * units_per_block + j
    scored = [scores_of(*u, slot_of(0, j)) for j, u in enumerate(block_units[0])]
    for b in range(n_blocks):
        rows = slice(b * PROJ_ROWS, (b + 1) * PROJ_ROWS)
        scored_next = []
        for j, unit in enumerate(scored):
            attend(*unit)
            if b + 1 < n_blocks:
                scored_next.append(scores_of(*block_units[b + 1][j], slot_of(b + 1, j)))
        scored = scored_next
        project_q(hn_ref, b)
        sub = jnp.dot(y_s[rows, :], w_out_ref[...], preferred_element_type=F32)
        project_z(hn_ref, b)
        o_ref[rows, :] = _layer_norm(alpha * h_ref[rows, :] + sub, pg_ref[...], pb_ref[...])


def _swa_layer(h1, kv, w_in, bias_tab, scale_tab, w_out, pg, pb, *, alpha, tm, seq):
    t_tokens, d_model = h1.shape
    bw = w_out.shape[0]
    kvw2 = kv.shape[1]
    cpt = tm // CHUNK
    n_tiles = t_tokens // tm
    n_units = 2 * (PROJ_ROWS // CHUNK) * (bw // (HEAD_DIM * Q_PER_KV))
    whole = pl.BlockSpec(memory_space=pltpu.VMEM)
    hbm = pl.BlockSpec(memory_space=pl.ANY)
    stage = lambda w: ((STAGE_SLOTS, _stage_rows(w), w.shape[1]), F32)
    scratch = [((tm, bw), BF16), ((tm, bw), F32), ((tm, bw), BF16), ((n_units, 4 * CHUNK, 4 * CHUNK), F32),
               (w_in.shape, BF16), (w_out.shape, BF16), stage(w_in), stage(w_out)]
    vmem = _vmem_limit(
        resident=[(a.shape, a.dtype) for a in (bias_tab, scale_tab, pg, pb)],
        pipelined=[((tm, d_model), F32)] * 3 + [((tm, kvw2), BF16), ((CHUNK, kvw2), BF16)],
        scratch=scratch,
        live_values=[((PROJ_ROWS, bw), F32)] * 2 + [((PROJ_ROWS, d_model), F32)] * 2
        + [((4 * CHUNK, 4 * CHUNK), BF16)])
    return pl.pallas_call(
        functools.partial(_swa_kernel, alpha=alpha, tiles_per_seq=seq // tm),
        grid=(n_tiles,),
        in_specs=[pl.BlockSpec((tm, d_model), lambda i: (i, 0)),
                  pl.BlockSpec((tm, d_model), lambda i: (jnp.minimum(i + 1, n_tiles - 1), 0)),
                  pl.BlockSpec((tm, kvw2), lambda i: (i, 0)),
                  pl.BlockSpec((CHUNK, kvw2), lambda i: (jnp.maximum(i * cpt - 1, 0), 0)),
                  hbm, whole, whole, hbm, whole, whole],
        out_specs=pl.BlockSpec((tm, d_model), lambda i: (i, 0)),
        out_shape=jax.ShapeDtypeStruct((t_tokens, d_model), F32),
        scratch_shapes=[pltpu.VMEM(shape, dtype) for shape, dtype in scratch]
        + [pltpu.SemaphoreType.DMA((2, STAGE_SLOTS))],
        compiler_params=pltpu.CompilerParams(dimension_semantics=("arbitrary",), vmem_limit_bytes=vmem),
        name="swa_layer",
    )(h1, h1, kv, kv, w_in, bias_tab, scale_tab, w_out, pg, pb)


def kernel(x, w_in_a, sgu_ln_g, sgu_ln_b, w_spatial, b_spatial, w_out_a, w_kv, w_in_b, attn_sinks,
           rel_bias, w_out_b, post_ln_g, post_ln_b):
    bsz, seq, d_model = x.shape
    depth = post_ln_g.shape[0]
    assert w_in_a.shape[0] == 1 and w_in_b.shape[0] == 1 and depth == 2
    assert seq % CHUNK == 0 and w_spatial.shape[1:] == (A_GROUPS, CHUNK, CHUNK)
    assert w_kv.shape[1] == 2 * LANES and rel_bias.shape == (REL_BUCKETS, w_out_b.shape[1] // HEAD_DIM)
    alpha = (2.0 * depth) ** 0.25
    aw = w_out_a.shape[1]
    gd = aw // A_GROUPS
    tm = TOKEN_TILE
    assert seq % tm == 0

    x2 = x.reshape(bsz * seq, d_model)
    row = lambda a: a.reshape(1, -1)
    bsp_b = jnp.broadcast_to(b_spatial[0][:, :, None], (A_GROUPS, CHUNK, gd))
    h1, kv = _sgu_layer(x2, w_in_a[0], row(sgu_ln_g[0]), row(sgu_ln_b[0]), w_spatial[0], bsp_b,
                        w_out_a[0], w_kv, row(post_ln_g[0]), row(post_ln_b[0]),
                        alpha=alpha, tm=tm)
    bias_tab = _rel_bias_table(rel_bias, attn_sinks[0])
    out = _swa_layer(h1, kv, w_in_b[0], bias_tab, jnp.asarray(_band_scale_table()),
                     w_out_b[0], row(post_ln_g[1]), row(post_ln_b[1]),
                     alpha=alpha, tm=tm, seq=seq)
    return out.reshape(bsz, seq, d_model)
```

```python
import functools
import math

import numpy as np
import jax
import jax.numpy as jnp
from jax import lax
from jax.experimental import pallas as pl
from jax.experimental.pallas import tpu as pltpu

F32 = jnp.float32
BF16 = jnp.bfloat16

CHUNK = 128
A_GROUPS = 8
HEAD_DIM = 64
Q_PER_KV = 8
REL_BUCKETS = 32
REL_MAX_DIST = 128
LN_EPS = 1e-5
NEG_INF = -1e30
LOG2E = math.log2(math.e)

LANES = 128
SUBLANES = 8
PROJ_ROWS = 2 * CHUNK
TOKEN_TILE = 4 * CHUNK
SGU_ROWS = TOKEN_TILE
DEN_ROWS = 16
SPATIAL_LAG = 1
V7X_VMEM_BYTES = 64 * 1024 * 1024
COMPILER_TEMP_BYTES = 8 * 1024 * 1024
STAGE_BYTES = 256 * 1024
STAGE_SLOTS = 8


def _bucket_lower_bounds():
    max_exact = REL_BUCKETS // 2
    d = np.arange(REL_MAX_DIST, dtype=np.int32)
    df = np.maximum(d, 1).astype(np.float32)
    large = max_exact + (np.log(df / np.float32(max_exact)) / np.float32(math.log(REL_MAX_DIST / max_exact))
                         * np.float32(REL_BUCKETS - max_exact)).astype(np.int32)
    large = np.minimum(large, REL_BUCKETS - 1)
    bucket = np.where(d < max_exact, d, large)
    assert np.all(np.diff(bucket) >= 0) and bucket[0] == 0
    return [int(np.argmax(bucket >= b)) if np.any(bucket >= b) else REL_MAX_DIST for b in range(REL_BUCKETS)]


def _band_scale_table():
    j = np.arange(2 * CHUNK)[:, None]
    t = np.arange(CHUNK)[None, :]
    d = t + CHUNK - j
    in_window = (d >= 0) & (d < CHUNK)
    first = in_window & (j >= CHUNK)
    tab = np.stack([np.tile(in_window, (2, 1)), np.tile(first, (2, 1))]).astype(np.float32)
    return tab * np.float32(LOG2E)


def _layer_norm(x, g, b):
    mu = jnp.mean(x, axis=-1, keepdims=True)
    d = x - mu
    var = jnp.mean(d * d, axis=-1, keepdims=True)
    return d * lax.rsqrt(var + LN_EPS) * g + b


def _silu(z):
    half = 0.5 * z
    return half + half * jnp.tanh(half)


def _tile_bytes(shape, dtype):
    itemsize = jnp.dtype(dtype).itemsize
    sublanes = SUBLANES * 4 // itemsize
    *lead, rows, cols = (1,) + tuple(shape)
    return math.prod(lead) * (-(-rows // sublanes) * sublanes) * (-(-cols // LANES) * LANES) * itemsize


def _vmem_limit(resident, pipelined, scratch, live_values):
    size = lambda entries: sum(_tile_bytes(shape, dtype) for shape, dtype in entries)
    total = size(resident) + 2 * size(pipelined) + size(scratch) + size(live_values) + COMPILER_TEMP_BYTES
    assert total <= V7X_VMEM_BYTES, total
    return total


def _stage_rows(w_hbm):
    k, n = w_hbm.shape
    rows = SUBLANES
    while 2 * rows * n * jnp.dtype(F32).itemsize <= STAGE_BYTES and k % (2 * rows) == 0:
        rows *= 2
    assert k % rows == 0
    return rows


def _cast_weight(w_hbm, w_s, stage, sems, k):
    rows = stage.shape[1]
    n = w_hbm.shape[0] // rows

    def copy(c):
        slot = c % STAGE_SLOTS
        return pltpu.make_async_copy(w_hbm.at[pl.ds(c * rows, rows), :], stage.at[slot], sems.at[k, slot])

    for c in range(min(STAGE_SLOTS - 1, n)):
        copy(c).start()
    for c in range(n):
        if c + STAGE_SLOTS - 1 < n:
            copy(c + STAGE_SLOTS - 1).start()
        copy(c).wait()
        w_s[c * rows:(c + 1) * rows, :] = stage[c % STAGE_SLOTS].astype(BF16)


def _rel_bias_kernel(rb_ref, sink_ref, out_ref, *, lower_bounds, pairs_per_kv):
    pair = pl.program_id(0) * pairs_per_kv + pl.program_id(1)
    j = lax.broadcasted_iota(jnp.int32, (2 * CHUNK, CHUNK), 0)
    t = lax.broadcasted_iota(jnp.int32, (2 * CHUNK, CHUNK), 1)
    dist = t + CHUNK - j
    in_window = (dist >= 0) & (dist < CHUNK)
    d = jnp.clip(dist, 0, REL_MAX_DIST - 1)
    for e in range(2):
        h = 2 * pair + e
        acc = jnp.full((2 * CHUNK, CHUNK), rb_ref[0, h], F32)
        for b in range(1, REL_BUCKETS):
            if lower_bounds[b] < REL_MAX_DIST:
                acc = jnp.where(d >= lower_bounds[b], rb_ref[b, h], acc)
        hidden = jnp.where(j == 0, sink_ref[h], NEG_INF)
        for variant in range(2):
            visible = in_window & (j >= variant * CHUNK)
            out_ref[variant, 0, e * 2 * CHUNK:(e + 1) * 2 * CHUNK, :] = jnp.where(visible, acc, hidden) * LOG2E


def _rel_bias_table(rel_bias, sinks):
    n_heads = rel_bias.shape[1]
    n_kv = n_heads // Q_PER_KV
    pairs_per_kv = Q_PER_KV // 2
    smem = pl.BlockSpec(memory_space=pltpu.SMEM)
    return pl.pallas_call(
        functools.partial(_rel_bias_kernel, lower_bounds=_bucket_lower_bounds(), pairs_per_kv=pairs_per_kv),
        grid=(n_kv, pairs_per_kv),
        in_specs=[smem, smem],
        out_specs=pl.BlockSpec((2, 1, 4 * CHUNK, CHUNK), lambda k, p: (0, k, 0, p)),
        out_shape=jax.ShapeDtypeStruct((2, n_kv, 4 * CHUNK, pairs_per_kv * CHUNK), F32),
        name="rel_bias_table",
    )(rel_bias, sinks)


def _sgu_kernel(x_ref, xn_ref, w_in_hbm, lng_ref, lnb_ref, ws_ref, bsp_ref, w_out_hbm, w_kv_hbm,
                pg_ref, pb_ref, h_ref, kv_ref, v_s, mu_s, rstd_s, y_s, w_in_ref, w_out_ref, w_kv_ref,
                st_in, st_out, st_kv, sems, *, alpha):
    tm, _ = x_ref.shape
    aw = v_s.shape[1]
    gd = aw // A_GROUPS
    n_chunks = tm // CHUNK
    n_blocks = tm // SGU_ROWS

    def project_v(src_ref, b):
        rows = slice(b * SGU_ROWS, (b + 1) * SGU_ROWS)
        v = jnp.dot(src_ref[rows, :].astype(BF16), w_in_ref[:, aw:2 * aw], preferred_element_type=F32)
        v_s[rows, :] = v
        mu = jnp.mean(v, axis=-1, keepdims=True)
        var = jnp.maximum(jnp.mean(v * v, axis=-1, keepdims=True) - mu * mu, 0.0)
        mu_s[rows, :] = jnp.broadcast_to(mu, (SGU_ROWS, LANES))
        rstd_s[rows, :] = jnp.broadcast_to(lax.rsqrt(var + LN_EPS), (SGU_ROWS, LANES))

    @pl.when(pl.program_id(0) == 0)
    def _():
        _cast_weight(w_in_hbm, w_in_ref, st_in, sems, 0)
        _cast_weight(w_out_hbm, w_out_ref, st_out, sems, 1)
        _cast_weight(w_kv_hbm, w_kv_ref, st_kv, sems, 2)
        for b in range(n_blocks):
            project_v(x_ref, b)

    xb = x_ref[...].astype(BF16)
    lane_tiles = gd // LANES
    mu = jnp.concatenate([mu_s[...]] * lane_tiles, axis=1)
    rstd = jnp.concatenate([rstd_s[...]] * lane_tiles, axis=1)

    row = lax.broadcasted_iota(jnp.int32, (CHUNK, CHUNK), 0)
    col = lax.broadcasted_iota(jnp.int32, (CHUNK, CHUNK), 1)
    tril = col <= row

    def gate_of(g):
        u = jnp.dot(xb, w_in_ref[:, g * gd:(g + 1) * gd], preferred_element_type=F32)
        z = jnp.dot(xb, w_in_ref[:, 2 * aw + g * gd:2 * aw + (g + 1) * gd], preferred_element_type=F32)
        return u * _silu(z)

    def spatial(g, gate):
        cols = slice(g * gd, (g + 1) * gd)
        ws = jnp.where(tril, ws_ref[g], 0.0).astype(BF16)
        vn = ((v_s[:, cols] - mu) * rstd * lng_ref[:, cols] + lnb_ref[:, cols]).astype(BF16)
        for c in range(n_chunks):
            rows = slice(c * CHUNK, (c + 1) * CHUNK)
            s = jnp.dot(ws, vn[rows], preferred_element_type=F32) + bsp_ref[g]
            y_s[rows, cols] = (gate[rows] * s).astype(BF16)

    gates = {g: gate_of(g) for g in range(SPATIAL_LAG)}
    for g in range(A_GROUPS):
        if g + SPATIAL_LAG < A_GROUPS:
            gates[g + SPATIAL_LAG] = gate_of(g + SPATIAL_LAG)
        spatial(g, gates.pop(g))

    def kv_of(b, h):
        for r in range(b * SGU_ROWS, (b + 1) * SGU_ROWS, CHUNK):
            hr = h[r - b * SGU_ROWS:r - b * SGU_ROWS + CHUNK]
            kv_ref[r:r + CHUNK, :] = jnp.dot(hr.astype(BF16), w_kv_ref[...],
                                             preferred_element_type=F32).astype(BF16)

    h_prev = None
    for b in range(n_blocks):
        rows = slice(b * SGU_ROWS, (b + 1) * SGU_ROWS)
        sub = jnp.dot(y_s[rows, :], w_out_ref[...], preferred_element_type=F32)
        if h_prev is not None:
            kv_of(b - 1, h_prev)
        project_v(xn_ref, b)
        h_prev = _layer_norm(alpha * x_ref[rows, :] + sub, pg_ref[...], pb_ref[...])
        h_ref[rows, :] = h_prev
    kv_of(n_blocks - 1, h_prev)


def _sgu_layer(x2, w_in, ln_g, ln_b, w_spatial, bsp_b, w_out, w_kv, pg, pb, *, alpha, tm):
    t_tokens, d_model = x2.shape
    aw = w_out.shape[0]
    gd = aw // A_GROUPS
    kvw = w_kv.shape[1]
    n_tiles = t_tokens // tm
    whole = pl.BlockSpec(memory_space=pltpu.VMEM)
    hbm = pl.BlockSpec(memory_space=pl.ANY)
    stage = lambda w: ((STAGE_SLOTS, _stage_rows(w), w.shape[1]), F32)
    scratch = [((tm, aw), F32), ((tm, LANES), F32), ((tm, LANES), F32), ((tm, aw), BF16),
               (w_in.shape, BF16), (w_out.shape, BF16), (w_kv.shape, BF16), stage(w_in), stage(w_out), stage(w_kv)]
    vmem = _vmem_limit(
        resident=[(a.shape, a.dtype) for a in (ln_g, ln_b, w_spatial, bsp_b, pg, pb)],
        pipelined=[((tm, d_model), F32)] * 3 + [((tm, kvw), BF16)],
        scratch=scratch,
        live_values=[((tm, gd), F32)] * (3 + SPATIAL_LAG) + [((tm, d_model), F32)] * 2)
    return pl.pallas_call(
        functools.partial(_sgu_kernel, alpha=alpha),
        grid=(n_tiles,),
        in_specs=[pl.BlockSpec((tm, d_model), lambda i: (i, 0)),
                  pl.BlockSpec((tm, d_model), lambda i: (jnp.minimum(i + 1, n_tiles - 1), 0)),
                  hbm, whole, whole, whole, whole, hbm, hbm, whole, whole],
        out_specs=[pl.BlockSpec((tm, d_model), lambda i: (i, 0)),
                   pl.BlockSpec((tm, kvw), lambda i: (i, 0))],
        out_shape=[jax.ShapeDtypeStruct((t_tokens, d_model), F32),
                   jax.ShapeDtypeStruct((t_tokens, kvw), BF16)],
        scratch_shapes=[pltpu.VMEM(shape, dtype) for shape, dtype in scratch]
        + [pltpu.SemaphoreType.DMA((3, STAGE_SLOTS))],
        compiler_params=pltpu.CompilerParams(dimension_semantics=("arbitrary",), vmem_limit_bytes=vmem),
        name="sgu_layer",
    )(x2, x2, w_in, ln_g, ln_b, w_spatial, bsp_b, w_out, w_kv, pg, pb)


def _swa_kernel(h_ref, hn_ref, kv_ref, kvp_ref, w_in_hbm, bias_ref, scale_ref, w_out_hbm,
                pg_ref, pb_ref, o_ref, q_s, g_s, y_s, s_s, w_in_ref, w_out_ref, st_in, st_out, sems,
                *, alpha, tiles_per_seq):
    tm, _ = h_ref.shape
    bw = y_s.shape[1]
    n_kv = bw // (HEAD_DIM * Q_PER_KV)
    pairs_per_kv = Q_PER_KV // 2
    i = pl.program_id(0)

    n_blocks = tm // PROJ_ROWS
    chunks_per_block = PROJ_ROWS // CHUNK

    def project_q(src_ref, b):
        rows = slice(b * PROJ_ROWS, (b + 1) * PROJ_ROWS)
        q = (jnp.dot(src_ref[rows, :].astype(BF16), w_in_ref[:, :bw], preferred_element_type=F32)
             * (HEAD_DIM ** -0.5)).astype(BF16)
        for p in range(bw // LANES):
            q_s[p, rows, :] = q[:, p * LANES:(p + 1) * LANES]

    def project_z(src_ref, b):
        rows = slice(b * PROJ_ROWS, (b + 1) * PROJ_ROWS)
        g_s[rows, :] = _silu(jnp.dot(src_ref[rows, :].astype(BF16), w_in_ref[:, bw:],
                                     preferred_element_type=F32))

    @pl.when(i == 0)
    def _():
        _cast_weight(w_in_hbm, w_in_ref, st_in, sems, 0)
        _cast_weight(w_out_hbm, w_out_ref, st_out, sems, 1)
        for b in range(n_blocks):
            project_q(h_ref, b)
            project_z(h_ref, b)

    as_bf16 = lambda a: a.astype(F32).astype(BF16)
    lo = as_bf16(lax.broadcasted_iota(jnp.int32, (2 * CHUNK, LANES), 1)) < HEAD_DIM
    key0 = as_bf16(lax.broadcasted_iota(jnp.int32, (2 * CHUNK, LANES), 0)) < 1
    zeros_k = jnp.zeros((2 * CHUNK, LANES), BF16)
    zeros_v = jnp.zeros((HEAD_DIM, 2 * CHUNK), BF16)
    zeros_d = jnp.zeros((DEN_ROWS, 2 * CHUNK), BF16)
    ones_d = jnp.ones((DEN_ROWS, 2 * CHUNK), BF16)
    first = (i % tiles_per_seq == 0).astype(jnp.int32)

    bands = {}

    def band_of(c):
        if c not in bands:
            rows = slice(c * CHUNK, (c + 1) * CHUNK)
            kv_prev = kvp_ref[...] if c == 0 else kv_ref[(c - 1) * CHUNK:c * CHUNK, :]
            band = jnp.concatenate([kv_prev, kv_ref[rows, :]], axis=0)
            kb, vb = band[:, :LANES], band[:, LANES:]
            kr = pltpu.roll(kb, HEAD_DIM, axis=1)
            v_t = jnp.where(key0, zeros_k, vb).astype(F32).T.astype(BF16)
            bands[c] = (kb, kr, v_t)
        return bands[c]

    def scores_of(c, kvh, slot):
        rows = slice(c * CHUNK, (c + 1) * CHUNK)
        kb, kr, v_t = band_of(c)
        variant = first if c == 0 else 0
        k_src, k_alt = (kb, kr) if kvh == 0 else (kr, kb)
        k_cat = jnp.concatenate([jnp.where(lo, k_src, zeros_k), jnp.where(lo, zeros_k, k_alt)], axis=0)
        vk = v_t[kvh * HEAD_DIM:(kvh + 1) * HEAD_DIM]
        v_lhs = jnp.concatenate(
            [jnp.concatenate([vk, zeros_v], axis=1), jnp.concatenate([zeros_v, vk], axis=1),
             jnp.concatenate([ones_d, zeros_d], axis=1), jnp.concatenate([zeros_d, ones_d], axis=1)],
            axis=0)
        pair0 = kvh * pairs_per_kv
        q_stack = jnp.concatenate(
            [q_s[pair0 + p, rows, :] for p in range(pairs_per_kv)], axis=0)
        s_s[slot] = lax.dot_general(k_cat, q_stack, (((1,), (1,)), ((), ())),
                                    preferred_element_type=F32)
        return rows, variant, kvh, slot, v_lhs

    def attend(rows, variant, kvh, slot, v_lhs):
        pair0 = kvh * pairs_per_kv
        e_cols = []
        for p in range(pairs_per_kv):
            qc = slice(p * CHUNK, (p + 1) * CHUNK)
            halves = []
            for e in range(2):
                kr_ = slice(e * 2 * CHUNK, (e + 1) * 2 * CHUNK)
                le = s_s[slot, kr_, qc] * scale_ref[variant, kr_, :] + bias_ref[variant, kvh, kr_, qc]
                halves.append(jnp.exp2(le - jnp.max(le, axis=0, keepdims=True)).astype(BF16))
            e_cols.append(jnp.concatenate(halves, axis=0))
        pv_t = jnp.dot(v_lhs, jnp.concatenate(e_cols, axis=1), preferred_element_type=F32)
        n_num = 2 * HEAD_DIM
        inv_e = 1.0 / pv_t[n_num:n_num + SUBLANES]
        inv_o = 1.0 / pv_t[n_num + DEN_ROWS:n_num + DEN_ROWS + SUBLANES]
        inv = jnp.concatenate([inv_e] * (HEAD_DIM // SUBLANES) + [inv_o] * (HEAD_DIM // SUBLANES), axis=0)
        o_t = pv_t[:n_num] * inv
        for p in range(pairs_per_kv):
            pc = slice((pair0 + p) * LANES, (pair0 + p + 1) * LANES)
            y_s[rows, pc] = (o_t[:, p * CHUNK:(p + 1) * CHUNK].T * g_s[rows, pc]).astype(BF16)

    block_units = [[(c, kvh) for c in range(b * chunks_per_block, (b + 1) * chunks_per_block)
                    for kvh in range(n_kv)] for b in range(n_blocks)]
    units_per_block = chunks_per_block * n_kv
    slot_of = lambda b, j: (b % 2) * units_per_block + j
    scored = [scores_of(*u, slot_of(0, j)) for j, u in enumerate(block_units[0])]
    for b in range(n_blocks):
        rows = slice(b * PROJ_ROWS, (b + 1) * PROJ_ROWS)
        scored_next = []
        for j, unit in enumerate(scored):
            attend(*unit)
            if b + 1 < n_blocks:
                scored_next.append(scores_of(*block_units[b + 1][j], slot_of(b + 1, j)))
        scored = scored_next
        project_q(hn_ref, b)
        sub = jnp.dot(y_s[rows, :], w_out_ref[...], preferred_element_type=F32)
        project_z(hn_ref, b)
        o_ref[rows, :] = _layer_norm(alpha * h_ref[rows, :] + sub, pg_ref[...], pb_ref[...])


def _swa_layer(h1, kv, w_in, bias_tab, scale_tab, w_out, pg, pb, *, alpha, tm, seq):
    t_tokens, d_model = h1.shape
    bw = w_out.shape[0]
    kvw2 = kv.shape[1]
    cpt = tm // CHUNK
    n_tiles = t_tokens // tm
    n_units = 2 * (PROJ_ROWS // CHUNK) * (bw // (HEAD_DIM * Q_PER_KV))
    whole = pl.BlockSpec(memory_space=pltpu.VMEM)
    hbm = pl.BlockSpec(memory_space=pl.ANY)
    stage = lambda w: ((STAGE_SLOTS, _stage_rows(w), w.shape[1]), F32)
    scratch = [((bw // LANES, tm, LANES), BF16), ((tm, bw), F32), ((tm, bw), BF16), ((n_units, 4 * CHUNK, 4 * CHUNK), F32),
               (w_in.shape, BF16), (w_out.shape, BF16), stage(w_in), stage(w_out)]
    vmem = _vmem_limit(
        resident=[(a.shape, a.dtype) for a in (bias_tab, scale_tab, pg, pb)],
        pipelined=[((tm, d_model), F32)] * 3 + [((tm, kvw2), BF16), ((CHUNK, kvw2), BF16)],
        scratch=scratch,
        live_values=[((PROJ_ROWS, bw), F32)] * 2 + [((PROJ_ROWS, d_model), F32)] * 2
        + [((4 * CHUNK, 4 * CHUNK), BF16)])
    return pl.pallas_call(
        functools.partial(_swa_kernel, alpha=alpha, tiles_per_seq=seq // tm),
        grid=(n_tiles,),
        in_specs=[pl.BlockSpec((tm, d_model), lambda i: (i, 0)),
                  pl.BlockSpec((tm, d_model), lambda i: (jnp.minimum(i + 1, n_tiles - 1), 0)),
                  pl.BlockSpec((tm, kvw2), lambda i: (i, 0)),
                  pl.BlockSpec((CHUNK, kvw2), lambda i: (jnp.maximum(i * cpt - 1, 0), 0)),
                  hbm, whole, whole, hbm, whole, whole],
        out_specs=pl.BlockSpec((tm, d_model), lambda i: (i, 0)),
        out_shape=jax.ShapeDtypeStruct((t_tokens, d_model), F32),
        scratch_shapes=[pltpu.VMEM(shape, dtype) for shape, dtype in scratch]
        + [pltpu.SemaphoreType.DMA((2, STAGE_SLOTS))],
        compiler_params=pltpu.CompilerParams(dimension_semantics=("arbitrary",), vmem_limit_bytes=vmem),
        name="swa_layer",
    )(h1, h1, kv, kv, w_in, bias_tab, scale_tab, w_out, pg, pb)


def kernel(x, w_in_a, sgu_ln_g, sgu_ln_b, w_spatial, b_spatial, w_out_a, w_kv, w_in_b, attn_sinks,
           rel_bias, w_out_b, post_ln_g, post_ln_b):
    bsz, seq, d_model = x.shape
    depth = post_ln_g.shape[0]
    assert w_in_a.shape[0] == 1 and w_in_b.shape[0] == 1 and depth == 2
    assert seq % CHUNK == 0 and w_spatial.shape[1:] == (A_GROUPS, CHUNK, CHUNK)
    assert w_kv.shape[1] == 2 * LANES and rel_bias.shape == (REL_BUCKETS, w_out_b.shape[1] // HEAD_DIM)
    alpha = (2.0 * depth) ** 0.25
    aw = w_out_a.shape[1]
    gd = aw // A_GROUPS
    tm = TOKEN_TILE
    assert seq % tm == 0

    x2 = x.reshape(bsz * seq, d_model)
    row = lambda a: a.reshape(1, -1)
    bsp_b = jnp.broadcast_to(b_spatial[0][:, :, None], (A_GROUPS, CHUNK, gd))
    h1, kv = _sgu_layer(x2, w_in_a[0], row(sgu_ln_g[0]), row(sgu_ln_b[0]), w_spatial[0], bsp_b,
                        w_out_a[0], w_kv, row(post_ln_g[0]), row(post_ln_b[0]),
                        alpha=alpha, tm=tm)
    bias_tab = _rel_bias_table(rel_bias, attn_sinks[0])
    out = _swa_layer(h1, kv, w_in_b[0], bias_tab, jnp.asarray(_band_scale_table()),
                     w_out_b[0], row(post_ln_g[1]), row(post_ln_b[1]),
                     alpha=alpha, tm=tm, seq=seq)
    return out.reshape(bsz, seq, d_model)
```

```python
import functools
import math

import numpy as np
import jax
import jax.numpy as jnp
from jax import lax
from jax.experimental import pallas as pl
from jax.experimental.pallas import tpu as pltpu

F32 = jnp.float32
BF16 = jnp.bfloat16

CHUNK = 128
A_GROUPS = 8
HEAD_DIM = 64
Q_PER_KV = 8
REL_BUCKETS = 32
REL_MAX_DIST = 128
LN_EPS = 1e-5
NEG_INF = -1e30
LOG2E = math.log2(math.e)

LANES = 128
SUBLANES = 8
PROJ_ROWS = 2 * CHUNK
TOKEN_TILE = 4 * CHUNK
SGU_ROWS = TOKEN_TILE
WEIGHT_COLS = 2 * LANES
DEN_ROWS = 16
SPATIAL_LAG = 1
V7X_VMEM_BYTES = 64 * 1024 * 1024
COMPILER_TEMP_BYTES = 8 * 1024 * 1024
STAGE_BYTES = 256 * 1024
STAGE_SLOTS = 8


def _bucket_lower_bounds():
    max_exact = REL_BUCKETS // 2
    d = np.arange(REL_MAX_DIST, dtype=np.int32)
    df = np.maximum(d, 1).astype(np.float32)
    large = max_exact + (np.log(df / np.float32(max_exact)) / np.float32(math.log(REL_MAX_DIST / max_exact))
                         * np.float32(REL_BUCKETS - max_exact)).astype(np.int32)
    large = np.minimum(large, REL_BUCKETS - 1)
    bucket = np.where(d < max_exact, d, large)
    assert np.all(np.diff(bucket) >= 0) and bucket[0] == 0
    return [int(np.argmax(bucket >= b)) if np.any(bucket >= b) else REL_MAX_DIST for b in range(REL_BUCKETS)]


def _band_scale_table():
    j = np.arange(2 * CHUNK)[:, None]
    t = np.arange(CHUNK)[None, :]
    d = t + CHUNK - j
    in_window = (d >= 0) & (d < CHUNK)
    first = in_window & (j >= CHUNK)
    tab = np.stack([np.tile(in_window, (2, 1)), np.tile(first, (2, 1))]).astype(np.float32)
    return tab * np.float32(LOG2E)


def _layer_norm(x, g, b):
    mu = jnp.mean(x, axis=-1, keepdims=True)
    d = x - mu
    var = jnp.mean(d * d, axis=-1, keepdims=True)
    return d * lax.rsqrt(var + LN_EPS) * g + b


def _silu(z):
    half = 0.5 * z
    return half + half * jnp.tanh(half)


def _blocked_dot(lhs, w_ref, first, count):
    return [jnp.dot(lhs, w_ref[first + j], preferred_element_type=F32) for j in range(count)]


def _tile_bytes(shape, dtype):
    itemsize = jnp.dtype(dtype).itemsize
    sublanes = SUBLANES * 4 // itemsize
    *lead, rows, cols = (1,) + tuple(shape)
    return math.prod(lead) * (-(-rows // sublanes) * sublanes) * (-(-cols // LANES) * LANES) * itemsize


def _vmem_limit(resident, pipelined, scratch, live_values):
    size = lambda entries: sum(_tile_bytes(shape, dtype) for shape, dtype in entries)
    total = size(resident) + 2 * size(pipelined) + size(scratch) + size(live_values) + COMPILER_TEMP_BYTES
    assert total <= V7X_VMEM_BYTES, total
    return total


def _stage_rows(w_hbm):
    k, n = w_hbm.shape
    rows = SUBLANES
    while 2 * rows * n * jnp.dtype(F32).itemsize <= STAGE_BYTES and k % (2 * rows) == 0:
        rows *= 2
    assert k % rows == 0
    return rows


def _cast_weight(w_hbm, w_s, stage, sems, k):
    rows = stage.shape[1]
    n = w_hbm.shape[0] // rows

    def copy(c):
        slot = c % STAGE_SLOTS
        return pltpu.make_async_copy(w_hbm.at[pl.ds(c * rows, rows), :], stage.at[slot], sems.at[k, slot])

    for c in range(min(STAGE_SLOTS - 1, n)):
        copy(c).start()
    for c in range(n):
        if c + STAGE_SLOTS - 1 < n:
            copy(c + STAGE_SLOTS - 1).start()
        copy(c).wait()
        chunk = stage[c % STAGE_SLOTS].astype(BF16)
        for j in range(w_s.shape[0]):
            w_s[j, c * rows:(c + 1) * rows, :] = chunk[:, j * WEIGHT_COLS:(j + 1) * WEIGHT_COLS]


def _rel_bias_kernel(rb_ref, sink_ref, out_ref, *, lower_bounds, pairs_per_kv):
    pair = pl.program_id(0) * pairs_per_kv + pl.program_id(1)
    j = lax.broadcasted_iota(jnp.int32, (2 * CHUNK, CHUNK), 0)
    t = lax.broadcasted_iota(jnp.int32, (2 * CHUNK, CHUNK), 1)
    dist = t + CHUNK - j
    in_window = (dist >= 0) & (dist < CHUNK)
    d = jnp.clip(dist, 0, REL_MAX_DIST - 1)
    for e in range(2):
        h = 2 * pair + e
        acc = jnp.full((2 * CHUNK, CHUNK), rb_ref[0, h], F32)
        for b in range(1, REL_BUCKETS):
            if lower_bounds[b] < REL_MAX_DIST:
                acc = jnp.where(d >= lower_bounds[b], rb_ref[b, h], acc)
        hidden = jnp.where(j == 0, sink_ref[h], NEG_INF)
        for variant in range(2):
            visible = in_window & (j >= variant * CHUNK)
            out_ref[variant, 0, e * 2 * CHUNK:(e + 1) * 2 * CHUNK, :] = jnp.where(visible, acc, hidden) * LOG2E


def _rel_bias_table(rel_bias, sinks):
    n_heads = rel_bias.shape[1]
    n_kv = n_heads // Q_PER_KV
    pairs_per_kv = Q_PER_KV // 2
    smem = pl.BlockSpec(memory_space=pltpu.SMEM)
    return pl.pallas_call(
        functools.partial(_rel_bias_kernel, lower_bounds=_bucket_lower_bounds(), pairs_per_kv=pairs_per_kv),
        grid=(n_kv, pairs_per_kv),
        in_specs=[smem, smem],
        out_specs=pl.BlockSpec((2, 1, 4 * CHUNK, CHUNK), lambda k, p: (0, k, 0, p)),
        out_shape=jax.ShapeDtypeStruct((2, n_kv, 4 * CHUNK, pairs_per_kv * CHUNK), F32),
        name="rel_bias_table",
    )(rel_bias, sinks)


def _sgu_kernel(x_ref, xn_ref, w_in_hbm, lng_ref, lnb_ref, ws_ref, bsp_ref, w_out_hbm, w_kv_hbm,
                pg_ref, pb_ref, h_ref, kv_ref, v_s, mu_s, rstd_s, y_s, w_in_ref, w_out_ref, w_kv_ref,
                st_in, st_out, st_kv, sems, *, alpha):
    tm, _ = x_ref.shape
    aw = v_s.shape[1]
    gd = aw // A_GROUPS
    n_chunks = tm // CHUNK
    n_blocks = tm // SGU_ROWS

    def project_v(src_ref, b):
        rows = slice(b * SGU_ROWS, (b + 1) * SGU_ROWS)
        blocks = aw // WEIGHT_COLS
        v = jnp.concatenate(_blocked_dot(src_ref[rows, :].astype(BF16), w_in_ref, blocks, blocks), axis=1)
        v_s[rows, :] = v
        mu = jnp.mean(v, axis=-1, keepdims=True)
        var = jnp.maximum(jnp.mean(v * v, axis=-1, keepdims=True) - mu * mu, 0.0)
        mu_s[rows, :] = jnp.broadcast_to(mu, (SGU_ROWS, LANES))
        rstd_s[rows, :] = jnp.broadcast_to(lax.rsqrt(var + LN_EPS), (SGU_ROWS, LANES))

    @pl.when(pl.program_id(0) == 0)
    def _():
        _cast_weight(w_in_hbm, w_in_ref, st_in, sems, 0)
        _cast_weight(w_out_hbm, w_out_ref, st_out, sems, 1)
        _cast_weight(w_kv_hbm, w_kv_ref, st_kv, sems, 2)
        for b in range(n_blocks):
            project_v(x_ref, b)

    xb = x_ref[...].astype(BF16)
    lane_tiles = gd // LANES
    mu = jnp.concatenate([mu_s[...]] * lane_tiles, axis=1)
    rstd = jnp.concatenate([rstd_s[...]] * lane_tiles, axis=1)

    row = lax.broadcasted_iota(jnp.int32, (CHUNK, CHUNK), 0)
    col = lax.broadcasted_iota(jnp.int32, (CHUNK, CHUNK), 1)
    tril = col <= row

    def gate_of(g):
        per_group = gd // WEIGHT_COLS
        u = jnp.concatenate(_blocked_dot(xb, w_in_ref, g * per_group, per_group), axis=1)
        z = jnp.concatenate(_blocked_dot(xb, w_in_ref, (2 * A_GROUPS + g) * per_group, per_group), axis=1)
        return u * _silu(z)

    def spatial(g, gate):
        cols = slice(g * gd, (g + 1) * gd)
        ws = jnp.where(tril, ws_ref[g], 0.0).astype(BF16)
        vn = ((v_s[:, cols] - mu) * rstd * lng_ref[:, cols] + lnb_ref[:, cols]).astype(BF16)
        for c in range(n_chunks):
            rows = slice(c * CHUNK, (c + 1) * CHUNK)
            s = jnp.dot(ws, vn[rows], preferred_element_type=F32) + bsp_ref[g]
            y_s[rows, cols] = (gate[rows] * s).astype(BF16)

    gates = {g: gate_of(g) for g in range(SPATIAL_LAG)}
    for g in range(A_GROUPS):
        if g + SPATIAL_LAG < A_GROUPS:
            gates[g + SPATIAL_LAG] = gate_of(g + SPATIAL_LAG)
        spatial(g, gates.pop(g))

    def kv_of(b, h):
        for r in range(b * SGU_ROWS, (b + 1) * SGU_ROWS, CHUNK):
            hr = h[r - b * SGU_ROWS:r - b * SGU_ROWS + CHUNK]
            kv_ref[r:r + CHUNK, :] = jnp.concatenate(
                _blocked_dot(hr.astype(BF16), w_kv_ref, 0, w_kv_ref.shape[0]), axis=1).astype(BF16)

    h_prev = None
    for b in range(n_blocks):
        rows = slice(b * SGU_ROWS, (b + 1) * SGU_ROWS)
        sub = jnp.concatenate(_blocked_dot(y_s[rows, :], w_out_ref, 0, w_out_ref.shape[0]), axis=1)
        if h_prev is not None:
            kv_of(b - 1, h_prev)
        project_v(xn_ref, b)
        h_prev = _layer_norm(alpha * x_ref[rows, :] + sub, pg_ref[...], pb_ref[...])
        h_ref[rows, :] = h_prev
    kv_of(n_blocks - 1, h_prev)


def _sgu_layer(x2, w_in, ln_g, ln_b, w_spatial, bsp_b, w_out, w_kv, pg, pb, *, alpha, tm):
    t_tokens, d_model = x2.shape
    aw = w_out.shape[0]
    gd = aw // A_GROUPS
    kvw = w_kv.shape[1]
    n_tiles = t_tokens // tm
    whole = pl.BlockSpec(memory_space=pltpu.VMEM)
    hbm = pl.BlockSpec(memory_space=pl.ANY)
    stage = lambda w: ((STAGE_SLOTS, _stage_rows(w), w.shape[1]), F32)
    blocked = lambda w: ((w.shape[1] // WEIGHT_COLS, w.shape[0], WEIGHT_COLS), BF16)
    scratch = [((tm, aw), F32), ((tm, LANES), F32), ((tm, LANES), F32), ((tm, aw), BF16),
               blocked(w_in), blocked(w_out), blocked(w_kv), stage(w_in), stage(w_out), stage(w_kv)]
    vmem = _vmem_limit(
        resident=[(a.shape, a.dtype) for a in (ln_g, ln_b, w_spatial, bsp_b, pg, pb)],
        pipelined=[((tm, d_model), F32)] * 3 + [((tm, kvw), BF16)],
        scratch=scratch,
        live_values=[((tm, gd), F32)] * (3 + SPATIAL_LAG) + [((tm, d_model), F32)] * 2)
    return pl.pallas_call(
        functools.partial(_sgu_kernel, alpha=alpha),
        grid=(n_tiles,),
        in_specs=[pl.BlockSpec((tm, d_model), lambda i: (i, 0)),
                  pl.BlockSpec((tm, d_model), lambda i: (jnp.minimum(i + 1, n_tiles - 1), 0)),
                  hbm, whole, whole, whole, whole, hbm, hbm, whole, whole],
        out_specs=[pl.BlockSpec((tm, d_model), lambda i: (i, 0)),
                   pl.BlockSpec((tm, kvw), lambda i: (i, 0))],
        out_shape=[jax.ShapeDtypeStruct((t_tokens, d_model), F32),
                   jax.ShapeDtypeStruct((t_tokens, kvw), BF16)],
        scratch_shapes=[pltpu.VMEM(shape, dtype) for shape, dtype in scratch]
        + [pltpu.SemaphoreType.DMA((3, STAGE_SLOTS))],
        compiler_params=pltpu.CompilerParams(dimension_semantics=("arbitrary",), vmem_limit_bytes=vmem),
        name="sgu_layer",
    )(x2, x2, w_in, ln_g, ln_b, w_spatial, bsp_b, w_out, w_kv, pg, pb)


def _swa_kernel(h_ref, hn_ref, kv_ref, kvp_ref, w_in_hbm, bias_ref, scale_ref, w_out_hbm,
                pg_ref, pb_ref, o_ref, q_s, g_s, y_s, s_s, w_in_ref, w_out_ref, st_in, st_out, sems,
                *, alpha, tiles_per_seq):
    tm, _ = h_ref.shape
    bw = y_s.shape[1]
    n_kv = bw // (HEAD_DIM * Q_PER_KV)
    pairs_per_kv = Q_PER_KV // 2
    i = pl.program_id(0)

    n_blocks = tm // PROJ_ROWS
    chunks_per_block = PROJ_ROWS // CHUNK

    def project_q(src_ref, b):
        rows = slice(b * PROJ_ROWS, (b + 1) * PROJ_ROWS)
        q = jnp.concatenate(_blocked_dot(src_ref[rows, :].astype(BF16), w_in_ref, 0, bw // WEIGHT_COLS), axis=1)
        q = (q * (HEAD_DIM ** -0.5)).astype(BF16)
        for p in range(bw // LANES):
            q_s[p, rows, :] = q[:, p * LANES:(p + 1) * LANES]

    def project_z(src_ref, b):
        rows = slice(b * PROJ_ROWS, (b + 1) * PROJ_ROWS)
        blocks = bw // WEIGHT_COLS
        g_s[rows, :] = _silu(jnp.concatenate(
            _blocked_dot(src_ref[rows, :].astype(BF16), w_in_ref, blocks, blocks), axis=1))

    @pl.when(i == 0)
    def _():
        _cast_weight(w_in_hbm, w_in_ref, st_in, sems, 0)
        _cast_weight(w_out_hbm, w_out_ref, st_out, sems, 1)
        for b in range(n_blocks):
            project_q(h_ref, b)
            project_z(h_ref, b)

    as_bf16 = lambda a: a.astype(F32).astype(BF16)
    lo = as_bf16(lax.broadcasted_iota(jnp.int32, (2 * CHUNK, LANES), 1)) < HEAD_DIM
    key0 = as_bf16(lax.broadcasted_iota(jnp.int32, (2 * CHUNK, LANES), 0)) < 1
    zeros_k = jnp.zeros((2 * CHUNK, LANES), BF16)
    zeros_v = jnp.zeros((HEAD_DIM, 2 * CHUNK), BF16)
    zeros_d = jnp.zeros((DEN_ROWS, 2 * CHUNK), BF16)
    ones_d = jnp.ones((DEN_ROWS, 2 * CHUNK), BF16)
    first = (i % tiles_per_seq == 0).astype(jnp.int32)

    bands = {}

    def band_of(c):
        if c not in bands:
            rows = slice(c * CHUNK, (c + 1) * CHUNK)
            kv_prev = kvp_ref[...] if c == 0 else kv_ref[(c - 1) * CHUNK:c * CHUNK, :]
            band = jnp.concatenate([kv_prev, kv_ref[rows, :]], axis=0)
            kb, vb = band[:, :LANES], band[:, LANES:]
            kr = pltpu.roll(kb, HEAD_DIM, axis=1)
            v_t = jnp.where(key0, zeros_k, vb).astype(F32).T.astype(BF16)
            bands[c] = (kb, kr, v_t)
        return bands[c]

    def scores_of(c, kvh, slot):
        rows = slice(c * CHUNK, (c + 1) * CHUNK)
        kb, kr, v_t = band_of(c)
        variant = first if c == 0 else 0
        k_src, k_alt = (kb, kr) if kvh == 0 else (kr, kb)
        k_cat = jnp.concatenate([jnp.where(lo, k_src, zeros_k), jnp.where(lo, zeros_k, k_alt)], axis=0)
        vk = v_t[kvh * HEAD_DIM:(kvh + 1) * HEAD_DIM]
        v_lhs = jnp.concatenate(
            [jnp.concatenate([vk, zeros_v], axis=1), jnp.concatenate([zeros_v, vk], axis=1),
             jnp.concatenate([ones_d, zeros_d], axis=1), jnp.concatenate([zeros_d, ones_d], axis=1)],
            axis=0)
        pair0 = kvh * pairs_per_kv
        q_stack = jnp.concatenate(
            [q_s[pair0 + p, rows, :] for p in range(pairs_per_kv)], axis=0)
        s_s[slot] = lax.dot_general(k_cat, q_stack, (((1,), (1,)), ((), ())),
                                    preferred_element_type=F32)
        return rows, variant, kvh, slot, v_lhs

    def attend(rows, variant, kvh, slot, v_lhs):
        pair0 = kvh * pairs_per_kv
        e_cols = []
        for p in range(pairs_per_kv):
            qc = slice(p * CHUNK, (p + 1) * CHUNK)
            halves = []
            for e in range(2):
                kr_ = slice(e * 2 * CHUNK, (e + 1) * 2 * CHUNK)
                le = s_s[slot, kr_, qc] * scale_ref[variant, kr_, :] + bias_ref[variant, kvh, kr_, qc]
                halves.append(jnp.exp2(le - jnp.max(le, axis=0, keepdims=True)).astype(BF16))
            e_cols.append(jnp.concatenate(halves, axis=0))
        pv_t = jnp.dot(v_lhs, jnp.concatenate(e_cols, axis=1), preferred_element_type=F32)
        n_num = 2 * HEAD_DIM
        inv_e = 1.0 / pv_t[n_num:n_num + SUBLANES]
        inv_o = 1.0 / pv_t[n_num + DEN_ROWS:n_num + DEN_ROWS + SUBLANES]
        inv = jnp.concatenate([inv_e] * (HEAD_DIM // SUBLANES) + [inv_o] * (HEAD_DIM // SUBLANES), axis=0)
        o_t = pv_t[:n_num] * inv
        for p in range(pairs_per_kv):
            pc = slice((pair0 + p) * LANES, (pair0 + p + 1) * LANES)
            y_s[rows, pc] = (o_t[:, p * CHUNK:(p + 1) * CHUNK].T * g_s[rows, pc]).astype(BF16)

    block_units = [[(c, kvh) for c in range(b * chunks_per_block, (b + 1) * chunks_per_block)
                    for kvh in range(n_kv)] for b in range(n_blocks)]
    units_per_block = chunks_per_block * n_kv
    slot_of = lambda b, j: (b % 2) * units_per_block + j
    scored = [scores_of(*u, slot_of(0, j)) for j, u in enumerate(block_units[0])]
    for b in range(n_blocks):
        rows = slice(b * PROJ_ROWS, (b + 1) * PROJ_ROWS)
        scored_next = []
        for j, unit in enumerate(scored):
            attend(*unit)
            if b + 1 < n_blocks:
                scored_next.append(scores_of(*block_units[b + 1][j], slot_of(b + 1, j)))
        scored = scored_next
        project_q(hn_ref, b)
        sub = jnp.concatenate(_blocked_dot(y_s[rows, :], w_out_ref, 0, w_out_ref.shape[0]), axis=1)
        project_z(hn_ref, b)
        o_ref[rows, :] = _layer_norm(alpha * h_ref[rows, :] + sub, pg_ref[...], pb_ref[...])


def _swa_layer(h1, kv, w_in, bias_tab, scale_tab, w_out, pg, pb, *, alpha, tm, seq):
    t_tokens, d_model = h1.shape
    bw = w_out.shape[0]
    kvw2 = kv.shape[1]
    cpt = tm // CHUNK
    n_tiles = t_tokens // tm
    n_units = 2 * (PROJ_ROWS // CHUNK) * (bw // (HEAD_DIM * Q_PER_KV))
    whole = pl.BlockSpec(memory_space=pltpu.VMEM)
    hbm = pl.BlockSpec(memory_space=pl.ANY)
    stage = lambda w: ((STAGE_SLOTS, _stage_rows(w), w.shape[1]), F32)
    blocked = lambda w: ((w.shape[1] // WEIGHT_COLS, w.shape[0], WEIGHT_COLS), BF16)
    scratch = [((bw // LANES, tm, LANES), BF16), ((tm, bw), F32), ((tm, bw), BF16), ((n_units, 4 * CHUNK, 4 * CHUNK), F32),
               blocked(w_in), blocked(w_out), stage(w_in), stage(w_out)]
    vmem = _vmem_limit(
        resident=[(a.shape, a.dtype) for a in (bias_tab, scale_tab, pg, pb)],
        pipelined=[((tm, d_model), F32)] * 3 + [((tm, kvw2), BF16), ((CHUNK, kvw2), BF16)],
        scratch=scratch,
        live_values=[((PROJ_ROWS, bw), F32)] * 2 + [((PROJ_ROWS, d_model), F32)] * 2
        + [((4 * CHUNK, 4 * CHUNK), BF16)])
    return pl.pallas_call(
        functools.partial(_swa_kernel, alpha=alpha, tiles_per_seq=seq // tm),
        grid=(n_tiles,),
        in_specs=[pl.BlockSpec((tm, d_model), lambda i: (i, 0)),
                  pl.BlockSpec((tm, d_model), lambda i: (jnp.minimum(i + 1, n_tiles - 1), 0)),
                  pl.BlockSpec((tm, kvw2), lambda i: (i, 0)),
                  pl.BlockSpec((CHUNK, kvw2), lambda i: (jnp.maximum(i * cpt - 1, 0), 0)),
                  hbm, whole, whole, hbm, whole, whole],
        out_specs=pl.BlockSpec((tm, d_model), lambda i: (i, 0)),
        out_shape=jax.ShapeDtypeStruct((t_tokens, d_model), F32),
        scratch_shapes=[pltpu.VMEM(shape, dtype) for shape, dtype in scratch]
        + [pltpu.SemaphoreType.DMA((2, STAGE_SLOTS))],
        compiler_params=pltpu.CompilerParams(dimension_semantics=("arbitrary",), vmem_limit_bytes=vmem),
        name="swa_layer",
    )(h1, h1, kv, kv, w_in, bias_tab, scale_tab, w_out, pg, pb)


def kernel(x, w_in_a, sgu_ln_g, sgu_ln_b, w_spatial, b_spatial, w_out_a, w_kv, w_in_b, attn_sinks,
           rel_bias, w_out_b, post_ln_g, post_ln_b):
    bsz, seq, d_model = x.shape
    depth = post_ln_g.shape[0]
    assert w_in_a.shape[0] == 1 and w_in_b.shape[0] == 1 and depth == 2
    assert seq % CHUNK == 0 and w_spatial.shape[1:] == (A_GROUPS, CHUNK, CHUNK)
    assert w_kv.shape[1] == 2 * LANES and rel_bias.shape == (REL_BUCKETS, w_out_b.shape[1] // HEAD_DIM)
    alpha = (2.0 * depth) ** 0.25
    aw = w_out_a.shape[1]
    gd = aw // A_GROUPS
    tm = TOKEN_TILE
    assert seq % tm == 0

    x2 = x.reshape(bsz * seq, d_model)
    row = lambda a: a.reshape(1, -1)
    bsp_b = jnp.broadcast_to(b_spatial[0][:, :, None], (A_GROUPS, CHUNK, gd))
    h1, kv = _sgu_layer(x2, w_in_a[0], row(sgu_ln_g[0]), row(sgu_ln_b[0]), w_spatial[0], bsp_b,
                        w_out_a[0], w_kv, row(post_ln_g[0]), row(post_ln_b[0]),
                        alpha=alpha, tm=tm)
    bias_tab = _rel_bias_table(rel_bias, attn_sinks[0])
    out = _swa_layer(h1, kv, w_in_b[0], bias_tab, jnp.asarray(_band_scale_table()),
                     w_out_b[0], row(post_ln_g[1]), row(post_ln_b[1]),
                     alpha=alpha, tm=tm, seq=seq)
    return out.reshape(bsz, seq, d_model)
```

```python
import functools
import math

import numpy as np
import jax
import jax.numpy as jnp
from jax import lax
from jax.experimental import pallas as pl
from jax.experimental.pallas import tpu as pltpu

F32 = jnp.float32
BF16 = jnp.bfloat16

CHUNK = 128
A_GROUPS = 8
HEAD_DIM = 64
Q_PER_KV = 8
REL_BUCKETS = 32
REL_MAX_DIST = 128
LN_EPS = 1e-5
NEG_INF = -1e30
LOG2E = math.log2(math.e)

LANES = 128
SUBLANES = 8
PROJ_ROWS = 2 * CHUNK
TOKEN_TILE = 4 * CHUNK
SGU_ROWS = TOKEN_TILE
WEIGHT_COLS = 2 * LANES
DEN_ROWS = 16
SPATIAL_LAG = 1
V7X_VMEM_BYTES = 64 * 1024 * 1024
COMPILER_TEMP_BYTES = 8 * 1024 * 1024
STAGE_BYTES = 256 * 1024
STAGE_SLOTS = 8


def _bucket_lower_bounds():
    max_exact = REL_BUCKETS // 2
    d = np.arange(REL_MAX_DIST, dtype=np.int32)
    df = np.maximum(d, 1).astype(np.float32)
    large = max_exact + (np.log(df / np.float32(max_exact)) / np.float32(math.log(REL_MAX_DIST / max_exact))
                         * np.float32(REL_BUCKETS - max_exact)).astype(np.int32)
    large = np.minimum(large, REL_BUCKETS - 1)
    bucket = np.where(d < max_exact, d, large)
    assert np.all(np.diff(bucket) >= 0) and bucket[0] == 0
    return [int(np.argmax(bucket >= b)) if np.any(bucket >= b) else REL_MAX_DIST for b in range(REL_BUCKETS)]


def _band_scale_table():
    j = np.arange(2 * CHUNK)[:, None]
    t = np.arange(CHUNK)[None, :]
    d = t + CHUNK - j
    in_window = (d >= 0) & (d < CHUNK)
    first = in_window & (j >= CHUNK)
    tab = np.stack([np.tile(in_window, (2, 1)), np.tile(first, (2, 1))]).astype(np.float32)
    return tab * np.float32(LOG2E)


def _layer_norm(x, g, b):
    mu = jnp.mean(x, axis=-1, keepdims=True)
    d = x - mu
    var = jnp.mean(d * d, axis=-1, keepdims=True)
    return d * lax.rsqrt(var + LN_EPS) * g + b


def _silu(z):
    half = 0.5 * z
    return half + half * jnp.tanh(half)


def _blocked_dot(lhs, w_ref, first, count):
    return [jnp.dot(lhs, w_ref[first + j], preferred_element_type=F32) for j in range(count)]


def _tile_bytes(shape, dtype):
    itemsize = jnp.dtype(dtype).itemsize
    sublanes = SUBLANES * 4 // itemsize
    *lead, rows, cols = (1,) + tuple(shape)
    return math.prod(lead) * (-(-rows // sublanes) * sublanes) * (-(-cols // LANES) * LANES) * itemsize


def _vmem_limit(resident, pipelined, scratch, live_values):
    size = lambda entries: sum(_tile_bytes(shape, dtype) for shape, dtype in entries)
    total = size(resident) + 2 * size(pipelined) + size(scratch) + size(live_values) + COMPILER_TEMP_BYTES
    assert total <= V7X_VMEM_BYTES, total
    return total


def _stage_rows(w_hbm):
    k, n = w_hbm.shape
    rows = SUBLANES
    while 2 * rows * n * jnp.dtype(F32).itemsize <= STAGE_BYTES and k % (2 * rows) == 0:
        rows *= 2
    assert k % rows == 0
    return rows


def _cast_weight(w_hbm, w_s, stage, sems, k):
    rows = stage.shape[1]
    n = w_hbm.shape[0] // rows

    def copy(c):
        slot = c % STAGE_SLOTS
        return pltpu.make_async_copy(w_hbm.at[pl.ds(c * rows, rows), :], stage.at[slot], sems.at[k, slot])

    for c in range(min(STAGE_SLOTS - 1, n)):
        copy(c).start()
    for c in range(n):
        if c + STAGE_SLOTS - 1 < n:
            copy(c + STAGE_SLOTS - 1).start()
        copy(c).wait()
        chunk = stage[c % STAGE_SLOTS].astype(BF16)
        for j in range(w_s.shape[0]):
            w_s[j, c * rows:(c + 1) * rows, :] = chunk[:, j * WEIGHT_COLS:(j + 1) * WEIGHT_COLS]


def _rel_bias_kernel(rb_ref, sink_ref, out_ref, *, lower_bounds, pairs_per_kv):
    pair = pl.program_id(0) * pairs_per_kv + pl.program_id(1)
    j = lax.broadcasted_iota(jnp.int32, (2 * CHUNK, CHUNK), 0)
    t = lax.broadcasted_iota(jnp.int32, (2 * CHUNK, CHUNK), 1)
    dist = t + CHUNK - j
    in_window = (dist >= 0) & (dist < CHUNK)
    d = jnp.clip(dist, 0, REL_MAX_DIST - 1)
    for e in range(2):
        h = 2 * pair + e
        acc = jnp.full((2 * CHUNK, CHUNK), rb_ref[0, h], F32)
        for b in range(1, REL_BUCKETS):
            if lower_bounds[b] < REL_MAX_DIST:
                acc = jnp.where(d >= lower_bounds[b], rb_ref[b, h], acc)
        hidden = jnp.where(j == 0, sink_ref[h], NEG_INF)
        for variant in range(2):
            visible = in_window & (j >= variant * CHUNK)
            out_ref[variant, 0, e * 2 * CHUNK:(e + 1) * 2 * CHUNK, :] = jnp.where(visible, acc, hidden) * LOG2E


def _rel_bias_table(rel_bias, sinks):
    n_heads = rel_bias.shape[1]
    n_kv = n_heads // Q_PER_KV
    pairs_per_kv = Q_PER_KV // 2
    smem = pl.BlockSpec(memory_space=pltpu.SMEM)
    return pl.pallas_call(
        functools.partial(_rel_bias_kernel, lower_bounds=_bucket_lower_bounds(), pairs_per_kv=pairs_per_kv),
        grid=(n_kv, pairs_per_kv),
        in_specs=[smem, smem],
        out_specs=pl.BlockSpec((2, 1, 4 * CHUNK, CHUNK), lambda k, p: (0, k, 0, p)),
        out_shape=jax.ShapeDtypeStruct((2, n_kv, 4 * CHUNK, pairs_per_kv * CHUNK), F32),
        name="rel_bias_table",
    )(rel_bias, sinks)


def _sgu_kernel(x_ref, xn_ref, w_in_hbm, lng_ref, lnb_ref, ws_ref, bsp_ref, w_out_hbm, w_kv_hbm,
                pg_ref, pb_ref, h_ref, kv_ref, v_s, mu_s, rstd_s, w_in_ref, w_out_ref, w_kv_ref,
                st_in, st_out, st_kv, sems, *, alpha):
    tm, _ = x_ref.shape
    aw = v_s.shape[1]
    gd = aw // A_GROUPS
    n_chunks = tm // CHUNK
    n_blocks = tm // SGU_ROWS

    def project_v(src_ref, b):
        rows = slice(b * SGU_ROWS, (b + 1) * SGU_ROWS)
        blocks = aw // WEIGHT_COLS
        v = jnp.concatenate(_blocked_dot(src_ref[rows, :].astype(BF16), w_in_ref, blocks, blocks), axis=1)
        v_s[rows, :] = v
        mu = jnp.mean(v, axis=-1, keepdims=True)
        var = jnp.maximum(jnp.mean(v * v, axis=-1, keepdims=True) - mu * mu, 0.0)
        mu_s[rows, :] = jnp.broadcast_to(mu, (SGU_ROWS, LANES))
        rstd_s[rows, :] = jnp.broadcast_to(lax.rsqrt(var + LN_EPS), (SGU_ROWS, LANES))

    @pl.when(pl.program_id(0) == 0)
    def _():
        _cast_weight(w_in_hbm, w_in_ref, st_in, sems, 0)
        _cast_weight(w_out_hbm, w_out_ref, st_out, sems, 1)
        _cast_weight(w_kv_hbm, w_kv_ref, st_kv, sems, 2)
        for b in range(n_blocks):
            project_v(x_ref, b)

    xb = x_ref[...].astype(BF16)
    lane_tiles = gd // LANES
    mu = jnp.concatenate([mu_s[...]] * lane_tiles, axis=1)
    rstd = jnp.concatenate([rstd_s[...]] * lane_tiles, axis=1)

    row = lax.broadcasted_iota(jnp.int32, (CHUNK, CHUNK), 0)
    col = lax.broadcasted_iota(jnp.int32, (CHUNK, CHUNK), 1)
    tril = col <= row

    def gate_of(g):
        per_group = gd // WEIGHT_COLS
        u = jnp.concatenate(_blocked_dot(xb, w_in_ref, g * per_group, per_group), axis=1)
        z = jnp.concatenate(_blocked_dot(xb, w_in_ref, (2 * A_GROUPS + g) * per_group, per_group), axis=1)
        return u * _silu(z)

    def spatial(g, gate):
        cols = slice(g * gd, (g + 1) * gd)
        ws = jnp.where(tril, ws_ref[g], 0.0).astype(BF16)
        vn = ((v_s[:, cols] - mu) * rstd * lng_ref[:, cols] + lnb_ref[:, cols]).astype(BF16)
        y_chunks = []
        for c in range(n_chunks):
            rows = slice(c * CHUNK, (c + 1) * CHUNK)
            s = jnp.dot(ws, vn[rows], preferred_element_type=F32) + bsp_ref[g]
            y_chunks.append((gate[rows] * s).astype(BF16))
        return jnp.concatenate(y_chunks, axis=0)

    gates = {g: gate_of(g) for g in range(SPATIAL_LAG)}
    y_groups = []
    for g in range(A_GROUPS):
        if g + SPATIAL_LAG < A_GROUPS:
            gates[g + SPATIAL_LAG] = gate_of(g + SPATIAL_LAG)
        y_groups.append(spatial(g, gates.pop(g)))
    y = jnp.concatenate(y_groups, axis=1)

    def kv_of(b, h):
        for r in range(b * SGU_ROWS, (b + 1) * SGU_ROWS, CHUNK):
            hr = h[r - b * SGU_ROWS:r - b * SGU_ROWS + CHUNK]
            kv_ref[r:r + CHUNK, :] = jnp.concatenate(
                _blocked_dot(hr.astype(BF16), w_kv_ref, 0, w_kv_ref.shape[0]), axis=1).astype(BF16)

    h_prev = None
    for b in range(n_blocks):
        rows = slice(b * SGU_ROWS, (b + 1) * SGU_ROWS)
        sub = jnp.concatenate(_blocked_dot(y[rows], w_out_ref, 0, w_out_ref.shape[0]), axis=1)
        if h_prev is not None:
            kv_of(b - 1, h_prev)
        project_v(xn_ref, b)
        h_prev = _layer_norm(alpha * x_ref[rows, :] + sub, pg_ref[...], pb_ref[...])
        h_ref[rows, :] = h_prev
    kv_of(n_blocks - 1, h_prev)


def _sgu_layer(x2, w_in, ln_g, ln_b, w_spatial, bsp_b, w_out, w_kv, pg, pb, *, alpha, tm):
    t_tokens, d_model = x2.shape
    aw = w_out.shape[0]
    gd = aw // A_GROUPS
    kvw = w_kv.shape[1]
    n_tiles = t_tokens // tm
    whole = pl.BlockSpec(memory_space=pltpu.VMEM)
    hbm = pl.BlockSpec(memory_space=pl.ANY)
    stage = lambda w: ((STAGE_SLOTS, _stage_rows(w), w.shape[1]), F32)
    blocked = lambda w: ((w.shape[1] // WEIGHT_COLS, w.shape[0], WEIGHT_COLS), BF16)
    scratch = [((tm, aw), F32), ((tm, LANES), F32), ((tm, LANES), F32),
               blocked(w_in), blocked(w_out), blocked(w_kv), stage(w_in), stage(w_out), stage(w_kv)]
    vmem = _vmem_limit(
        resident=[(a.shape, a.dtype) for a in (ln_g, ln_b, w_spatial, bsp_b, pg, pb)],
        pipelined=[((tm, d_model), F32)] * 3 + [((tm, kvw), BF16)],
        scratch=scratch,
        live_values=[((tm, gd), F32)] * (3 + SPATIAL_LAG) + [((tm, aw), BF16)] + [((tm, d_model), F32)] * 2)
    return pl.pallas_call(
        functools.partial(_sgu_kernel, alpha=alpha),
        grid=(n_tiles,),
        in_specs=[pl.BlockSpec((tm, d_model), lambda i: (i, 0)),
                  pl.BlockSpec((tm, d_model), lambda i: (jnp.minimum(i + 1, n_tiles - 1), 0)),
                  hbm, whole, whole, whole, whole, hbm, hbm, whole, whole],
        out_specs=[pl.BlockSpec((tm, d_model), lambda i: (i, 0)),
                   pl.BlockSpec((tm, kvw), lambda i: (i, 0))],
        out_shape=[jax.ShapeDtypeStruct((t_tokens, d_model), F32),
                   jax.ShapeDtypeStruct((t_tokens, kvw), BF16)],
        scratch_shapes=[pltpu.VMEM(shape, dtype) for shape, dtype in scratch]
        + [pltpu.SemaphoreType.DMA((3, STAGE_SLOTS))],
        compiler_params=pltpu.CompilerParams(dimension_semantics=("arbitrary",), vmem_limit_bytes=vmem),
        name="sgu_layer",
    )(x2, x2, w_in, ln_g, ln_b, w_spatial, bsp_b, w_out, w_kv, pg, pb)


def _swa_kernel(h_ref, hn_ref, kv_ref, kvp_ref, w_in_hbm, bias_ref, scale_ref, w_out_hbm,
                pg_ref, pb_ref, o_ref, q_s, g_s, s_s, w_in_ref, w_out_ref, st_in, st_out, sems,
                *, alpha, tiles_per_seq):
    tm, _ = h_ref.shape
    bw = g_s.shape[1]
    n_kv = bw // (HEAD_DIM * Q_PER_KV)
    pairs_per_kv = Q_PER_KV // 2
    i = pl.program_id(0)

    n_blocks = tm // PROJ_ROWS
    chunks_per_block = PROJ_ROWS // CHUNK

    def project_q(src_ref, b):
        rows = slice(b * PROJ_ROWS, (b + 1) * PROJ_ROWS)
        q = jnp.concatenate(_blocked_dot(src_ref[rows, :].astype(BF16), w_in_ref, 0, bw // WEIGHT_COLS), axis=1)
        q = (q * (HEAD_DIM ** -0.5)).astype(BF16)
        for p in range(bw // LANES):
            q_s[p, rows, :] = q[:, p * LANES:(p + 1) * LANES]

    def project_z(src_ref, b):
        rows = slice(b * PROJ_ROWS, (b + 1) * PROJ_ROWS)
        blocks = bw // WEIGHT_COLS
        g_s[rows, :] = _silu(jnp.concatenate(
            _blocked_dot(src_ref[rows, :].astype(BF16), w_in_ref, blocks, blocks), axis=1))

    @pl.when(i == 0)
    def _():
        _cast_weight(w_in_hbm, w_in_ref, st_in, sems, 0)
        _cast_weight(w_out_hbm, w_out_ref, st_out, sems, 1)
        for b in range(n_blocks):
            project_q(h_ref, b)
            project_z(h_ref, b)

    as_bf16 = lambda a: a.astype(F32).astype(BF16)
    lo = as_bf16(lax.broadcasted_iota(jnp.int32, (2 * CHUNK, LANES), 1)) < HEAD_DIM
    key0 = as_bf16(lax.broadcasted_iota(jnp.int32, (2 * CHUNK, LANES), 0)) < 1
    zeros_k = jnp.zeros((2 * CHUNK, LANES), BF16)
    zeros_v = jnp.zeros((HEAD_DIM, 2 * CHUNK), BF16)
    zeros_d = jnp.zeros((DEN_ROWS, 2 * CHUNK), BF16)
    ones_d = jnp.ones((DEN_ROWS, 2 * CHUNK), BF16)
    first = (i % tiles_per_seq == 0).astype(jnp.int32)

    bands = {}

    def band_of(c):
        if c not in bands:
            rows = slice(c * CHUNK, (c + 1) * CHUNK)
            kv_prev = kvp_ref[...] if c == 0 else kv_ref[(c - 1) * CHUNK:c * CHUNK, :]
            band = jnp.concatenate([kv_prev, kv_ref[rows, :]], axis=0)
            kb, vb = band[:, :LANES], band[:, LANES:]
            kr = pltpu.roll(kb, HEAD_DIM, axis=1)
            v_t = jnp.where(key0, zeros_k, vb).astype(F32).T.astype(BF16)
            bands[c] = (kb, kr, v_t)
        return bands[c]

    def scores_of(c, kvh, slot):
        rows = slice(c * CHUNK, (c + 1) * CHUNK)
        kb, kr, v_t = band_of(c)
        variant = first if c == 0 else 0
        k_src, k_alt = (kb, kr) if kvh == 0 else (kr, kb)
        k_cat = jnp.concatenate([jnp.where(lo, k_src, zeros_k), jnp.where(lo, zeros_k, k_alt)], axis=0)
        vk = v_t[kvh * HEAD_DIM:(kvh + 1) * HEAD_DIM]
        v_lhs = jnp.concatenate(
            [jnp.concatenate([vk, zeros_v], axis=1), jnp.concatenate([zeros_v, vk], axis=1),
             jnp.concatenate([ones_d, zeros_d], axis=1), jnp.concatenate([zeros_d, ones_d], axis=1)],
            axis=0)
        pair0 = kvh * pairs_per_kv
        q_stack = jnp.concatenate(
            [q_s[pair0 + p, rows, :] for p in range(pairs_per_kv)], axis=0)
        s_s[slot] = lax.dot_general(k_cat, q_stack, (((1,), (1,)), ((), ())),
                                    preferred_element_type=F32)
        return rows, variant, kvh, slot, v_lhs

    def attend(rows, variant, kvh, slot, v_lhs):
        pair0 = kvh * pairs_per_kv
        e_cols = []
        for p in range(pairs_per_kv):
            qc = slice(p * CHUNK, (p + 1) * CHUNK)
            halves = []
            for e in range(2):
                kr_ = slice(e * 2 * CHUNK, (e + 1) * 2 * CHUNK)
                le = s_s[slot, kr_, qc] * scale_ref[variant, kr_, :] + bias_ref[variant, kvh, kr_, qc]
                halves.append(jnp.exp2(le - jnp.max(le, axis=0, keepdims=True)).astype(BF16))
            e_cols.append(jnp.concatenate(halves, axis=0))
        pv_t = jnp.dot(v_lhs, jnp.concatenate(e_cols, axis=1), preferred_element_type=F32)
        n_num = 2 * HEAD_DIM
        inv_e = 1.0 / pv_t[n_num:n_num + SUBLANES]
        inv_o = 1.0 / pv_t[n_num + DEN_ROWS:n_num + DEN_ROWS + SUBLANES]
        inv = jnp.concatenate([inv_e] * (HEAD_DIM // SUBLANES) + [inv_o] * (HEAD_DIM // SUBLANES), axis=0)
        o_t = pv_t[:n_num] * inv
        pieces = []
        for p in range(pairs_per_kv):
            pc = slice((pair0 + p) * LANES, (pair0 + p + 1) * LANES)
            pieces.append((o_t[:, p * CHUNK:(p + 1) * CHUNK].T * g_s[rows, pc]).astype(BF16))
        return jnp.concatenate(pieces, axis=1)

    block_units = [[(c, kvh) for c in range(b * chunks_per_block, (b + 1) * chunks_per_block)
                    for kvh in range(n_kv)] for b in range(n_blocks)]
    units_per_block = chunks_per_block * n_kv
    slot_of = lambda b, j: (b % 2) * units_per_block + j
    scored = [scores_of(*u, slot_of(0, j)) for j, u in enumerate(block_units[0])]
    for b in range(n_blocks):
        rows = slice(b * PROJ_ROWS, (b + 1) * PROJ_ROWS)
        scored_next = []
        y_parts = []
        for j, unit in enumerate(scored):
            y_parts.append(attend(*unit))
            if b + 1 < n_blocks:
                scored_next.append(scores_of(*block_units[b + 1][j], slot_of(b + 1, j)))
        scored = scored_next
        project_q(hn_ref, b)
        y = jnp.concatenate([jnp.concatenate(y_parts[c * n_kv:(c + 1) * n_kv], axis=1)
                             for c in range(chunks_per_block)], axis=0)
        sub = jnp.concatenate(_blocked_dot(y, w_out_ref, 0, w_out_ref.shape[0]), axis=1)
        project_z(hn_ref, b)
        o_ref[rows, :] = _layer_norm(alpha * h_ref[rows, :] + sub, pg_ref[...], pb_ref[...])


def _swa_layer(h1, kv, w_in, bias_tab, scale_tab, w_out, pg, pb, *, alpha, tm, seq):
    t_tokens, d_model = h1.shape
    bw = w_out.shape[0]
    kvw2 = kv.shape[1]
    cpt = tm // CHUNK
    n_tiles = t_tokens // tm
    n_units = 2 * (PROJ_ROWS // CHUNK) * (bw // (HEAD_DIM * Q_PER_KV))
    whole = pl.BlockSpec(memory_space=pltpu.VMEM)
    hbm = pl.BlockSpec(memory_space=pl.ANY)
    stage = lambda w: ((STAGE_SLOTS, _stage_rows(w), w.shape[1]), F32)
    blocked = lambda w: ((w.shape[1] // WEIGHT_COLS, w.shape[0], WEIGHT_COLS), BF16)
    scratch = [((bw // LANES, tm, LANES), BF16), ((tm, bw), F32), ((n_units, 4 * CHUNK, 4 * CHUNK), F32),
               blocked(w_in), blocked(w_out), stage(w_in), stage(w_out)]
    vmem = _vmem_limit(
        resident=[(a.shape, a.dtype) for a in (bias_tab, scale_tab, pg, pb)],
        pipelined=[((tm, d_model), F32)] * 3 + [((tm, kvw2), BF16), ((CHUNK, kvw2), BF16)],
        scratch=scratch,
        live_values=[((PROJ_ROWS, bw), F32)] * 2 + [((PROJ_ROWS, bw), BF16)] + [((PROJ_ROWS, d_model), F32)] * 2
        + [((4 * CHUNK, 4 * CHUNK), BF16)])
    return pl.pallas_call(
        functools.partial(_swa_kernel, alpha=alpha, tiles_per_seq=seq // tm),
        grid=(n_tiles,),
        in_specs=[pl.BlockSpec((tm, d_model), lambda i: (i, 0)),
                  pl.BlockSpec((tm, d_model), lambda i: (jnp.minimum(i + 1, n_tiles - 1), 0)),
                  pl.BlockSpec((tm, kvw2), lambda i: (i, 0)),
                  pl.BlockSpec((CHUNK, kvw2), lambda i: (jnp.maximum(i * cpt - 1, 0), 0)),
                  hbm, whole, whole, hbm, whole, whole],
        out_specs=pl.BlockSpec((tm, d_model), lambda i: (i, 0)),
        out_shape=jax.ShapeDtypeStruct((t_tokens, d_model), F32),
        scratch_shapes=[pltpu.VMEM(shape, dtype) for shape, dtype in scratch]
        + [pltpu.SemaphoreType.DMA((2, STAGE_SLOTS))],
        compiler_params=pltpu.CompilerParams(dimension_semantics=("arbitrary",), vmem_limit_bytes=vmem),
        name="swa_layer",
    )(h1, h1, kv, kv, w_in, bias_tab, scale_tab, w_out, pg, pb)


def kernel(x, w_in_a, sgu_ln_g, sgu_ln_b, w_spatial, b_spatial, w_out_a, w_kv, w_in_b, attn_sinks,
           rel_bias, w_out_b, post_ln_g, post_ln_b):
    bsz, seq, d_model = x.shape
    depth = post_ln_g.shape[0]
    assert w_in_a.shape[0] == 1 and w_in_b.shape[0] == 1 and depth == 2
    assert seq % CHUNK == 0 and w_spatial.shape[1:] == (A_GROUPS, CHUNK, CHUNK)
    assert w_kv.shape[1] == 2 * LANES and rel_bias.shape == (REL_BUCKETS, w_out_b.shape[1] // HEAD_DIM)
    alpha = (2.0 * depth) ** 0.25
    aw = w_out_a.shape[1]
    gd = aw // A_GROUPS
    tm = TOKEN_TILE
    assert seq % tm == 0

    x2 = x.reshape(bsz * seq, d_model)
    row = lambda a: a.reshape(1, -1)
    bsp_b = jnp.broadcast_to(b_spatial[0][:, :, None], (A_GROUPS, CHUNK, gd))
    h1, kv = _sgu_layer(x2, w_in_a[0], row(sgu_ln_g[0]), row(sgu_ln_b[0]), w_spatial[0], bsp_b,
                        w_out_a[0], w_kv, row(post_ln_g[0]), row(post_ln_b[0]),
                        alpha=alpha, tm=tm)
    bias_tab = _rel_bias_table(rel_bias, attn_sinks[0])
    out = _swa_layer(h1, kv, w_in_b[0], bias_tab, jnp.asarray(_band_scale_table()),
                     w_out_b[0], row(post_ln_g[1]), row(post_ln_b[1]),
                     alpha=alpha, tm=tm, seq=seq)
    return out.reshape(bsz, seq, d_model)
```

```python
import functools
import math

import numpy as np
import jax
import jax.numpy as jnp
from jax import lax
from jax.experimental import pallas as pl
from jax.experimental.pallas import tpu as pltpu

F32 = jnp.float32
BF16 = jnp.bfloat16

CHUNK = 128
A_GROUPS = 8
HEAD_DIM = 64
Q_PER_KV = 8
REL_BUCKETS = 32
REL_MAX_DIST = 128
LN_EPS = 1e-5
NEG_INF = -1e30
LOG2E = math.log2(math.e)

LANES = 128
SUBLANES = 8
PROJ_ROWS = 2 * CHUNK
TOKEN_TILE = 4 * CHUNK
SGU_ROWS = TOKEN_TILE
WEIGHT_COLS = 2 * LANES
DEN_ROWS = 16
SPATIAL_LAG = 1
V7X_VMEM_BYTES = 64 * 1024 * 1024
COMPILER_TEMP_BYTES = 8 * 1024 * 1024
STAGE_BYTES = 256 * 1024
STAGE_SLOTS = 8


def _bucket_lower_bounds():
    max_exact = REL_BUCKETS // 2
    d = np.arange(REL_MAX_DIST, dtype=np.int32)
    df = np.maximum(d, 1).astype(np.float32)
    large = max_exact + (np.log(df / np.float32(max_exact)) / np.float32(math.log(REL_MAX_DIST / max_exact))
                         * np.float32(REL_BUCKETS - max_exact)).astype(np.int32)
    large = np.minimum(large, REL_BUCKETS - 1)
    bucket = np.where(d < max_exact, d, large)
    assert np.all(np.diff(bucket) >= 0) and bucket[0] == 0
    return [int(np.argmax(bucket >= b)) if np.any(bucket >= b) else REL_MAX_DIST for b in range(REL_BUCKETS)]


def _band_scale_table():
    j = np.arange(2 * CHUNK)[:, None]
    t = np.arange(CHUNK)[None, :]
    d = t + CHUNK - j
    in_window = (d >= 0) & (d < CHUNK)
    first = in_window & (j >= CHUNK)
    tab = np.stack([np.tile(in_window, (2, 1)), np.tile(first, (2, 1))]).astype(np.float32)
    return tab * np.float32(LOG2E)


def _layer_norm(x, g, b):
    mu = jnp.mean(x, axis=-1, keepdims=True)
    d = x - mu
    var = jnp.mean(d * d, axis=-1, keepdims=True)
    return d * lax.rsqrt(var + LN_EPS) * g + b


def _silu(z):
    half = 0.5 * z
    return half + half * jnp.tanh(half)


def _blocked_dot(lhs, w_ref, first, count):
    return [jnp.dot(lhs, w_ref[first + j], preferred_element_type=F32) for j in range(count)]


def _tile_bytes(shape, dtype):
    itemsize = jnp.dtype(dtype).itemsize
    sublanes = SUBLANES * 4 // itemsize
    *lead, rows, cols = (1,) + tuple(shape)
    return math.prod(lead) * (-(-rows // sublanes) * sublanes) * (-(-cols // LANES) * LANES) * itemsize


def _vmem_limit(resident, pipelined, scratch, live_values):
    size = lambda entries: sum(_tile_bytes(shape, dtype) for shape, dtype in entries)
    total = size(resident) + 2 * size(pipelined) + size(scratch) + size(live_values) + COMPILER_TEMP_BYTES
    assert total <= V7X_VMEM_BYTES, total
    return total


def _stage_rows(w_hbm):
    k, n = w_hbm.shape
    rows = SUBLANES
    while 2 * rows * n * jnp.dtype(F32).itemsize <= STAGE_BYTES and k % (2 * rows) == 0:
        rows *= 2
    assert k % rows == 0
    return rows


def _cast_weights(jobs, sems):
    def copy(k, c):
        w_hbm, _, stage = jobs[k]
        rows, slot = stage.shape[1], c % STAGE_SLOTS
        return pltpu.make_async_copy(w_hbm.at[pl.ds(c * rows, rows), :], stage.at[slot], sems.at[k, slot])

    counts = [w_hbm.shape[0] // stage.shape[1] for w_hbm, _, stage in jobs]
    for k, n in enumerate(counts):
        for c in range(min(STAGE_SLOTS - 1, n)):
            copy(k, c).start()
    for k, (_, w_s, stage) in enumerate(jobs):
        rows = stage.shape[1]
        for c in range(counts[k]):
            if c + STAGE_SLOTS - 1 < counts[k]:
                copy(k, c + STAGE_SLOTS - 1).start()
            copy(k, c).wait()
            chunk = stage[c % STAGE_SLOTS].astype(BF16)
            for j in range(w_s.shape[0]):
                w_s[j, c * rows:(c + 1) * rows, :] = chunk[:, j * WEIGHT_COLS:(j + 1) * WEIGHT_COLS]


def _rel_bias_kernel(rb_ref, sink_ref, out_ref, *, lower_bounds, pairs_per_kv):
    pair = pl.program_id(0) * pairs_per_kv + pl.program_id(1)
    j = lax.broadcasted_iota(jnp.int32, (2 * CHUNK, CHUNK), 0)
    t = lax.broadcasted_iota(jnp.int32, (2 * CHUNK, CHUNK), 1)
    dist = t + CHUNK - j
    in_window = (dist >= 0) & (dist < CHUNK)
    d = jnp.clip(dist, 0, REL_MAX_DIST - 1)
    for e in range(2):
        h = 2 * pair + e
        acc = jnp.full((2 * CHUNK, CHUNK), rb_ref[0, h], F32)
        for b in range(1, REL_BUCKETS):
            if lower_bounds[b] < REL_MAX_DIST:
                acc = jnp.where(d >= lower_bounds[b], rb_ref[b, h], acc)
        hidden = jnp.where(j == 0, sink_ref[h], NEG_INF)
        for variant in range(2):
            visible = in_window & (j >= variant * CHUNK)
            out_ref[variant, 0, e * 2 * CHUNK:(e + 1) * 2 * CHUNK, :] = jnp.where(visible, acc, hidden) * LOG2E


def _rel_bias_table(rel_bias, sinks):
    n_heads = rel_bias.shape[1]
    n_kv = n_heads // Q_PER_KV
    pairs_per_kv = Q_PER_KV // 2
    smem = pl.BlockSpec(memory_space=pltpu.SMEM)
    return pl.pallas_call(
        functools.partial(_rel_bias_kernel, lower_bounds=_bucket_lower_bounds(), pairs_per_kv=pairs_per_kv),
        grid=(n_kv, pairs_per_kv),
        in_specs=[smem, smem],
        out_specs=pl.BlockSpec((2, 1, 4 * CHUNK, CHUNK), lambda k, p: (0, k, 0, p)),
        out_shape=jax.ShapeDtypeStruct((2, n_kv, 4 * CHUNK, pairs_per_kv * CHUNK), F32),
        name="rel_bias_table",
    )(rel_bias, sinks)


def _sgu_kernel(x_ref, xn_ref, w_in_hbm, lng_ref, lnb_ref, ws_ref, bsp_ref, w_out_hbm, w_kv_hbm,
                pg_ref, pb_ref, h_ref, kv_ref, v_s, mu_s, rstd_s, w_in_ref, w_out_ref, w_kv_ref,
                st_in, st_out, st_kv, sems, *, alpha):
    tm, _ = x_ref.shape
    aw = v_s.shape[1]
    gd = aw // A_GROUPS
    n_chunks = tm // CHUNK
    n_blocks = tm // SGU_ROWS

    def project_v(src_ref, b):
        rows = slice(b * SGU_ROWS, (b + 1) * SGU_ROWS)
        blocks = aw // WEIGHT_COLS
        v = jnp.concatenate(_blocked_dot(src_ref[rows, :].astype(BF16), w_in_ref, blocks, blocks), axis=1)
        v_s[rows, :] = v
        mu = jnp.mean(v, axis=-1, keepdims=True)
        var = jnp.maximum(jnp.mean(v * v, axis=-1, keepdims=True) - mu * mu, 0.0)
        mu_s[rows, :] = jnp.broadcast_to(mu, (SGU_ROWS, LANES))
        rstd_s[rows, :] = jnp.broadcast_to(lax.rsqrt(var + LN_EPS), (SGU_ROWS, LANES))

    @pl.when(pl.program_id(0) == 0)
    def _():
        _cast_weights([(w_in_hbm, w_in_ref, st_in), (w_out_hbm, w_out_ref, st_out),
                       (w_kv_hbm, w_kv_ref, st_kv)], sems)
        for b in range(n_blocks):
            project_v(x_ref, b)

    xb = x_ref[...].astype(BF16)
    lane_tiles = gd // LANES
    mu = jnp.concatenate([mu_s[...]] * lane_tiles, axis=1)
    rstd = jnp.concatenate([rstd_s[...]] * lane_tiles, axis=1)

    row = lax.broadcasted_iota(jnp.int32, (CHUNK, CHUNK), 0)
    col = lax.broadcasted_iota(jnp.int32, (CHUNK, CHUNK), 1)
    tril = col <= row

    def gate_of(g):
        per_group = gd // WEIGHT_COLS
        u = jnp.concatenate(_blocked_dot(xb, w_in_ref, g * per_group, per_group), axis=1)
        z = jnp.concatenate(_blocked_dot(xb, w_in_ref, (2 * A_GROUPS + g) * per_group, per_group), axis=1)
        return u * _silu(z)

    def spatial(g, gate):
        cols = slice(g * gd, (g + 1) * gd)
        ws = jnp.where(tril, ws_ref[g], 0.0).astype(BF16)
        vn = ((v_s[:, cols] - mu) * rstd * lng_ref[:, cols] + lnb_ref[:, cols]).astype(BF16)
        y_chunks = []
        for c in range(n_chunks):
            rows = slice(c * CHUNK, (c + 1) * CHUNK)
            s = jnp.dot(ws, vn[rows], preferred_element_type=F32) + bsp_ref[g]
            y_chunks.append((gate[rows] * s).astype(BF16))
        return jnp.concatenate(y_chunks, axis=0)

    gates = {g: gate_of(g) for g in range(SPATIAL_LAG)}
    y_groups = []
    for g in range(A_GROUPS):
        if g + SPATIAL_LAG < A_GROUPS:
            gates[g + SPATIAL_LAG] = gate_of(g + SPATIAL_LAG)
        y_groups.append(spatial(g, gates.pop(g)))
    y = jnp.concatenate(y_groups, axis=1)

    def kv_of(b, h):
        for r in range(b * SGU_ROWS, (b + 1) * SGU_ROWS, CHUNK):
            hr = h[r - b * SGU_ROWS:r - b * SGU_ROWS + CHUNK]
            kv_ref[r:r + CHUNK, :] = jnp.concatenate(
                _blocked_dot(hr.astype(BF16), w_kv_ref, 0, w_kv_ref.shape[0]), axis=1).astype(BF16)

    h_prev = None
    for b in range(n_blocks):
        rows = slice(b * SGU_ROWS, (b + 1) * SGU_ROWS)
        sub = jnp.concatenate(_blocked_dot(y[rows], w_out_ref, 0, w_out_ref.shape[0]), axis=1)
        if h_prev is not None:
            kv_of(b - 1, h_prev)
        project_v(xn_ref, b)
        h_prev = _layer_norm(alpha * x_ref[rows, :] + sub, pg_ref[...], pb_ref[...])
        h_ref[rows, :] = h_prev
    kv_of(n_blocks - 1, h_prev)


def _sgu_layer(x2, w_in, ln_g, ln_b, w_spatial, bsp_b, w_out, w_kv, pg, pb, *, alpha, tm):
    t_tokens, d_model = x2.shape
    aw = w_out.shape[0]
    gd = aw // A_GROUPS
    kvw = w_kv.shape[1]
    n_tiles = t_tokens // tm
    whole = pl.BlockSpec(memory_space=pltpu.VMEM)
    hbm = pl.BlockSpec(memory_space=pl.ANY)
    stage = lambda w: ((STAGE_SLOTS, _stage_rows(w), w.shape[1]), F32)
    blocked = lambda w: ((w.shape[1] // WEIGHT_COLS, w.shape[0], WEIGHT_COLS), BF16)
    scratch = [((tm, aw), F32), ((tm, LANES), F32), ((tm, LANES), F32),
               blocked(w_in), blocked(w_out), blocked(w_kv), stage(w_in), stage(w_out), stage(w_kv)]
    vmem = _vmem_limit(
        resident=[(a.shape, a.dtype) for a in (ln_g, ln_b, w_spatial, bsp_b, pg, pb)],
        pipelined=[((tm, d_model), F32)] * 3 + [((tm, kvw), BF16)],
        scratch=scratch,
        live_values=[((tm, gd), F32)] * (3 + SPATIAL_LAG) + [((tm, aw), BF16)] + [((tm, d_model), F32)] * 2)
    return pl.pallas_call(
        functools.partial(_sgu_kernel, alpha=alpha),
        grid=(n_tiles,),
        in_specs=[pl.BlockSpec((tm, d_model), lambda i: (i, 0)),
                  pl.BlockSpec((tm, d_model), lambda i: (jnp.minimum(i + 1, n_tiles - 1), 0)),
                  hbm, whole, whole, whole, whole, hbm, hbm, whole, whole],
        out_specs=[pl.BlockSpec((tm, d_model), lambda i: (i, 0)),
                   pl.BlockSpec((tm, kvw), lambda i: (i, 0))],
        out_shape=[jax.ShapeDtypeStruct((t_tokens, d_model), F32),
                   jax.ShapeDtypeStruct((t_tokens, kvw), BF16)],
        scratch_shapes=[pltpu.VMEM(shape, dtype) for shape, dtype in scratch]
        + [pltpu.SemaphoreType.DMA((3, STAGE_SLOTS))],
        compiler_params=pltpu.CompilerParams(dimension_semantics=("arbitrary",), vmem_limit_bytes=vmem),
        name="sgu_layer",
    )(x2, x2, w_in, ln_g, ln_b, w_spatial, bsp_b, w_out, w_kv, pg, pb)


def _swa_kernel(h_ref, hn_ref, kv_ref, kvp_ref, w_in_hbm, bias_ref, scale_ref, w_out_hbm,
                pg_ref, pb_ref, o_ref, q_s, g_s, w_in_ref, w_out_ref, st_in, st_out, sems,
                *, alpha, tiles_per_seq):
    tm, _ = h_ref.shape
    bw = g_s.shape[1]
    n_kv = bw // (HEAD_DIM * Q_PER_KV)
    pairs_per_kv = Q_PER_KV // 2
    i = pl.program_id(0)

    n_blocks = tm // PROJ_ROWS
    chunks_per_block = PROJ_ROWS // CHUNK

    def project_q(src_ref, b):
        rows = slice(b * PROJ_ROWS, (b + 1) * PROJ_ROWS)
        q = jnp.concatenate(_blocked_dot(src_ref[rows, :].astype(BF16), w_in_ref, 0, bw // WEIGHT_COLS), axis=1)
        q = (q * (HEAD_DIM ** -0.5)).astype(BF16)
        for p in range(bw // LANES):
            q_s[p, rows, :] = q[:, p * LANES:(p + 1) * LANES]

    def project_z(src_ref, b):
        rows = slice(b * PROJ_ROWS, (b + 1) * PROJ_ROWS)
        blocks = bw // WEIGHT_COLS
        g_s[rows, :] = _silu(jnp.concatenate(
            _blocked_dot(src_ref[rows, :].astype(BF16), w_in_ref, blocks, blocks), axis=1))

    @pl.when(i == 0)
    def _():
        _cast_weights([(w_in_hbm, w_in_ref, st_in), (w_out_hbm, w_out_ref, st_out)], sems)
        for b in range(n_blocks):
            project_q(h_ref, b)
            project_z(h_ref, b)

    as_bf16 = lambda a: a.astype(F32).astype(BF16)
    lo = as_bf16(lax.broadcasted_iota(jnp.int32, (2 * CHUNK, LANES), 1)) < HEAD_DIM
    key0 = as_bf16(lax.broadcasted_iota(jnp.int32, (2 * CHUNK, LANES), 0)) < 1
    zeros_k = jnp.zeros((2 * CHUNK, LANES), BF16)
    zeros_v = jnp.zeros((HEAD_DIM, 2 * CHUNK), BF16)
    zeros_d = jnp.zeros((DEN_ROWS, 2 * CHUNK), BF16)
    ones_d = jnp.ones((DEN_ROWS, 2 * CHUNK), BF16)
    first = (i % tiles_per_seq == 0).astype(jnp.int32)

    bands = {}

    def band_of(c):
        if c not in bands:
            rows = slice(c * CHUNK, (c + 1) * CHUNK)
            kv_prev = kvp_ref[...] if c == 0 else kv_ref[(c - 1) * CHUNK:c * CHUNK, :]
            band = jnp.concatenate([kv_prev, kv_ref[rows, :]], axis=0)
            kb, vb = band[:, :LANES], band[:, LANES:]
            kr = pltpu.roll(kb, HEAD_DIM, axis=1)
            v_t = jnp.where(key0, zeros_k, vb).astype(F32).T.astype(BF16)
            bands[c] = (kb, kr, v_t)
        return bands[c]

    def scores_of(c, kvh):
        rows = slice(c * CHUNK, (c + 1) * CHUNK)
        kb, kr, v_t = band_of(c)
        variant = first if c == 0 else 0
        k_src, k_alt = (kb, kr) if kvh == 0 else (kr, kb)
        k_cat = jnp.concatenate([jnp.where(lo, k_src, zeros_k), jnp.where(lo, zeros_k, k_alt)], axis=0)
        vk = v_t[kvh * HEAD_DIM:(kvh + 1) * HEAD_DIM]
        v_lhs = jnp.concatenate(
            [jnp.concatenate([vk, zeros_v], axis=1), jnp.concatenate([zeros_v, vk], axis=1),
             jnp.concatenate([ones_d, zeros_d], axis=1), jnp.concatenate([zeros_d, ones_d], axis=1)],
            axis=0)
        pair0 = kvh * pairs_per_kv
        q_stack = jnp.concatenate(
            [q_s[pair0 + p, rows, :] for p in range(pairs_per_kv)], axis=0)
        s_t = lax.dot_general(k_cat, q_stack, (((1,), (1,)), ((), ())),
                              preferred_element_type=F32)
        return rows, variant, kvh, s_t, v_lhs

    def attend(rows, variant, kvh, s_t, v_lhs):
        pair0 = kvh * pairs_per_kv
        e_cols = []
        for p in range(pairs_per_kv):
            qc = slice(p * CHUNK, (p + 1) * CHUNK)
            halves = []
            for e in range(2):
                kr_ = slice(e * 2 * CHUNK, (e + 1) * 2 * CHUNK)
                le = s_t[kr_, qc] * scale_ref[variant, kr_, :] + bias_ref[variant, kvh, kr_, qc]
                halves.append(jnp.exp2(le - jnp.max(le, axis=0, keepdims=True)).astype(BF16))
            e_cols.append(jnp.concatenate(halves, axis=0))
        pv_t = jnp.dot(v_lhs, jnp.concatenate(e_cols, axis=1), preferred_element_type=F32)
        n_num = 2 * HEAD_DIM
        inv_e = 1.0 / pv_t[n_num:n_num + SUBLANES]
        inv_o = 1.0 / pv_t[n_num + DEN_ROWS:n_num + DEN_ROWS + SUBLANES]
        inv = jnp.concatenate([inv_e] * (HEAD_DIM // SUBLANES) + [inv_o] * (HEAD_DIM // SUBLANES), axis=0)
        o_t = pv_t[:n_num] * inv
        pieces = []
        for p in range(pairs_per_kv):
            pc = slice((pair0 + p) * LANES, (pair0 + p + 1) * LANES)
            pieces.append((o_t[:, p * CHUNK:(p + 1) * CHUNK].T * g_s[rows, pc]).astype(BF16))
        return jnp.concatenate(pieces, axis=1)

    block_units = [[(c, kvh) for c in range(b * chunks_per_block, (b + 1) * chunks_per_block)
                    for kvh in range(n_kv)] for b in range(n_blocks)]
    scored = [scores_of(*u) for u in block_units[0]]
    for b in range(n_blocks):
        rows = slice(b * PROJ_ROWS, (b + 1) * PROJ_ROWS)
        scored_next = []
        y_parts = []
        for j, unit in enumerate(scored):
            y_parts.append(attend(*unit))
            if b + 1 < n_blocks:
                scored_next.append(scores_of(*block_units[b + 1][j]))
        scored = scored_next
        project_q(hn_ref, b)
        y = jnp.concatenate([jnp.concatenate(y_parts[c * n_kv:(c + 1) * n_kv], axis=1)
                             for c in range(chunks_per_block)], axis=0)
        sub = jnp.concatenate(_blocked_dot(y, w_out_ref, 0, w_out_ref.shape[0]), axis=1)
        project_z(hn_ref, b)
        o_ref[rows, :] = _layer_norm(alpha * h_ref[rows, :] + sub, pg_ref[...], pb_ref[...])


def _swa_layer(h1, kv, w_in, bias_tab, scale_tab, w_out, pg, pb, *, alpha, tm, seq):
    t_tokens, d_model = h1.shape
    bw = w_out.shape[0]
    kvw2 = kv.shape[1]
    cpt = tm // CHUNK
    n_tiles = t_tokens // tm
    n_units = 2 * (PROJ_ROWS // CHUNK) * (bw // (HEAD_DIM * Q_PER_KV))
    whole = pl.BlockSpec(memory_space=pltpu.VMEM)
    hbm = pl.BlockSpec(memory_space=pl.ANY)
    stage = lambda w: ((STAGE_SLOTS, _stage_rows(w), w.shape[1]), F32)
    blocked = lambda w: ((w.shape[1] // WEIGHT_COLS, w.shape[0], WEIGHT_COLS), BF16)
    scratch = [((bw // LANES, tm, LANES), BF16), ((tm, bw), F32),
               blocked(w_in), blocked(w_out), stage(w_in), stage(w_out)]
    vmem = _vmem_limit(
        resident=[(a.shape, a.dtype) for a in (bias_tab, scale_tab, pg, pb)],
        pipelined=[((tm, d_model), F32)] * 3 + [((tm, kvw2), BF16), ((CHUNK, kvw2), BF16)],
        scratch=scratch,
        live_values=[((PROJ_ROWS, bw), F32)] * 2 + [((PROJ_ROWS, bw), BF16)] + [((PROJ_ROWS, d_model), F32)] * 2
        + [((n_units, 4 * CHUNK, 4 * CHUNK), F32), ((4 * CHUNK, 4 * CHUNK), BF16)])
    return pl.pallas_call(
        functools.partial(_swa_kernel, alpha=alpha, tiles_per_seq=seq // tm),
        grid=(n_tiles,),
        in_specs=[pl.BlockSpec((tm, d_model), lambda i: (i, 0)),
                  pl.BlockSpec((tm, d_model), lambda i: (jnp.minimum(i + 1, n_tiles - 1), 0)),
                  pl.BlockSpec((tm, kvw2), lambda i: (i, 0)),
                  pl.BlockSpec((CHUNK, kvw2), lambda i: (jnp.maximum(i * cpt - 1, 0), 0)),
                  hbm, whole, whole, hbm, whole, whole],
        out_specs=pl.BlockSpec((tm, d_model), lambda i: (i, 0)),
        out_shape=jax.ShapeDtypeStruct((t_tokens, d_model), F32),
        scratch_shapes=[pltpu.VMEM(shape, dtype) for shape, dtype in scratch]
        + [pltpu.SemaphoreType.DMA((2, STAGE_SLOTS))],
        compiler_params=pltpu.CompilerParams(dimension_semantics=("arbitrary",), vmem_limit_bytes=vmem),
        name="swa_layer",
    )(h1, h1, kv, kv, w_in, bias_tab, scale_tab, w_out, pg, pb)


def kernel(x, w_in_a, sgu_ln_g, sgu_ln_b, w_spatial, b_spatial, w_out_a, w_kv, w_in_b, attn_sinks,
           rel_bias, w_out_b, post_ln_g, post_ln_b):
    bsz, seq, d_model = x.shape
    depth = post_ln_g.shape[0]
    assert w_in_a.shape[0] == 1 and w_in_b.shape[0] == 1 and depth == 2
    assert seq % CHUNK == 0 and w_spatial.shape[1:] == (A_GROUPS, CHUNK, CHUNK)
    assert w_kv.shape[1] == 2 * LANES and rel_bias.shape == (REL_BUCKETS, w_out_b.shape[1] // HEAD_DIM)
    alpha = (2.0 * depth) ** 0.25
    aw = w_out_a.shape[1]
    gd = aw // A_GROUPS
    tm = TOKEN_TILE
    assert seq % tm == 0

    x2 = x.reshape(bsz * seq, d_model)
    row = lambda a: a.reshape(1, -1)
    bsp_b = jnp.broadcast_to(b_spatial[0][:, :, None], (A_GROUPS, CHUNK, gd))
    h1, kv = _sgu_layer(x2, w_in_a[0], row(sgu_ln_g[0]), row(sgu_ln_b[0]), w_spatial[0], bsp_b,
                        w_out_a[0], w_kv, row(post_ln_g[0]), row(post_ln_b[0]),
                        alpha=alpha, tm=tm)
    bias_tab = _rel_bias_table(rel_bias, attn_sinks[0])
    out = _swa_layer(h1, kv, w_in_b[0], bias_tab, jnp.asarray(_band_scale_table()),
                     w_out_b[0], row(post_ln_g[1]), row(post_ln_b[1]),
                     alpha=alpha, tm=tm, seq=seq)
    return out.reshape(bsz, seq, d_model)
```

```python
import functools
import math

import numpy as np
import jax
import jax.numpy as jnp
from jax import lax
from jax.experimental import pallas as pl
from jax.experimental.pallas import tpu as pltpu

F32 = jnp.float32
BF16 = jnp.bfloat16

CHUNK = 128
A_GROUPS = 8
HEAD_DIM = 64
Q_PER_KV = 8
REL_BUCKETS = 32
REL_MAX_DIST = 128
LN_EPS = 1e-5
NEG_INF = -1e30
LOG2E = math.log2(math.e)

LANES = 128
SUBLANES = 8
PROJ_ROWS = 2 * CHUNK
TOKEN_TILE = 4 * CHUNK
SGU_ROWS = TOKEN_TILE
WEIGHT_COLS = 2 * LANES
DEN_ROWS = 16
SPATIAL_LAG = 1
V7X_VMEM_BYTES = 64 * 1024 * 1024
COMPILER_TEMP_BYTES = 8 * 1024 * 1024
STAGE_BYTES = 256 * 1024
STAGE_SLOTS = 8


def _bucket_lower_bounds():
    max_exact = REL_BUCKETS // 2
    d = np.arange(REL_MAX_DIST, dtype=np.int32)
    df = np.maximum(d, 1).astype(np.float32)
    large = max_exact + (np.log(df / np.float32(max_exact)) / np.float32(math.log(REL_MAX_DIST / max_exact))
                         * np.float32(REL_BUCKETS - max_exact)).astype(np.int32)
    large = np.minimum(large, REL_BUCKETS - 1)
    bucket = np.where(d < max_exact, d, large)
    assert np.all(np.diff(bucket) >= 0) and bucket[0] == 0
    return [int(np.argmax(bucket >= b)) if np.any(bucket >= b) else REL_MAX_DIST for b in range(REL_BUCKETS)]


def _band_scale_table():
    j = np.arange(2 * CHUNK)[:, None]
    t = np.arange(CHUNK)[None, :]
    d = t + CHUNK - j
    in_window = (d >= 0) & (d < CHUNK)
    first = in_window & (j >= CHUNK)
    tab = np.stack([np.tile(in_window, (2, 1)), np.tile(first, (2, 1))]).astype(np.float32)
    return tab * np.float32(LOG2E)


def _layer_norm(x, g, b):
    mu = jnp.mean(x, axis=-1, keepdims=True)
    d = x - mu
    var = jnp.mean(d * d, axis=-1, keepdims=True)
    return d * lax.rsqrt(var + LN_EPS) * g + b


def _silu(z):
    half = 0.5 * z
    return half + half * jnp.tanh(half)


def _blocked_dot(lhs, w_ref, first, count):
    return [jnp.dot(lhs, w_ref[first + j], preferred_element_type=F32) for j in range(count)]


def _tile_bytes(shape, dtype):
    itemsize = jnp.dtype(dtype).itemsize
    sublanes = SUBLANES * 4 // itemsize
    *lead, rows, cols = (1,) + tuple(shape)
    return math.prod(lead) * (-(-rows // sublanes) * sublanes) * (-(-cols // LANES) * LANES) * itemsize


def _vmem_limit(resident, pipelined, scratch, live_values):
    size = lambda entries: sum(_tile_bytes(shape, dtype) for shape, dtype in entries)
    total = size(resident) + 2 * size(pipelined) + size(scratch) + size(live_values) + COMPILER_TEMP_BYTES
    assert total <= V7X_VMEM_BYTES, total
    return total


def _stage_rows(w_hbm):
    k, n = w_hbm.shape
    rows = SUBLANES
    while 2 * rows * n * jnp.dtype(F32).itemsize <= STAGE_BYTES and k % (2 * rows) == 0:
        rows *= 2
    assert k % rows == 0
    return rows


def _cast_weight(w_hbm, w_s, stage, sems, k):
    rows = stage.shape[1]
    n = w_hbm.shape[0] // rows

    def copy(c):
        slot = c % STAGE_SLOTS
        return pltpu.make_async_copy(w_hbm.at[pl.ds(c * rows, rows), :], stage.at[slot], sems.at[k, slot])

    for c in range(min(STAGE_SLOTS - 1, n)):
        copy(c).start()
    for c in range(n):
        if c + STAGE_SLOTS - 1 < n:
            copy(c + STAGE_SLOTS - 1).start()
        copy(c).wait()
        chunk = stage[c % STAGE_SLOTS].astype(BF16)
        for j in range(w_s.shape[0]):
            w_s[j, c * rows:(c + 1) * rows, :] = chunk[:, j * WEIGHT_COLS:(j + 1) * WEIGHT_COLS]


def _rel_bias_kernel(rb_ref, sink_ref, out_ref, *, lower_bounds, pairs_per_kv):
    pair = pl.program_id(0) * pairs_per_kv + pl.program_id(1)
    j = lax.broadcasted_iota(jnp.int32, (2 * CHUNK, CHUNK), 0)
    t = lax.broadcasted_iota(jnp.int32, (2 * CHUNK, CHUNK), 1)
    dist = t + CHUNK - j
    in_window = (dist >= 0) & (dist < CHUNK)
    d = jnp.clip(dist, 0, REL_MAX_DIST - 1)
    for e in range(2):
        h = 2 * pair + e
        acc = jnp.full((2 * CHUNK, CHUNK), rb_ref[0, h], F32)
        for b in range(1, REL_BUCKETS):
            if lower_bounds[b] < REL_MAX_DIST:
                acc = jnp.where(d >= lower_bounds[b], rb_ref[b, h], acc)
        hidden = jnp.where(j == 0, sink_ref[h], NEG_INF)
        for variant in range(2):
            visible = in_window & (j >= variant * CHUNK)
            out_ref[variant, 0, e * 2 * CHUNK:(e + 1) * 2 * CHUNK, :] = jnp.where(visible, acc, hidden) * LOG2E


def _rel_bias_table(rel_bias, sinks):
    n_heads = rel_bias.shape[1]
    n_kv = n_heads // Q_PER_KV
    pairs_per_kv = Q_PER_KV // 2
    smem = pl.BlockSpec(memory_space=pltpu.SMEM)
    return pl.pallas_call(
        functools.partial(_rel_bias_kernel, lower_bounds=_bucket_lower_bounds(), pairs_per_kv=pairs_per_kv),
        grid=(n_kv, pairs_per_kv),
        in_specs=[smem, smem],
        out_specs=pl.BlockSpec((2, 1, 4 * CHUNK, CHUNK), lambda k, p: (0, k, 0, p)),
        out_shape=jax.ShapeDtypeStruct((2, n_kv, 4 * CHUNK, pairs_per_kv * CHUNK), F32),
        name="rel_bias_table",
    )(rel_bias, sinks)


def _sgu_kernel(x_ref, xn_ref, w_in_hbm, lng_ref, lnb_ref, ws_ref, bsp_ref, w_out_hbm,
                pg_ref, pb_ref, h_ref, v_s, mu_s, rstd_s, w_in_ref, w_out_ref,
                st_in, st_out, sems, *, alpha):
    tm, _ = x_ref.shape
    aw = v_s.shape[1]
    gd = aw // A_GROUPS
    n_chunks = tm // CHUNK
    n_blocks = tm // SGU_ROWS

    def project_v(src_ref, b):
        rows = slice(b * SGU_ROWS, (b + 1) * SGU_ROWS)
        blocks = aw // WEIGHT_COLS
        v = jnp.concatenate(_blocked_dot(src_ref[rows, :].astype(BF16), w_in_ref, blocks, blocks), axis=1)
        v_s[rows, :] = v
        mu = jnp.mean(v, axis=-1, keepdims=True)
        var = jnp.maximum(jnp.mean(v * v, axis=-1, keepdims=True) - mu * mu, 0.0)
        mu_s[rows, :] = jnp.broadcast_to(mu, (SGU_ROWS, LANES))
        rstd_s[rows, :] = jnp.broadcast_to(lax.rsqrt(var + LN_EPS), (SGU_ROWS, LANES))

    @pl.when(pl.program_id(0) == 0)
    def _():
        _cast_weight(w_in_hbm, w_in_ref, st_in, sems, 0)
        _cast_weight(w_out_hbm, w_out_ref, st_out, sems, 1)
        for b in range(n_blocks):
            project_v(x_ref, b)

    xb = x_ref[...].astype(BF16)
    lane_tiles = gd // LANES
    mu = jnp.concatenate([mu_s[...]] * lane_tiles, axis=1)
    rstd = jnp.concatenate([rstd_s[...]] * lane_tiles, axis=1)

    row = lax.broadcasted_iota(jnp.int32, (CHUNK, CHUNK), 0)
    col = lax.broadcasted_iota(jnp.int32, (CHUNK, CHUNK), 1)
    tril = col <= row

    def gate_of(g):
        per_group = gd // WEIGHT_COLS
        u = jnp.concatenate(_blocked_dot(xb, w_in_ref, g * per_group, per_group), axis=1)
        z = jnp.concatenate(_blocked_dot(xb, w_in_ref, (2 * A_GROUPS + g) * per_group, per_group), axis=1)
        return u * _silu(z)

    def spatial(g, gate):
        cols = slice(g * gd, (g + 1) * gd)
        ws = jnp.where(tril, ws_ref[g], 0.0).astype(BF16)
        vn = ((v_s[:, cols] - mu) * rstd * lng_ref[:, cols] + lnb_ref[:, cols]).astype(BF16)
        y_chunks = []
        for c in range(n_chunks):
            rows = slice(c * CHUNK, (c + 1) * CHUNK)
            s = jnp.dot(ws, vn[rows], preferred_element_type=F32) + bsp_ref[g]
            y_chunks.append((gate[rows] * s).astype(BF16))
        return jnp.concatenate(y_chunks, axis=0)

    gates = {g: gate_of(g) for g in range(SPATIAL_LAG)}
    y_groups = []
    for g in range(A_GROUPS):
        if g + SPATIAL_LAG < A_GROUPS:
            gates[g + SPATIAL_LAG] = gate_of(g + SPATIAL_LAG)
        y_groups.append(spatial(g, gates.pop(g)))
    y = jnp.concatenate(y_groups, axis=1)

    for b in range(n_blocks):
        rows = slice(b * SGU_ROWS, (b + 1) * SGU_ROWS)
        sub = jnp.concatenate(_blocked_dot(y[rows], w_out_ref, 0, w_out_ref.shape[0]), axis=1)
        project_v(xn_ref, b)
        h_ref[rows, :] = _layer_norm(alpha * x_ref[rows, :] + sub, pg_ref[...], pb_ref[...])


def _sgu_layer(x2, w_in, ln_g, ln_b, w_spatial, bsp_b, w_out, pg, pb, *, alpha, tm):
    t_tokens, d_model = x2.shape
    aw = w_out.shape[0]
    gd = aw // A_GROUPS
    n_tiles = t_tokens // tm
    whole = pl.BlockSpec(memory_space=pltpu.VMEM)
    hbm = pl.BlockSpec(memory_space=pl.ANY)
    stage = lambda w: ((STAGE_SLOTS, _stage_rows(w), w.shape[1]), F32)
    blocked = lambda w: ((w.shape[1] // WEIGHT_COLS, w.shape[0], WEIGHT_COLS), BF16)
    scratch = [((tm, aw), F32), ((tm, LANES), F32), ((tm, LANES), F32),
               blocked(w_in), blocked(w_out), stage(w_in), stage(w_out)]
    vmem = _vmem_limit(
        resident=[(a.shape, a.dtype) for a in (ln_g, ln_b, w_spatial, bsp_b, pg, pb)],
        pipelined=[((tm, d_model), F32)] * 3,
        scratch=scratch,
        live_values=[((tm, gd), F32)] * (3 + SPATIAL_LAG) + [((tm, aw), BF16)] + [((tm, d_model), F32)] * 2)
    return pl.pallas_call(
        functools.partial(_sgu_kernel, alpha=alpha),
        grid=(n_tiles,),
        in_specs=[pl.BlockSpec((tm, d_model), lambda i: (i, 0)),
                  pl.BlockSpec((tm, d_model), lambda i: (jnp.minimum(i + 1, n_tiles - 1), 0)),
                  hbm, whole, whole, whole, whole, hbm, whole, whole],
        out_specs=pl.BlockSpec((tm, d_model), lambda i: (i, 0)),
        out_shape=jax.ShapeDtypeStruct((t_tokens, d_model), F32),
        scratch_shapes=[pltpu.VMEM(shape, dtype) for shape, dtype in scratch]
        + [pltpu.SemaphoreType.DMA((2, STAGE_SLOTS))],
        compiler_params=pltpu.CompilerParams(dimension_semantics=("arbitrary",), vmem_limit_bytes=vmem),
        name="sgu_layer",
    )(x2, x2, w_in, ln_g, ln_b, w_spatial, bsp_b, w_out, pg, pb)


def _swa_kernel(h_ref, hn_ref, w_in_hbm, w_kv_hbm, bias_ref, scale_ref, w_out_hbm,
                pg_ref, pb_ref, o_ref, q_s, g_s, kv_s, w_in_ref, w_kv_ref, w_out_ref, st_in, st_kv, st_out, sems,
                *, alpha, tiles_per_seq):
    tm, _ = h_ref.shape
    bw = g_s.shape[1]
    n_kv = bw // (HEAD_DIM * Q_PER_KV)
    pairs_per_kv = Q_PER_KV // 2
    i = pl.program_id(0)

    n_blocks = tm // PROJ_ROWS
    chunks_per_block = PROJ_ROWS // CHUNK

    def project_q(src_ref, b):
        rows = slice(b * PROJ_ROWS, (b + 1) * PROJ_ROWS)
        q = jnp.concatenate(_blocked_dot(src_ref[rows, :].astype(BF16), w_in_ref, 0, bw // WEIGHT_COLS), axis=1)
        q = (q * (HEAD_DIM ** -0.5)).astype(BF16)
        for p in range(bw // LANES):
            q_s[p, rows, :] = q[:, p * LANES:(p + 1) * LANES]

    def project_kv(src_ref, b):
        rows = slice(b * PROJ_ROWS, (b + 1) * PROJ_ROWS)
        kv = jnp.concatenate(_blocked_dot(src_ref[rows, :].astype(BF16), w_kv_ref, 0, w_kv_ref.shape[0]), axis=1)
        kv_s[CHUNK + b * PROJ_ROWS:CHUNK + (b + 1) * PROJ_ROWS, :] = kv.astype(BF16)

    def project_z(src_ref, b):
        rows = slice(b * PROJ_ROWS, (b + 1) * PROJ_ROWS)
        blocks = bw // WEIGHT_COLS
        g_s[rows, :] = _silu(jnp.concatenate(
            _blocked_dot(src_ref[rows, :].astype(BF16), w_in_ref, blocks, blocks), axis=1))

    @pl.when(i == 0)
    def _():
        _cast_weight(w_in_hbm, w_in_ref, st_in, sems, 0)
        _cast_weight(w_out_hbm, w_out_ref, st_out, sems, 1)
        _cast_weight(w_kv_hbm, w_kv_ref, st_kv, sems, 2)
        kv_s[:CHUNK, :] = jnp.zeros((CHUNK, kv_s.shape[1]), BF16)
        for b in range(n_blocks):
            project_q(h_ref, b)
            project_z(h_ref, b)
            project_kv(h_ref, b)

    as_bf16 = lambda a: a.astype(F32).astype(BF16)
    lo = as_bf16(lax.broadcasted_iota(jnp.int32, (2 * CHUNK, LANES), 1)) < HEAD_DIM
    key0 = as_bf16(lax.broadcasted_iota(jnp.int32, (2 * CHUNK, LANES), 0)) < 1
    zeros_k = jnp.zeros((2 * CHUNK, LANES), BF16)
    zeros_v = jnp.zeros((HEAD_DIM, 2 * CHUNK), BF16)
    zeros_d = jnp.zeros((DEN_ROWS, 2 * CHUNK), BF16)
    ones_d = jnp.ones((DEN_ROWS, 2 * CHUNK), BF16)
    first = (i % tiles_per_seq == 0).astype(jnp.int32)

    bands = {}

    def band_of(c):
        if c not in bands:
            band = kv_s[c * CHUNK:(c + 2) * CHUNK, :]
            kb, vb = band[:, :LANES], band[:, LANES:]
            kr = pltpu.roll(kb, HEAD_DIM, axis=1)
            v_t = jnp.where(key0, zeros_k, vb).astype(F32).T.astype(BF16)
            bands[c] = (kb, kr, v_t)
        return bands[c]

    def scores_of(c, kvh):
        rows = slice(c * CHUNK, (c + 1) * CHUNK)
        kb, kr, v_t = band_of(c)
        variant = first if c == 0 else 0
        k_src, k_alt = (kb, kr) if kvh == 0 else (kr, kb)
        k_cat = jnp.concatenate([jnp.where(lo, k_src, zeros_k), jnp.where(lo, zeros_k, k_alt)], axis=0)
        vk = v_t[kvh * HEAD_DIM:(kvh + 1) * HEAD_DIM]
        v_lhs = jnp.concatenate(
            [jnp.concatenate([vk, zeros_v], axis=1), jnp.concatenate([zeros_v, vk], axis=1),
             jnp.concatenate([ones_d, zeros_d], axis=1), jnp.concatenate([zeros_d, ones_d], axis=1)],
            axis=0)
        pair0 = kvh * pairs_per_kv
        q_stack = jnp.concatenate(
            [q_s[pair0 + p, rows, :] for p in range(pairs_per_kv)], axis=0)
        s_t = lax.dot_general(k_cat, q_stack, (((1,), (1,)), ((), ())),
                              preferred_element_type=F32)
        return rows, variant, kvh, s_t, v_lhs

    def attend(rows, variant, kvh, s_t, v_lhs):
        pair0 = kvh * pairs_per_kv
        e_cols = []
        for p in range(pairs_per_kv):
            qc = slice(p * CHUNK, (p + 1) * CHUNK)
            halves = []
            for e in range(2):
                kr_ = slice(e * 2 * CHUNK, (e + 1) * 2 * CHUNK)
                le = s_t[kr_, qc] * scale_ref[variant, kr_, :] + bias_ref[variant, kvh, kr_, qc]
                halves.append(jnp.exp2(le - jnp.max(le, axis=0, keepdims=True)).astype(BF16))
            e_cols.append(jnp.concatenate(halves, axis=0))
        pv_t = jnp.dot(v_lhs, jnp.concatenate(e_cols, axis=1), preferred_element_type=F32)
        n_num = 2 * HEAD_DIM
        inv_e = 1.0 / pv_t[n_num:n_num + SUBLANES]
        inv_o = 1.0 / pv_t[n_num + DEN_ROWS:n_num + DEN_ROWS + SUBLANES]
        inv = jnp.concatenate([inv_e] * (HEAD_DIM // SUBLANES) + [inv_o] * (HEAD_DIM // SUBLANES), axis=0)
        o_t = pv_t[:n_num] * inv
        pieces = []
        for p in range(pairs_per_kv):
            pc = slice((pair0 + p) * LANES, (pair0 + p + 1) * LANES)
            pieces.append((o_t[:, p * CHUNK:(p + 1) * CHUNK].T * g_s[rows, pc]).astype(BF16))
        return jnp.concatenate(pieces, axis=1)

    block_units = [[(c, kvh) for c in range(b * chunks_per_block, (b + 1) * chunks_per_block)
                    for kvh in range(n_kv)] for b in range(n_blocks)]
    scored = [scores_of(*u) for u in block_units[0]]
    for b in range(n_blocks):
        rows = slice(b * PROJ_ROWS, (b + 1) * PROJ_ROWS)
        scored_next = []
        y_parts = []
        for j, unit in enumerate(scored):
            y_parts.append(attend(*unit))
            if b + 1 < n_blocks:
                scored_next.append(scores_of(*block_units[b + 1][j]))
        scored = scored_next
        project_q(hn_ref, b)
        y = jnp.concatenate([jnp.concatenate(y_parts[c * n_kv:(c + 1) * n_kv], axis=1)
                             for c in range(chunks_per_block)], axis=0)
        sub = jnp.concatenate(_blocked_dot(y, w_out_ref, 0, w_out_ref.shape[0]), axis=1)
        project_z(hn_ref, b)
        o_ref[rows, :] = _layer_norm(alpha * h_ref[rows, :] + sub, pg_ref[...], pb_ref[...])
    kv_s[:CHUNK, :] = kv_s[tm:tm + CHUNK, :]
    for b in range(n_blocks):
        project_kv(hn_ref, b)


def _swa_layer(h1, w_in, w_kv, bias_tab, scale_tab, w_out, pg, pb, *, alpha, tm, seq):
    t_tokens, d_model = h1.shape
    bw = w_out.shape[0]
    kvw2 = w_kv.shape[1]
    n_tiles = t_tokens // tm
    n_units = 2 * (PROJ_ROWS // CHUNK) * (bw // (HEAD_DIM * Q_PER_KV))
    whole = pl.BlockSpec(memory_space=pltpu.VMEM)
    hbm = pl.BlockSpec(memory_space=pl.ANY)
    stage = lambda w: ((STAGE_SLOTS, _stage_rows(w), w.shape[1]), F32)
    blocked = lambda w: ((w.shape[1] // WEIGHT_COLS, w.shape[0], WEIGHT_COLS), BF16)
    scratch = [((bw // LANES, tm, LANES), BF16), ((tm, bw), F32), ((CHUNK + tm, kvw2), BF16),
               blocked(w_in), blocked(w_kv), blocked(w_out), stage(w_in), stage(w_kv), stage(w_out)]
    vmem = _vmem_limit(
        resident=[(a.shape, a.dtype) for a in (bias_tab, scale_tab, pg, pb)],
        pipelined=[((tm, d_model), F32)] * 3,
        scratch=scratch,
        live_values=[((PROJ_ROWS, bw), F32)] * 2 + [((PROJ_ROWS, bw), BF16)] + [((PROJ_ROWS, d_model), F32)] * 2
        + [((n_units, 4 * CHUNK, 4 * CHUNK), F32), ((4 * CHUNK, 4 * CHUNK), BF16)])
    return pl.pallas_call(
        functools.partial(_swa_kernel, alpha=alpha, tiles_per_seq=seq // tm),
        grid=(n_tiles,),
        in_specs=[pl.BlockSpec((tm, d_model), lambda i: (i, 0)),
                  pl.BlockSpec((tm, d_model), lambda i: (jnp.minimum(i + 1, n_tiles - 1), 0)),
                  hbm, hbm, whole, whole, hbm, whole, whole],
        out_specs=pl.BlockSpec((tm, d_model), lambda i: (i, 0)),
        out_shape=jax.ShapeDtypeStruct((t_tokens, d_model), F32),
        scratch_shapes=[pltpu.VMEM(shape, dtype) for shape, dtype in scratch]
        + [pltpu.SemaphoreType.DMA((3, STAGE_SLOTS))],
        compiler_params=pltpu.CompilerParams(dimension_semantics=("arbitrary",), vmem_limit_bytes=vmem),
        name="swa_layer",
    )(h1, h1, w_in, w_kv, bias_tab, scale_tab, w_out, pg, pb)


def kernel(x, w_in_a, sgu_ln_g, sgu_ln_b, w_spatial, b_spatial, w_out_a, w_kv, w_in_b, attn_sinks,
           rel_bias, w_out_b, post_ln_g, post_ln_b):
    bsz, seq, d_model = x.shape
    depth = post_ln_g.shape[0]
    assert w_in_a.shape[0] == 1 and w_in_b.shape[0] == 1 and depth == 2
    assert seq % CHUNK == 0 and w_spatial.shape[1:] == (A_GROUPS, CHUNK, CHUNK)
    assert w_kv.shape[1] == 2 * LANES and rel_bias.shape == (REL_BUCKETS, w_out_b.shape[1] // HEAD_DIM)
    alpha = (2.0 * depth) ** 0.25
    aw = w_out_a.shape[1]
    gd = aw // A_GROUPS
    tm = TOKEN_TILE
    assert seq % tm == 0

    x2 = x.reshape(bsz * seq, d_model)
    row = lambda a: a.reshape(1, -1)
    bsp_b = jnp.broadcast_to(b_spatial[0][:, :, None], (A_GROUPS, CHUNK, gd))
    h1 = _sgu_layer(x2, w_in_a[0], row(sgu_ln_g[0]), row(sgu_ln_b[0]), w_spatial[0], bsp_b,
                    w_out_a[0], row(post_ln_g[0]), row(post_ln_b[0]), alpha=alpha, tm=tm)
    bias_tab = _rel_bias_table(rel_bias, attn_sinks[0])
    out = _swa_layer(h1, w_in_b[0], w_kv, bias_tab, jnp.asarray(_band_scale_table()),
                     w_out_b[0], row(post_ln_g[1]), row(post_ln_b[1]),
                     alpha=alpha, tm=tm, seq=seq)
    return out.reshape(bsz, seq, d_model)
```
